```python
import jax, jax.numpy as jnp
from jax import lax
import numpy as np

D_MODEL = 1024
BATCH = 8
SEQ = 8192
DEPTH = 1

N_META = 16
BLOCK = 128
PAD = BLOCK - N_META
HEAD_DIM = 64
SB_HEADS = 8
SB_WIDTH = SB_HEADS * HEAD_DIM
SWA_Q_HEADS = 16
SWA_KV_HEADS = 2
SWA_GROUP = SWA_Q_HEADS // SWA_KV_HEADS
SWA_WIDTH = SWA_Q_HEADS * HEAD_DIM
SWA_KV_WIDTH = SWA_KV_HEADS * HEAD_DIM
WINDOW = 128
ROPE_THETA = 10000.0
RMS_EPS = 1e-6
SPLITS = (SB_WIDTH, SB_WIDTH, SB_WIDTH, SWA_WIDTH, SWA_KV_WIDTH, SWA_KV_WIDTH, SB_WIDTH, SWA_WIDTH, D_MODEL, D_MODEL)
IN_COLS = sum(SPLITS)

kernel_name = "hybrid_stickbreak_swa_sink_gated"


def rms_norm(x, g):
    xf = x.astype(jnp.float32)
    y = xf * lax.rsqrt(jnp.mean(xf * xf, axis=-1, keepdims=True) + RMS_EPS)
    return (y * g.astype(jnp.float32)).astype(x.dtype)


def rope(x, pos):
    half = HEAD_DIM // 2
    inv = ROPE_THETA ** (-jnp.arange(half, dtype=jnp.float32) / half)
    ang = pos.astype(jnp.float32)[:, None] * inv[None, :]
    cos = jnp.cos(ang)[None, :, None, :]
    sin = jnp.sin(ang)[None, :, None, :]
    x1 = x[..., :half].astype(jnp.float32)
    x2 = x[..., half:].astype(jnp.float32)
    out = jnp.concatenate([x1 * cos - x2 * sin, x2 * cos + x1 * sin], axis=-1)
    return out.astype(x.dtype)


def stick_breaking_attention(q, k, v, valid):
    B, L, H, d = q.shape
    scale = d ** -0.5
    outs = []
    for blk in range(L // BLOCK):
        q0 = blk * BLOCK
        end = q0 + BLOCK
        z = jnp.einsum('bqhd,bkhd->bhqk', q[:, q0:end], k[:, :end],
                       preferred_element_type=jnp.float32) * scale
        t = q0 + jnp.arange(BLOCK)
        s = jnp.arange(end)
        mask = (s[None, :] < t[:, None]) & valid[None, :end]
        log_beta = jax.nn.log_sigmoid(z)
        log_1m = jnp.where(mask, log_beta - z, 0.0)
        rev = lax.cumsum(log_1m, axis=3, reverse=True)
        suffix = jnp.concatenate([rev[..., 1:], jnp.zeros_like(rev[..., :1])], axis=-1)
        w = jnp.where(mask, jnp.exp(log_beta + suffix), 0.0)
        outs.append(jnp.einsum('bhqk,bkhd->bqhd', w.astype(v.dtype), v[:, :end]))
    return jnp.concatenate(outs, axis=1)


def sliding_window_sink_attention(q, k, v, sinks, valid):
    B, L, Hq, d = q.shape
    nb = L // BLOCK
    scale = d ** -0.5
    qb = q.reshape(B, nb, BLOCK, SWA_KV_HEADS, SWA_GROUP, d)

    def band(t):
        tb = t.reshape(B, nb, BLOCK, SWA_KV_HEADS, d)
        prev = jnp.pad(tb[:, :-1], ((0, 0), (1, 0), (0, 0), (0, 0), (0, 0)))
        return jnp.concatenate([prev, tb], axis=2)

    kb, vb = band(k), band(v)
    vblk = valid.reshape(nb, BLOCK)
    kvalid = jnp.concatenate([jnp.pad(vblk[:-1], ((1, 0), (0, 0)), constant_values=False), vblk], axis=1)
    scores = jnp.einsum('bnqhgd,bnkhd->bnhgqk', qb, kb,
                        preferred_element_type=jnp.float32) * scale
    diff = (BLOCK + jnp.arange(BLOCK))[:, None] - jnp.arange(2 * BLOCK)[None, :]
    mask = ((diff >= 0) & (diff < WINDOW))[None] & kvalid[:, None, :]
    scores = jnp.where(mask[None, :, None, None], scores, -jnp.inf)
    sink = sinks.astype(jnp.float32).reshape(SWA_KV_HEADS, SWA_GROUP)[None, None, :, :, None, None]
    sink = jnp.broadcast_to(sink, scores.shape[:-1] + (1,))
    probs = jax.nn.softmax(jnp.concatenate([scores, sink], axis=-1), axis=-1)[..., :-1]
    o = jnp.einsum('bnhgqk,bnkhd->bnqhgd', probs.astype(v.dtype), vb)
    return o.reshape(B, L, Hq * d)


def _fwd_setup_inputs(seed: int = 0) -> dict:
    key = jax.random.key(seed)
    ks = jax.random.split(key, 10)
    f32 = jnp.float32
    x = jax.random.normal(ks[0], (BATCH, SEQ, D_MODEL), f32)
    meta_tokens = jax.random.normal(ks[1], (N_META, D_MODEL), f32)
    norm_gain = 1.0 + 0.02 * jax.random.normal(ks[2], (DEPTH, D_MODEL), f32)
    w_in = jax.random.normal(ks[3], (DEPTH, D_MODEL, IN_COLS), f32) * D_MODEL ** -0.5
    w_branch_sb = jax.random.normal(ks[4], (DEPTH, SB_WIDTH, D_MODEL), f32) * SB_WIDTH ** -0.5
    w_branch_swa = jax.random.normal(ks[5], (DEPTH, SWA_WIDTH, D_MODEL), f32) * SWA_WIDTH ** -0.5
    w_out = jax.random.normal(ks[6], (DEPTH, D_MODEL, D_MODEL), f32) * D_MODEL ** -0.5
    attn_sinks = jax.random.normal(ks[7], (DEPTH, SWA_Q_HEADS), f32)
    final_norm_gain = 1.0 + 0.02 * jax.random.normal(ks[8], (D_MODEL,), f32)
    return {"x": x, "meta_tokens": meta_tokens, "norm_gain": norm_gain, "w_in": w_in,
            "w_branch_sb": w_branch_sb, "w_branch_swa": w_branch_swa, "w_out": w_out,
            "attn_sinks": attn_sinks, "final_norm_gain": final_norm_gain}


def _fwd_reference(x, meta_tokens, norm_gain, w_in, w_branch_sb, w_branch_swa, w_out, attn_sinks, final_norm_gain):
    B = x.shape[0]
    meta = jnp.broadcast_to(meta_tokens[None].astype(x.dtype), (B, N_META, D_MODEL))
    pad = jnp.zeros((B, PAD, D_MODEL), x.dtype)
    h = jnp.concatenate([pad, meta, x], axis=1)
    L = h.shape[1]
    idx = jnp.arange(L)
    valid = idx >= PAD
    pos = idx - PAD
    offsets = [int(o) for o in np.cumsum(SPLITS)[:-1]]
    for l in range(DEPTH):
        xn = rms_norm(h, norm_gain[l])
        proj = xn @ w_in[l]
        sb_q, sb_k, sb_v, sw_q, sw_k, sw_v, sb_z, sw_z, g_sb, g_sw = jnp.split(proj, offsets, axis=-1)
        o_sb = stick_breaking_attention(sb_q.reshape(B, L, SB_HEADS, HEAD_DIM),
                                        sb_k.reshape(B, L, SB_HEADS, HEAD_DIM),
                                        sb_v.reshape(B, L, SB_HEADS, HEAD_DIM), valid).reshape(B, L, SB_WIDTH)
        q = rope(sw_q.reshape(B, L, SWA_Q_HEADS, HEAD_DIM), pos)
        k = rope(sw_k.reshape(B, L, SWA_KV_HEADS, HEAD_DIM), pos)
        o_sw = sliding_window_sink_attention(q, k, sw_v.reshape(B, L, SWA_KV_HEADS, HEAD_DIM),
                                             attn_sinks[l], valid)
        y_sb = (o_sb * jax.nn.silu(sb_z)) @ w_branch_sb[l]
        y_sw = (o_sw * jax.nn.silu(sw_z)) @ w_branch_swa[l]
        merged = jax.nn.sigmoid(g_sb) * y_sb + jax.nn.sigmoid(g_sw) * y_sw
        h = h + merged @ w_out[l]
    return rms_norm(h, final_norm_gain)[:, BLOCK:]


import jax as _jax
import jax.numpy as _jnp

TWIN_FORMAT = 'train_step'
FWD_PARAMS = ['x', 'meta_tokens', 'norm_gain', 'w_in', 'w_branch_sb', 'w_branch_swa', 'w_out', 'attn_sinks', 'final_norm_gain']
TWIN_WEIGHTS = ['meta_tokens', 'norm_gain', 'w_in', 'w_branch_sb', 'w_branch_swa', 'w_out', 'attn_sinks', 'final_norm_gain']
TWIN_DIFF_INPUT = 'x'
TWIN_INPUTS = ['x', 'meta_tokens', 'norm_gain', 'w_in', 'w_branch_sb', 'w_branch_swa', 'w_out', 'attn_sinks', 'final_norm_gain', 'loss_target', 'm_meta_tokens', 'm_norm_gain', 'm_w_in', 'm_w_branch_sb', 'm_w_branch_swa', 'm_w_out', 'm_attn_sinks', 'm_final_norm_gain', 'v_meta_tokens', 'v_norm_gain', 'v_w_in', 'v_w_branch_sb', 'v_w_branch_swa', 'v_w_out', 'v_attn_sinks', 'v_final_norm_gain']
TWIN_OUTPUTS = ['loss', 'grad_x', 'grad_meta_tokens', 'grad_norm_gain', 'grad_w_in', 'grad_w_branch_sb', 'grad_w_branch_swa', 'grad_w_out', 'grad_attn_sinks', 'grad_final_norm_gain', 'delta_meta_tokens', 'delta_norm_gain', 'delta_w_in', 'delta_w_branch_sb', 'delta_w_branch_swa', 'delta_w_out', 'delta_attn_sinks', 'delta_final_norm_gain', 'new_m_meta_tokens', 'new_m_norm_gain', 'new_m_w_in', 'new_m_w_branch_sb', 'new_m_w_branch_swa', 'new_m_w_out', 'new_m_attn_sinks', 'new_m_final_norm_gain', 'new_v_meta_tokens', 'new_v_norm_gain', 'new_v_w_in', 'new_v_w_branch_sb', 'new_v_w_branch_swa', 'new_v_w_out', 'new_v_attn_sinks', 'new_v_final_norm_gain']
TWIN_LEAF_KINDS = {'loss': 'loss', 'grad_x': 'grad_x', 'grad_meta_tokens': 'grad_w', 'grad_norm_gain': 'grad_w', 'grad_w_in': 'grad_w', 'grad_w_branch_sb': 'grad_w', 'grad_w_branch_swa': 'grad_w', 'grad_w_out': 'grad_w', 'grad_attn_sinks': 'grad_w', 'grad_final_norm_gain': 'grad_w', 'delta_meta_tokens': 'delta_w', 'delta_norm_gain': 'delta_w', 'delta_w_in': 'delta_w', 'delta_w_branch_sb': 'delta_w', 'delta_w_branch_swa': 'delta_w', 'delta_w_out': 'delta_w', 'delta_attn_sinks': 'delta_w', 'delta_final_norm_gain': 'delta_w', 'new_m_meta_tokens': 'new_m', 'new_m_norm_gain': 'new_m', 'new_m_w_in': 'new_m', 'new_m_w_branch_sb': 'new_m', 'new_m_w_branch_swa': 'new_m', 'new_m_w_out': 'new_m', 'new_m_attn_sinks': 'new_m', 'new_m_final_norm_gain': 'new_m', 'new_v_meta_tokens': 'new_v', 'new_v_norm_gain': 'new_v', 'new_v_w_in': 'new_v', 'new_v_w_branch_sb': 'new_v', 'new_v_w_branch_swa': 'new_v', 'new_v_w_out': 'new_v', 'new_v_attn_sinks': 'new_v', 'new_v_final_norm_gain': 'new_v'}


def _forward(args):
    return _fwd_reference(*[args[k] for k in FWD_PARAMS])


def _output_shape():
    def fwd():
        inp = _fwd_setup_inputs(0)
        return _fwd_reference(*[inp[k] for k in FWD_PARAMS])
    out = _jax.eval_shape(fwd)
    return out.shape, out.dtype

N_MICROBATCH = 1
ADAM_LR = 0.001
ADAM_B1 = 0.9
ADAM_B2 = 0.999
ADAM_EPS = 1e-08
ADAM_WD = 0.01
ADAM_STEP = 10
PER_EXAMPLE_BATCH_AXIS = {'x': 0, 'loss_target': 0}
SHARED_INPUTS = []
_WEIGHT_DTYPES = {'meta_tokens': _jnp.float32, 'norm_gain': _jnp.float32, 'w_in': _jnp.float32, 'w_branch_sb': _jnp.float32, 'w_branch_swa': _jnp.float32, 'w_out': _jnp.float32, 'attn_sinks': _jnp.float32, 'final_norm_gain': _jnp.float32}
MOMENT_SCALE = {'meta_tokens': 2.434593e-03, 'norm_gain': 8.496360e-02, 'w_in': 3.483767e-02, 'w_branch_sb': 5.099493e-02, 'w_branch_swa': 1.397703e-02, 'w_out': 5.272350e-02, 'attn_sinks': 2.078631e-03, 'final_norm_gain': 6.399925e+01}


def _to_microbatches(a, axis):
    t = _jnp.moveaxis(a, axis, 0)
    t = t.reshape((N_MICROBATCH, t.shape[0] // N_MICROBATCH) + t.shape[1:])
    return _jnp.moveaxis(t, 1, axis + 1)


def setup_inputs(seed: int = 0) -> dict:
    inp = _fwd_setup_inputs(seed)
    key = _jax.random.fold_in(_jax.random.key(seed), 7919)
    shape, _ = _output_shape()
    out = dict(inp)
    out["loss_target"] = _jax.random.normal(_jax.random.fold_in(key, 0), shape, _jnp.float32)
    for i, name in enumerate(TWIN_WEIGHTS):
        w = inp[name].astype(_jnp.float32)
        if MOMENT_SCALE is None:
            s = _jnp.sqrt(_jnp.mean(_jnp.square(w)) + 1e-30)
        else:
            s = MOMENT_SCALE[name]
        km, kv = _jax.random.split(_jax.random.fold_in(key, i + 1))
        out[name] = w
        out["m_" + name] = s * _jax.random.normal(km, w.shape, _jnp.float32)
        out["v_" + name] = (s * s) * _jax.random.uniform(kv, w.shape, _jnp.float32, 0.5, 1.5)
    if N_MICROBATCH > 1:
        for name, axis in PER_EXAMPLE_BATCH_AXIS.items():
            out[name] = _to_microbatches(out[name], axis)
    return {'x': out['x'], 'meta_tokens': out['meta_tokens'], 'norm_gain': out['norm_gain'], 'w_in': out['w_in'], 'w_branch_sb': out['w_branch_sb'], 'w_branch_swa': out['w_branch_swa'], 'w_out': out['w_out'], 'attn_sinks': out['attn_sinks'], 'final_norm_gain': out['final_norm_gain'], 'loss_target': out['loss_target'], 'm_meta_tokens': out['m_meta_tokens'], 'm_norm_gain': out['m_norm_gain'], 'm_w_in': out['m_w_in'], 'm_w_branch_sb': out['m_w_branch_sb'], 'm_w_branch_swa': out['m_w_branch_swa'], 'm_w_out': out['m_w_out'], 'm_attn_sinks': out['m_attn_sinks'], 'm_final_norm_gain': out['m_final_norm_gain'], 'v_meta_tokens': out['v_meta_tokens'], 'v_norm_gain': out['v_norm_gain'], 'v_w_in': out['v_w_in'], 'v_w_branch_sb': out['v_w_branch_sb'], 'v_w_branch_swa': out['v_w_branch_swa'], 'v_w_out': out['v_w_out'], 'v_attn_sinks': out['v_attn_sinks'], 'v_final_norm_gain': out['v_final_norm_gain']}


def _loss(weights, diff, rest, loss_target):
    with _jax.named_scope("forward"):
        args = {**rest, TWIN_DIFF_INPUT: diff, **{k: w.astype(_WEIGHT_DTYPES[k]) for k, w in weights.items()}}
        y = _forward(args)
    with _jax.named_scope("loss_head"):
        err = _jnp.square(y.astype(_jnp.float32) - loss_target)
        return 0.5 * _jnp.sum(_jnp.mean(err, axis=-1)) if err.ndim else 0.5 * err


def _adamw(w, g, m, v):
    m = ADAM_B1 * m + (1.0 - ADAM_B1) * g
    v = ADAM_B2 * v + (1.0 - ADAM_B2) * _jnp.square(g)
    m_hat = m / (1.0 - ADAM_B1 ** ADAM_STEP)
    v_hat = v / (1.0 - ADAM_B2 ** ADAM_STEP)
    delta = -ADAM_LR * (m_hat / (_jnp.sqrt(v_hat) + ADAM_EPS) + ADAM_WD * w)
    return delta, m, v


def reference(x, meta_tokens, norm_gain, w_in, w_branch_sb, w_branch_swa, w_out, attn_sinks, final_norm_gain, loss_target, m_meta_tokens, m_norm_gain, m_w_in, m_w_branch_sb, m_w_branch_swa, m_w_out, m_attn_sinks, m_final_norm_gain, v_meta_tokens, v_norm_gain, v_w_in, v_w_branch_sb, v_w_branch_swa, v_w_out, v_attn_sinks, v_final_norm_gain):
    given = dict(x=x, meta_tokens=meta_tokens, norm_gain=norm_gain, w_in=w_in, w_branch_sb=w_branch_sb, w_branch_swa=w_branch_swa, w_out=w_out, attn_sinks=attn_sinks, final_norm_gain=final_norm_gain, loss_target=loss_target, m_meta_tokens=m_meta_tokens, m_norm_gain=m_norm_gain, m_w_in=m_w_in, m_w_branch_sb=m_w_branch_sb, m_w_branch_swa=m_w_branch_swa, m_w_out=m_w_out, m_attn_sinks=m_attn_sinks, m_final_norm_gain=m_final_norm_gain, v_meta_tokens=v_meta_tokens, v_norm_gain=v_norm_gain, v_w_in=v_w_in, v_w_branch_sb=v_w_branch_sb, v_w_branch_swa=v_w_branch_swa, v_w_out=v_w_out, v_attn_sinks=v_attn_sinks, v_final_norm_gain=v_final_norm_gain)
    weights = {n: given[n] for n in TWIN_WEIGHTS}
    shared = {n: given[n] for n in SHARED_INPUTS}
    per_example = {n: given[n] for n in ['x']}
    grad_fn = _jax.value_and_grad(_loss, argnums=(0, 1))

    def one_microbatch(ex, loss_target):
        ex = dict(ex)
        diff = ex.pop(TWIN_DIFF_INPUT)
        return grad_fn(weights, diff, {**shared, **ex}, loss_target)

    if N_MICROBATCH == 1:
        loss, (grad_w, grad_x) = one_microbatch(per_example, given["loss_target"])
    else:
        def body(carry, xs):
            loss_sum, grad_sum = carry
            l_k, (gw_k, gx_k) = one_microbatch(xs[0], xs[1])
            with _jax.named_scope("update"):
                return (loss_sum + l_k, _jax.tree.map(_jnp.add, grad_sum, gw_k)), gx_k

        init = (_jnp.zeros((), _jnp.float32), _jax.tree.map(_jnp.zeros_like, weights))
        (loss, grad_w), grad_x = _jax.lax.scan(body, init, (per_example, given["loss_target"]))
    with _jax.named_scope("update"):
        delta_w, new_m, new_v = {}, {}, {}
        for n in TWIN_WEIGHTS:
            delta_w[n], new_m[n], new_v[n] = _adamw(weights[n], grad_w[n], given["m_" + n], given["v_" + n])
    return (loss, grad_x, *[grad_w[n] for n in TWIN_WEIGHTS], *[delta_w[n] for n in TWIN_WEIGHTS],
            *[new_m[n] for n in TWIN_WEIGHTS], *[new_v[n] for n in TWIN_WEIGHTS])
```

```python
import functools

import numpy as np
import jax
import jax.numpy as jnp
from jax import lax
from jax.experimental import pallas as pl
from jax.experimental.pallas import tpu as pltpu

F32 = jnp.float32
BF16 = jnp.bfloat16

D_MODEL = 1024
N_META = 16
BLOCK = 128
PAD = BLOCK - N_META
HEAD_DIM = 64
SB_HEADS = 8
SB_WIDTH = SB_HEADS * HEAD_DIM
SWA_Q_HEADS = 16
SWA_KV_HEADS = 2
SWA_WIDTH = SWA_Q_HEADS * HEAD_DIM
SWA_KV_WIDTH = SWA_KV_HEADS * HEAD_DIM
ROPE_THETA = 10000.0
RMS_EPS = 1e-6
SCALE = HEAD_DIM ** -0.5
SPLITS = (SB_WIDTH, SB_WIDTH, SB_WIDTH, SWA_WIDTH, SWA_KV_WIDTH, SWA_KV_WIDTH,
          SB_WIDTH, SWA_WIDTH, D_MODEL, D_MODEL)
IN_COLS = sum(SPLITS)
(CB_SBQ, CB_SBK, CB_SBV, CB_SWQ, CB_SWK, CB_SWV, CB_SBZ, CB_SWZ, CB_GSB, CB_GSW) = [
    int(o) // BLOCK for o in np.concatenate([[0], np.cumsum(SPLITS)[:-1]])]
GATE_COL0 = CB_SBZ * BLOCK
GATE_COLS = IN_COLS - GATE_COL0

N_DEV = 8
ADAM_LR = 0.001
ADAM_B1 = 0.9
ADAM_B2 = 0.999
ADAM_EPS = 1e-08
ADAM_WD = 0.01
ADAM_STEP = 10

VMEM_LIMIT = 56 * 1024 * 1024


def _params(sem, **kw):
    return pltpu.CompilerParams(dimension_semantics=sem, vmem_limit_bytes=VMEM_LIMIT, **kw)


def _dot(a, b):
    return jnp.dot(a, b, preferred_element_type=F32)


def _dot_nt(a, b):
    return lax.dot_general(a, b, (((1,), (1,)), ((), ())), preferred_element_type=F32)


def _dot_tn(a, b):
    return lax.dot_general(a, b, (((0,), (0,)), ((), ())), preferred_element_type=F32)


def _split_bf16(x):
    hi = x.astype(BF16)
    lo = (x - hi.astype(F32)).astype(BF16)
    return jnp.concatenate([hi, lo], axis=1)


def _suffix_matrix():
    j = np.arange(BLOCK)[:, None]
    s = np.arange(BLOCK)[None, :]
    t = (j > s).astype(np.float32)
    return jnp.asarray(np.concatenate([t, t], axis=0), dtype=BF16)


def _log_terms(z):
    sp = jnp.log1p(jnp.exp(-jnp.abs(z)))
    return jnp.minimum(z, 0.0) - sp, -jnp.maximum(z, 0.0) - sp


def _sb_masks(i, diag):
    r = lax.broadcasted_iota(jnp.int32, (BLOCK, BLOCK), 0)
    c = lax.broadcasted_iota(jnp.int32, (BLOCK, BLOCK), 1)
    if diag:
        return (c < r) & (i * BLOCK + c >= PAD)
    return c >= PAD


def sb_attention_fwd(proj, tt):
    L = proj.shape[0]
    nb = L // BLOCK
    npair = SB_WIDTH // BLOCK

    def body(q_ref, k_ref, v_ref, tt_ref, o_ref, kb_ref, va_ref, vb_ref, acc_ref, c_ref):
        i = pl.program_id(1)
        half0 = lax.broadcasted_iota(jnp.int32, (1, BLOCK), 1) < HEAD_DIM

        @pl.when(i == 0)
        def _():
            kb_ref[...] = k_ref[...].astype(BF16)
            v = v_ref[...]
            va_ref[...] = jnp.where(half0, v, 0.0).astype(BF16)
            vb_ref[...] = jnp.where(half0, 0.0, v).astype(BF16)

        q = q_ref[...] * SCALE
        qh = (jnp.where(half0, q, 0.0).astype(BF16), jnp.where(half0, 0.0, q).astype(BF16))
        vh = (va_ref, vb_ref)
        acc_ref[...] = jnp.zeros_like(acc_ref)
        c_ref[...] = jnp.zeros_like(c_ref)

        def tile(j, mask):
            rows = pl.ds(pl.multiple_of(j * BLOCK, BLOCK), BLOCK)
            k2 = kb_ref[rows, :]
            acc = acc_ref[...]
            for a in range(2):
                z = _dot_nt(qh[a], k2)
                lb, l1m = _log_terms(z)
                if mask is not None:
                    l1m = jnp.where(mask, l1m, 0.0)
                suf = _dot(_split_bf16(l1m), tt_ref[...])
                c = c_ref[a]
                w = jnp.exp(lb + suf + c)
                if mask is not None:
                    w = jnp.where(mask, w, 0.0)
                c_ref[a] = c + suf[:, :1] + l1m[:, :1]
                acc = acc + _dot(w.astype(BF16), vh[a][rows, :])
            acc_ref[...] = acc

        tile(i, _sb_masks(i, True))

        def mid(t, carry):
            tile(i - 1 - t, None)
            return carry
        lax.fori_loop(0, jnp.maximum(i - 1, 0), mid, 0)

        @pl.when(i > 0)
        def _():
            tile(0, _sb_masks(i, False))

        o_ref[...] = acc_ref[...]

    return pl.pallas_call(
        body,
        name="sb_attention_fwd",
        grid=(npair, nb),
        in_specs=[
            pl.BlockSpec((BLOCK, BLOCK), lambda p, i: (i, CB_SBQ + p)),
            pl.BlockSpec((L, BLOCK), lambda p, i: (0, CB_SBK + p)),
            pl.BlockSpec((L, BLOCK), lambda p, i: (0, CB_SBV + p)),
            pl.BlockSpec((2 * BLOCK, BLOCK), lambda p, i: (0, 0)),
        ],
        out_specs=pl.BlockSpec((BLOCK, BLOCK), lambda p, i: (i, p)),
        out_shape=jax.ShapeDtypeStruct((L, SB_WIDTH), F32),
        scratch_shapes=[
            pltpu.VMEM((L, BLOCK), BF16), pltpu.VMEM((L, BLOCK), BF16), pltpu.VMEM((L, BLOCK), BF16),
            pltpu.VMEM((BLOCK, BLOCK), F32), pltpu.VMEM((2, BLOCK, 1), F32),
        ],
        compiler_params=_params(("arbitrary", "arbitrary")),
    )(proj, proj, proj, tt)


def sb_attention_bwd(proj, o_sb, do_sb, tt):
    L = proj.shape[0]
    nb = L // BLOCK
    npair = SB_WIDTH // BLOCK

    def body(q_ref, k_ref, v_ref, o_ref, do_ref, tt_ref, dq_ref, dk_ref, dv_ref,
             kb_ref, ka_ref, kc_ref, vb_ref, acc_ref, c_ref, ce_ref):
        i = pl.program_id(1)
        half0 = lax.broadcasted_iota(jnp.int32, (1, BLOCK), 1) < HEAD_DIM

        @pl.when(i == 0)
        def _():
            k = k_ref[...]
            kb_ref[...] = k.astype(BF16)
            ka_ref[...] = jnp.where(half0, k, 0.0).astype(BF16)
            kc_ref[...] = jnp.where(half0, 0.0, k).astype(BF16)
            vb_ref[...] = v_ref[...].astype(BF16)
            dk_ref[...] = jnp.zeros_like(dk_ref)
            dv_ref[...] = jnp.zeros_like(dv_ref)

        q = q_ref[...] * SCALE
        qh = (jnp.where(half0, q, 0.0).astype(BF16), jnp.where(half0, 0.0, q).astype(BF16))
        do = do_ref[...]
        doh = (jnp.where(half0, do, 0.0).astype(BF16), jnp.where(half0, 0.0, do).astype(BF16))
        od = o_ref[...] * do.astype(BF16).astype(F32)
        dsum = (jnp.sum(jnp.where(half0, od, 0.0), axis=1, keepdims=True),
                jnp.sum(jnp.where(half0, 0.0, od), axis=1, keepdims=True))
        kh = (ka_ref, kc_ref)
        acc_ref[...] = jnp.zeros_like(acc_ref)
        c_ref[...] = jnp.zeros_like(c_ref)
        ce_ref[...] = jnp.zeros_like(ce_ref)

        def tile(j, mask):
            rows = pl.ds(pl.multiple_of(j * BLOCK, BLOCK), BLOCK)
            k2 = kb_ref[rows, :]
            v2 = vb_ref[rows, :]
            acc = acc_ref[...]
            dk = dk_ref[rows, :]
            dv = dv_ref[rows, :]
            for a in range(2):
                z = _dot_nt(qh[a], k2)
                lb, l1m = _log_terms(z)
                if mask is not None:
                    l1m = jnp.where(mask, l1m, 0.0)
                suf = _dot(_split_bf16(l1m), tt_ref[...])
                c = c_ref[a]
                w = jnp.exp(lb + suf + c)
                if mask is not None:
                    w = jnp.where(mask, w, 0.0)
                wb = w.astype(BF16)
                e = wb.astype(F32) * _dot_nt(doh[a], v2)
                esuf = _dot(_split_bf16(e), tt_ref[...])
                ce = ce_ref[a]
                dz = e - jnp.exp(lb) * (dsum[a] - esuf - ce)
                if mask is not None:
                    dz = jnp.where(mask, dz, 0.0)
                c_ref[a] = c + suf[:, :1] + l1m[:, :1]
                ce_ref[a] = ce + esuf[:, :1] + e[:, :1]
                dzb = dz.astype(BF16)
                acc = acc + _dot(dzb, kh[a][rows, :])
                dk = dk + _dot_tn(dzb, qh[a])
                dv = dv + _dot_tn(wb, doh[a])
            acc_ref[...] = acc
            dk_ref[rows, :] = dk
            dv_ref[rows, :] = dv

        tile(i, _sb_masks(i, True))

        def mid(t, carry):
            tile(i - 1 - t, None)
            return carry
        lax.fori_loop(0, jnp.maximum(i - 1, 0), mid, 0)

        @pl.when(i > 0)
        def _():
            tile(0, _sb_masks(i, False))

        dq_ref[...] = acc_ref[...] * SCALE

    blk = pl.BlockSpec((BLOCK, BLOCK), lambda p, i: (i, p))
    panel = pl.BlockSpec((L, BLOCK), lambda p, i: (0, p))
    return pl.pallas_call(
        body,
        name="sb_attention_bwd",
        grid=(npair, nb),
        in_specs=[
            pl.BlockSpec((BLOCK, BLOCK), lambda p, i: (i, CB_SBQ + p)),
            pl.BlockSpec((L, BLOCK), lambda p, i: (0, CB_SBK + p)),
            pl.BlockSpec((L, BLOCK), lambda p, i: (0, CB_SBV + p)),
            blk, blk,
            pl.BlockSpec((2 * BLOCK, BLOCK), lambda p, i: (0, 0)),
        ],
        out_specs=[blk, panel, panel],
        out_shape=[jax.ShapeDtypeStruct((L, SB_WIDTH), F32)] * 3,
        scratch_shapes=[
            pltpu.VMEM((L, BLOCK), BF16), pltpu.VMEM((L, BLOCK), BF16), pltpu.VMEM((L, BLOCK), BF16),
            pltpu.VMEM((L, BLOCK), BF16),
            pltpu.VMEM((BLOCK, BLOCK), F32), pltpu.VMEM((2, BLOCK, 1), F32), pltpu.VMEM((2, BLOCK, 1), F32),
        ],
        compiler_params=_params(("arbitrary", "arbitrary")),
    )(proj, proj, proj, o_sb, do_sb, tt)


def rope_tables(L):
    half = HEAD_DIM // 2
    inv = ROPE_THETA ** (-jnp.arange(half, dtype=F32) / half)
    pos = (jnp.arange(L) - PAD).astype(F32)
    ang = pos[:, None] * inv[None, :]
    reps = BLOCK // half
    return jnp.tile(jnp.cos(ang), (1, reps)), jnp.tile(jnp.sin(ang), (1, reps))


def _rot_half(x):
    lane = lax.broadcasted_iota(jnp.int32, (1, BLOCK), 1)
    first = (lane % HEAD_DIM) < (HEAD_DIM // 2)
    return jnp.where(first, -pltpu.roll(x, BLOCK - HEAD_DIM // 2, axis=1), pltpu.roll(x, HEAD_DIM // 2, axis=1))


def _rope(x, cos, sin):
    return x * cos + _rot_half(x) * sin


def _unrope(x, cos, sin):
    return x * cos - _rot_half(x) * sin


def _swa_specs(nq):
    prev = lambda n: jnp.maximum(n - 1, 0)
    return [
        pl.BlockSpec((BLOCK, BLOCK), lambda n, p: (n, CB_SWQ + p)),
        pl.BlockSpec((BLOCK, BLOCK), lambda n, p: (prev(n), CB_SWK)),
        pl.BlockSpec((BLOCK, BLOCK), lambda n, p: (n, CB_SWK)),
        pl.BlockSpec((BLOCK, BLOCK), lambda n, p: (prev(n), CB_SWV)),
        pl.BlockSpec((BLOCK, BLOCK), lambda n, p: (n, CB_SWV)),
        pl.BlockSpec((BLOCK, BLOCK), lambda n, p: (prev(n), 0)),
        pl.BlockSpec((BLOCK, BLOCK), lambda n, p: (n, 0)),
        pl.BlockSpec((BLOCK, BLOCK), lambda n, p: (prev(n), 0)),
        pl.BlockSpec((BLOCK, BLOCK), lambda n, p: (n, 0)),
        pl.BlockSpec(memory_space=pltpu.SMEM),
    ]


def _swa_common(n, p, q_ref, kp_ref, kc_ref, vp_ref, vc_ref, cp_ref, cc_ref, sp_ref, sc_ref, sink_ref):
    lane = lax.broadcasted_iota(jnp.int32, (1, BLOCK), 1)
    halves = (lane < HEAD_DIM, lane >= HEAD_DIM)
    q = _rope(q_ref[...], cc_ref[...], sc_ref[...]) * SCALE
    kb = jnp.concatenate([_rope(kp_ref[...], cp_ref[...], sp_ref[...]),
                          _rope(kc_ref[...], cc_ref[...], sc_ref[...])], axis=0)
    vb = jnp.concatenate([vp_ref[...], vc_ref[...]], axis=0)
    ksw = pltpu.roll(kb, HEAD_DIM, axis=1)
    vsw = pltpu.roll(vb, HEAD_DIM, axis=1)
    g = p // (SWA_Q_HEADS // SWA_KV_HEADS // 2)
    r = lax.broadcasted_iota(jnp.int32, (BLOCK, 2 * BLOCK), 0)
    c = lax.broadcasted_iota(jnp.int32, (BLOCK, 2 * BLOCK), 1)
    valid = (c > r) & (c <= r + BLOCK) & ((n - 1) * BLOCK + c >= PAD)
    out = []
    for a in range(2):
        qa = jnp.where(halves[a], q, 0.0).astype(BF16)
        ksel = jnp.where(g == a, kb, ksw)
        vsel = jnp.where(g == a, vb, vsw)
        s = jnp.where(valid, _dot_nt(qa, ksel.astype(BF16)), -1e30)
        sink = sink_ref[0, 2 * p + a]
        mx = jnp.maximum(jnp.max(s, axis=1, keepdims=True), sink)
        pe = jnp.exp(s - mx)
        es = jnp.exp(sink - mx)
        inv = 1.0 / (jnp.sum(pe, axis=1, keepdims=True) + es)
        out.append((qa, ksel, vsel, pe * inv, es * inv, halves[a]))
    return out, g


def swa_attention_fwd(proj, cos, sin, sinks):
    L = proj.shape[0]
    nb = L // BLOCK
    npair = SWA_WIDTH // BLOCK

    def body(q_ref, kp_ref, kc_ref, vp_ref, vc_ref, cp_ref, cc_ref, sp_ref, sc_ref, sink_ref, o_ref):
        n, p = pl.program_id(0), pl.program_id(1)
        parts, _ = _swa_common(n, p, q_ref, kp_ref, kc_ref, vp_ref, vc_ref, cp_ref, cc_ref, sp_ref, sc_ref, sink_ref)
        o = jnp.zeros((BLOCK, BLOCK), F32)
        for qa, ksel, vsel, probs, psink, half in parts:
            o = o + _dot(probs.astype(BF16), jnp.where(half, vsel, 0.0).astype(BF16))
        o_ref[...] = o

    return pl.pallas_call(
        body,
        name="swa_attention_fwd",
        grid=(nb, npair),
        in_specs=_swa_specs(npair),
        out_specs=pl.BlockSpec((BLOCK, BLOCK), lambda n, p: (n, p)),
        out_shape=jax.ShapeDtypeStruct((L, SWA_WIDTH), F32),
        compiler_params=_params(("arbitrary", "arbitrary")),
    )(proj, proj, proj, proj, proj, cos, cos, sin, sin, sinks)


def swa_attention_bwd(proj, cos, sin, sinks, do_sw):
    L = proj.shape[0]
    nb = L // BLOCK
    npair = SWA_WIDTH // BLOCK

    def body(q_ref, kp_ref, kc_ref, vp_ref, vc_ref, cp_ref, cc_ref, sp_ref, sc_ref, sink_ref, do_ref,
             dq_ref, dk_ref, dv_ref, ds_ref):
        n, p = pl.program_id(0), pl.program_id(1)

        @pl.when((n == 0) & (p == 0))
        def _():
            dk_ref[...] = jnp.zeros_like(dk_ref)
            dv_ref[...] = jnp.zeros_like(dv_ref)
            ds_ref[...] = jnp.zeros_like(ds_ref)

        parts, g = _swa_common(n, p, q_ref, kp_ref, kc_ref, vp_ref, vc_ref, cp_ref, cc_ref, sp_ref, sc_ref, sink_ref)
        do = do_ref[...]
        lane = lax.broadcasted_iota(jnp.int32, (8, BLOCK), 1)
        dq = jnp.zeros((BLOCK, BLOCK), F32)
        dkb = jnp.zeros((2 * BLOCK, BLOCK), F32)
        dvb = jnp.zeros((2 * BLOCK, BLOCK), F32)
        dsk = jnp.zeros((8, BLOCK), F32)
        for a, (qa, ksel, vsel, probs, psink, half) in enumerate(parts):
            doa = jnp.where(half, do, 0.0).astype(BF16)
            dp = _dot_nt(doa, vsel.astype(BF16))
            delta = jnp.sum(probs * dp, axis=1, keepdims=True)
            ds = (probs * (dp - delta)).astype(BF16)
            dsk = dsk + jnp.where(lane == 2 * p + a, -jnp.sum(psink * delta), 0.0)
            dq = dq + _dot(ds, jnp.where(half, ksel, 0.0).astype(BF16))
            dk_a = _dot_tn(ds, qa)
            dv_a = _dot_tn(probs.astype(BF16), doa)
            dkb = dkb + jnp.where(g == a, dk_a, pltpu.roll(dk_a, HEAD_DIM, axis=1))
            dvb = dvb + jnp.where(g == a, dv_a, pltpu.roll(dv_a, HEAD_DIM, axis=1))
        dq_ref[...] = _unrope(dq * SCALE, cc_ref[...], sc_ref[...])
        ds_ref[...] += dsk
        cur = pl.ds(pl.multiple_of(n * BLOCK, BLOCK), BLOCK)
        dk_ref[cur, :] += _unrope(dkb[BLOCK:], cc_ref[...], sc_ref[...])
        dv_ref[cur, :] += dvb[BLOCK:]

        @pl.when(n > 0)
        def _():
            prv = pl.ds(pl.multiple_of((n - 1) * BLOCK, BLOCK), BLOCK)
            dk_ref[prv, :] += _unrope(dkb[:BLOCK], cp_ref[...], sp_ref[...])
            dv_ref[prv, :] += dvb[:BLOCK]

    whole = lambda n, p: (0, 0)
    return pl.pallas_call(
        body,
        name="swa_attention_bwd",
        grid=(nb, npair),
        in_specs=_swa_specs(npair) + [pl.BlockSpec((BLOCK, BLOCK), lambda n, p: (n, p))],
        out_specs=[pl.BlockSpec((BLOCK, BLOCK), lambda n, p: (n, p)),
                   pl.BlockSpec((L, BLOCK), whole), pl.BlockSpec((L, BLOCK), whole),
                   pl.BlockSpec((8, BLOCK), whole)],
        out_shape=[jax.ShapeDtypeStruct((L, SWA_WIDTH), F32), jax.ShapeDtypeStruct((L, BLOCK), F32),
                   jax.ShapeDtypeStruct((L, BLOCK), F32), jax.ShapeDtypeStruct((8, BLOCK), F32)],
        compiler_params=_params(("arbitrary", "arbitrary")),
    )(proj, proj, proj, proj, proj, cos, cos, sin, sin, sinks, do_sw)


ROW_TILE = 640
TAIL_ROWS = 208


def _pick(n, cands):
    for c in cands:
        if n % c == 0:
            return c
    raise ValueError(f"no tile for {n}")


def in_proj(h0, gain, w, name):
    L, D = h0.shape
    N = w.shape[1]
    tm = _pick(L, (ROW_TILE, BLOCK))
    tn = _pick(N, (1792, 1408, 1280, 896, 640, 512, 384, 256, 128))

    def body(h_ref, g_ref, w_ref, o_ref, xn_ref):
        @pl.when(pl.program_id(1) == 0)
        def _():
            x = h_ref[...]
            r = lax.rsqrt(jnp.mean(x * x, axis=1, keepdims=True) + RMS_EPS)
            xn_ref[...] = ((x * r) * g_ref[...]).astype(BF16)
        o_ref[...] = _dot(xn_ref[...], w_ref[...])

    return pl.pallas_call(
        body,
        name=name,
        grid=(L // tm, N // tn),
        in_specs=[pl.BlockSpec((tm, D), lambda i, j: (i, 0)),
                  pl.BlockSpec((1, D), lambda i, j: (0, 0)),
                  pl.BlockSpec((D, tn), lambda i, j: (0, j))],
        out_specs=[pl.BlockSpec((tm, tn), lambda i, j: (i, j)),
                   pl.BlockSpec((tm, D), lambda i, j: (i, 0))],
        out_shape=[jax.ShapeDtypeStruct((L, N), F32), jax.ShapeDtypeStruct((L, D), BF16)],
        compiler_params=_params(("arbitrary", "arbitrary")),
    )(h0, gain, w)


def matmul_nn(a, b, name):
    M, Kd = a.shape
    N = b.shape[1]
    tk = _pick(Kd, (ROW_TILE, BLOCK))
    tn = _pick(N, (1280, 1024, 896, 640, 512, 256, 128))
    nk = Kd // tk

    def body(a_ref, b_ref, o_ref):
        k = pl.program_id(1)

        @pl.when(k == 0)
        def _():
            o_ref[...] = jnp.zeros_like(o_ref)
        o_ref[...] += _dot(a_ref[...], b_ref[...])

    return pl.pallas_call(
        body,
        name=name,
        grid=(N // tn, nk),
        in_specs=[pl.BlockSpec((M, tk), lambda j, k: (0, k)),
                  pl.BlockSpec((tk, tn), lambda j, k: (k, j))],
        out_specs=pl.BlockSpec((M, tn), lambda j, k: (0, j)),
        out_shape=jax.ShapeDtypeStruct((M, N), F32),
        compiler_params=_params(("arbitrary", "arbitrary")),
    )(a, b)


def in_proj_bwd(dproj, w, h0, gain, dh1):
    L, N = dproj.shape
    D = w.shape[0]
    tm = _pick(L, (ROW_TILE, BLOCK))
    tk = _pick(N, (640, 512, 256, 128))
    nk = N // tk

    def body(dp_ref, w_ref, h_ref, g_ref, dh1_ref, dh0_ref, dg_ref, acc_ref):
        i, k = pl.program_id(0), pl.program_id(1)

        @pl.when(k == 0)
        def _():
            acc_ref[...] = jnp.zeros_like(acc_ref)

        @pl.when((i == 0) & (k == 0))
        def _():
            dg_ref[...] = jnp.zeros_like(dg_ref)

        acc_ref[...] += _dot_nt(dp_ref[...], w_ref[...])

        @pl.when(k == nk - 1)
        def _():
            x = h_ref[...]
            r = lax.rsqrt(jnp.mean(x * x, axis=1, keepdims=True) + RMS_EPS)
            xhat = x * r
            dxn = acc_ref[...]
            dg_ref[...] += jnp.sum(dxn * xhat, axis=0, keepdims=True)
            dxh = dxn * g_ref[...]
            dh0_ref[...] = r * (dxh - xhat * jnp.mean(dxh * xhat, axis=1, keepdims=True)) + dh1_ref[...]

    row = pl.BlockSpec((tm, D), lambda i, k: (i, 0))
    vec = pl.BlockSpec((1, D), lambda i, k: (0, 0))
    return pl.pallas_call(
        body,
        name="in_proj_bwd",
        grid=(L // tm, nk),
        in_specs=[pl.BlockSpec((tm, tk), lambda i, k: (i, k)),
                  pl.BlockSpec((D, tk), lambda i, k: (0, k)),
                  row, vec, row],
        out_specs=[row, vec],
        out_shape=[jax.ShapeDtypeStruct((L, D), F32), jax.ShapeDtypeStruct((1, D), F32)],
        scratch_shapes=[pltpu.VMEM((tm, D), F32)],
        compiler_params=_params(("arbitrary", "arbitrary")),
    )(dproj, w, h0, gain, dh1)


def tail_fwd_bwd(h0, tgt, o_sb, o_sw, gates, w_bsb, w_bswa, w_out, gain_f):
    L, D = h0.shape
    R = _pick(L, (TAIL_ROWS, BLOCK))
    z0, z1, z2, z3 = 0, SB_WIDTH, SB_WIDTH + SWA_WIDTH, SB_WIDTH + SWA_WIDTH + D_MODEL

    def body(h_ref, t_ref, osb_ref, osw_ref, g_ref, wsb_ref, wsw_ref, wo_ref, gf_ref,
             dosb_ref, dosw_ref, dg_ref, dh1_ref, mb_ref, usb_ref, usw_ref, dh1b_ref, dysb_ref, dysw_ref,
             dgf_ref, loss_ref):
        i = pl.program_id(0)

        @pl.when(i == 0)
        def _():
            dgf_ref[...] = jnp.zeros_like(dgf_ref)
            loss_ref[...] = jnp.zeros_like(loss_ref)

        sbz = g_ref[:, z0:z1]
        swz = g_ref[:, z1:z2]
        s1 = jax.nn.sigmoid(g_ref[:, z2:z3])
        s2 = jax.nn.sigmoid(g_ref[:, z3:])
        sg_sb = jax.nn.sigmoid(sbz)
        sg_sw = jax.nn.sigmoid(swz)
        silu_sb = sbz * sg_sb
        silu_sw = swz * sg_sw
        osb = osb_ref[...]
        osw = osw_ref[...]
        usb = (osb * silu_sb).astype(BF16)
        usw = (osw * silu_sw).astype(BF16)
        y_sb = _dot(usb, wsb_ref[...])
        y_sw = _dot(usw, wsw_ref[...])
        mb = (s1 * y_sb + s2 * y_sw).astype(BF16)
        h1 = h_ref[...] + _dot(mb, wo_ref[...])
        rf = lax.rsqrt(jnp.mean(h1 * h1, axis=1, keepdims=True) + RMS_EPS)
        hhat = h1 * rf
        gf = gf_ref[...]
        row = i * R + lax.broadcasted_iota(jnp.int32, (R, 1), 0)
        err = jnp.where(row >= BLOCK, hhat * gf - t_ref[...], 0.0)
        lane0 = (lax.broadcasted_iota(jnp.int32, (8, BLOCK), 0) == 0) & (lax.broadcasted_iota(jnp.int32, (8, BLOCK), 1) == 0)
        loss_ref[...] += jnp.where(lane0, (0.5 / D) * jnp.sum(err * err), 0.0)
        dy = err * (1.0 / D)
        dgf_ref[...] += jnp.sum(dy * hhat, axis=0, keepdims=True)
        dhh = dy * gf
        dh1 = rf * (dhh - hhat * jnp.mean(dhh * hhat, axis=1, keepdims=True))
        dh1b = dh1.astype(BF16)
        dm = _dot_nt(dh1b, wo_ref[...])
        dysb = (dm * s1).astype(BF16)
        dysw = (dm * s2).astype(BF16)
        dusb = _dot_nt(dysb, wsb_ref[...])
        dusw = _dot_nt(dysw, wsw_ref[...])
        dosb_ref[...] = dusb * silu_sb
        dosw_ref[...] = dusw * silu_sw
        dg_ref[:, z0:z1] = dusb * osb * (sg_sb * (1.0 + sbz * (1.0 - sg_sb)))
        dg_ref[:, z1:z2] = dusw * osw * (sg_sw * (1.0 + swz * (1.0 - sg_sw)))
        dg_ref[:, z2:z3] = dm * y_sb * (s1 * (1.0 - s1))
        dg_ref[:, z3:] = dm * y_sw * (s2 * (1.0 - s2))
        dh1_ref[...] = dh1
        mb_ref[...] = mb
        usb_ref[...] = usb
        usw_ref[...] = usw
        dh1b_ref[...] = dh1b
        dysb_ref[...] = dysb
        dysw_ref[...] = dysw

    def rows(n):
        return pl.BlockSpec((R, n), lambda i: (i, 0))

    def whole(shape):
        return pl.BlockSpec(shape, lambda i: (0, 0))

    GW = gates.shape[1]
    return pl.pallas_call(
        body,
        name="tail_fwd_bwd",
        grid=(L // R,),
        in_specs=[rows(D), rows(D), rows(SB_WIDTH), rows(SWA_WIDTH), rows(GW),
                  whole(w_bsb.shape), whole(w_bswa.shape), whole(w_out.shape), whole((1, D))],
        out_specs=[rows(SB_WIDTH), rows(SWA_WIDTH), rows(GW), rows(D),
                   rows(D), rows(SB_WIDTH), rows(SWA_WIDTH), rows(D), rows(D), rows(D),
                   whole((1, D)), whole((8, BLOCK))],
        out_shape=[jax.ShapeDtypeStruct((L, SB_WIDTH), F32), jax.ShapeDtypeStruct((L, SWA_WIDTH), F32),
                   jax.ShapeDtypeStruct((L, GW), F32), jax.ShapeDtypeStruct((L, D), F32),
                   jax.ShapeDtypeStruct((L, D), BF16), jax.ShapeDtypeStruct((L, SB_WIDTH), BF16),
                   jax.ShapeDtypeStruct((L, SWA_WIDTH), BF16), jax.ShapeDtypeStruct((L, D), BF16),
                   jax.ShapeDtypeStruct((L, D), BF16), jax.ShapeDtypeStruct((L, D), BF16),
                   jax.ShapeDtypeStruct((1, D), F32), jax.ShapeDtypeStruct((8, BLOCK), F32)],
        compiler_params=_params(("arbitrary",)),
    )(h0, tgt, o_sb, o_sw, gates, w_bsb, w_bswa, w_out, gain_f)


def local_step(x, tgt, meta, gain, w_in, w_bsb, w_bswa, w_out, sinks, gain_f):
    S, D = x.shape
    L = S + BLOCK
    h0 = jnp.concatenate([jnp.zeros((PAD, D), F32), meta, x], axis=0)
    tgt_p = jnp.concatenate([jnp.zeros((BLOCK, D), F32), tgt], axis=0)
    tt = _suffix_matrix()
    cos, sin = rope_tables(L)
    proj, xn = in_proj(h0, gain, w_in[:, :GATE_COL0], "in_proj_attn")
    gates, _ = in_proj(h0, gain, w_in[:, GATE_COL0:], "in_proj_gates")
    o_sb = sb_attention_fwd(proj, tt)
    o_sw = swa_attention_fwd(proj, cos, sin, sinks)
    (do_sb, do_sw, dgates, dh1, mb, usb, usw, dh1b, dysb, dysw, dgf, loss) = tail_fwd_bwd(
        h0, tgt_p, o_sb, o_sw, gates, w_bsb, w_bswa, w_out, gain_f)
    dq_sb, dk_sb, dv_sb = sb_attention_bwd(proj, o_sb, do_sb, tt)
    dq_sw, dk_sw, dv_sw, dsinks = swa_attention_bwd(proj, cos, sin, sinks, do_sw)
    dproj = jnp.concatenate([dq_sb, dk_sb, dv_sb, dq_sw, dk_sw, dv_sw, dgates], axis=1).astype(BF16)
    dw_in = matmul_nn(xn.T, dproj, "dw_in")
    dw_out = matmul_nn(mb.T, dh1b, "dw_out")
    dw_bsb = matmul_nn(usb.T, dysb, "dw_bsb")
    dw_bswa = matmul_nn(usw.T, dysw, "dw_bswa")
    dh0, dgain = in_proj_bwd(dproj, w_in, h0, gain, dh1)
    return (loss[0, 0], dh0[BLOCK:], dh0[PAD:BLOCK], dgain, dw_in, dw_bsb, dw_bswa, dw_out,
            dsinks[:1, :SWA_Q_HEADS], dgf)


MESH_IDS = pl.DeviceIdType.MESH
ANY = pl.BlockSpec(memory_space=pl.ANY)


def _place():
    return lax.axis_index("x"), lax.axis_index("y"), lax.axis_index("c")


def _index(x, y, c):
    return 4 * x + 2 * y + c


def all_gather(block, name):
    def body(x_ref, out_ref, send_sems, recv_sems, local_sem):
        x, y, c = _place()
        me, sibling = (x, y, c), (x, y, 1 - c)
        chips = [(1 - x, y), (x, 1 - y), (1 - x, 1 - y)]

        def copy(k, blk, to, src=None):
            dst = out_ref.at[_index(*blk)]
            return pltpu.make_async_remote_copy(
                src_ref=dst if src is None else src, dst_ref=dst,
                send_sem=send_sems.at[k], recv_sem=recv_sems.at[k], device_id=to, device_id_type=MESH_IDS)

        mine = pltpu.make_async_copy(x_ref, out_ref.at[_index(*me)], local_sem)
        mine.start()
        first = [copy(0, me, sibling, src=x_ref)]
        first += [copy(1 + j, me, (*chip, c), src=x_ref) for j, chip in enumerate(chips)]
        for cp in first:
            cp.start()
        passed = [copy(4 + j, (*chip, c), sibling) for j, chip in enumerate(chips)]
        for j, chip in enumerate(chips):
            copy(1 + j, (*chip, c), me).wait_recv()
            passed[j].start()
        copy(0, sibling, me).wait_recv()
        for j, chip in enumerate(chips):
            copy(4 + j, (*chip, 1 - c), me).wait_recv()
        for cp in first + passed:
            cp.wait_send()
        mine.wait()

    return pl.pallas_call(
        body,
        name=name,
        out_shape=jax.ShapeDtypeStruct((N_DEV,) + block.shape, block.dtype),
        in_specs=[ANY],
        out_specs=ANY,
        scratch_shapes=[pltpu.SemaphoreType.DMA((7,)), pltpu.SemaphoreType.DMA((7,)), pltpu.SemaphoreType.DMA],
    )(block)


def exchange_partials(parts):
    def body(g_ref, out_ref, send_sems, recv_sems, local_sem):
        x, y, c = _place()
        me = _index(x, y, c)
        mine = pltpu.make_async_copy(g_ref.at[me], out_ref.at[me], local_sem)
        mine.start()
        copies = []
        for m in range(1, N_DEV):
            px = 1 - x if m & 4 else x
            py = 1 - y if m & 2 else y
            pc = 1 - c if m & 1 else c
            cp = pltpu.make_async_remote_copy(
                src_ref=g_ref.at[_index(px, py, pc)], dst_ref=out_ref.at[me],
                send_sem=send_sems.at[m - 1], recv_sem=recv_sems.at[m - 1],
                device_id=(px, py, pc), device_id_type=MESH_IDS)
            cp.start()
            copies.append(cp)
        for cp in copies:
            cp.wait()
        mine.wait()

    return pl.pallas_call(
        body,
        name="exchange_partials",
        out_shape=jax.ShapeDtypeStruct(parts.shape, parts.dtype),
        in_specs=[ANY],
        out_specs=ANY,
        scratch_shapes=[pltpu.SemaphoreType.DMA((7,)), pltpu.SemaphoreType.DMA((7,)), pltpu.SemaphoreType.DMA],
    )(parts)


def _adamw(w, g, m, v):
    m = ADAM_B1 * m + (1.0 - ADAM_B1) * g
    v = ADAM_B2 * v + (1.0 - ADAM_B2) * (g * g)
    m_hat = m / (1.0 - ADAM_B1 ** ADAM_STEP)
    v_hat = v / (1.0 - ADAM_B2 ** ADAM_STEP)
    delta = -ADAM_LR * (m_hat / (jnp.sqrt(v_hat) + ADAM_EPS) + ADAM_WD * w)
    return delta, m, v


def sum_and_adamw(parts, w, m, v, name):
    _, R, C = parts.shape
    tr = _pick(R, (528, 512, 256, 128, 24, 8))

    def body(p_ref, w_ref, m_ref, v_ref, g_ref, d_ref, nm_ref, nv_ref):
        g = p_ref[0]
        for s in range(1, N_DEV):
            g = g + p_ref[s]
        d, nm, nv = _adamw(w_ref[...], g, m_ref[...], v_ref[...])
        g_ref[...] = g
        d_ref[...] = d
        nm_ref[...] = nm
        nv_ref[...] = nv

    row = pl.BlockSpec((tr, C), lambda i: (i, 0))
    return pl.pallas_call(
        body,
        name=name,
        grid=(R // tr,),
        in_specs=[pl.BlockSpec((N_DEV, tr, C), lambda i: (0, i, 0)), row, row, row],
        out_specs=[row, row, row, row],
        out_shape=[jax.ShapeDtypeStruct((R, C), F32)] * 4,
        compiler_params=_params(("arbitrary",)),
    )(parts, w, m, v)


W_IN_SHARD = IN_COLS // N_DEV
ROWS_W_IN = D_MODEL * W_IN_SHARD // BLOCK
ROWS_W_BSB = SB_WIDTH
ROWS_W_ROWSHARD = D_MODEL
SMALL_ROWS = 24


def _pack_shards(w_in, w_bsb, w_bswa, w_out, meta):
    return jnp.concatenate([w_in.reshape(ROWS_W_IN, BLOCK), w_bsb.reshape(ROWS_W_BSB, BLOCK),
                            w_bswa.reshape(ROWS_W_ROWSHARD, BLOCK), w_out.reshape(ROWS_W_ROWSHARD, BLOCK),
                            meta.reshape(N_META, BLOCK)], axis=0)


def _unpack_shards(p):
    o = np.cumsum([0, ROWS_W_IN, ROWS_W_BSB, ROWS_W_ROWSHARD, ROWS_W_ROWSHARD, N_META])
    return (p[o[0]:o[1]].reshape(1, D_MODEL, W_IN_SHARD), p[o[1]:o[2]].reshape(1, SB_WIDTH, BLOCK),
            p[o[2]:o[3]].reshape(1, BLOCK, D_MODEL), p[o[3]:o[4]].reshape(1, BLOCK, D_MODEL),
            p[o[4]:o[5]].reshape(N_META, BLOCK))


def _pack_by_owner(dw_in, dw_bsb, dw_bswa, dw_out, dmeta):
    cols = lambda a: a.reshape(a.shape[0], N_DEV, -1).transpose(1, 0, 2)
    return jnp.concatenate([cols(dw_in).reshape(N_DEV, ROWS_W_IN, BLOCK), cols(dw_bsb),
                            dw_bswa.reshape(N_DEV, ROWS_W_ROWSHARD, BLOCK), dw_out.reshape(N_DEV, ROWS_W_ROWSHARD, BLOCK),
                            cols(dmeta)], axis=1)


def _pack_small(gain, gain_f, sinks, loss):
    z = jnp.zeros((SMALL_ROWS - 16, BLOCK), F32)
    z = z.at[0, :SWA_Q_HEADS].set(sinks.reshape(-1)).at[1, 0].set(loss)
    return jnp.concatenate([gain.reshape(8, BLOCK), gain_f.reshape(8, BLOCK), z], axis=0)


def _unpack_small(p):
    return p[0:8].reshape(1, D_MODEL), p[8:16].reshape(D_MODEL), p[16:17, :SWA_Q_HEADS], p[17, 0]


def kernel(x, meta_tokens, norm_gain, w_in, w_branch_sb, w_branch_swa, w_out, attn_sinks, final_norm_gain, loss_target, m_meta_tokens, m_norm_gain, m_w_in, m_w_branch_sb, m_w_branch_swa, m_w_out, m_attn_sinks, m_final_norm_gain, v_meta_tokens, v_norm_gain, v_w_in, v_w_branch_sb, v_w_branch_swa, v_w_out, v_attn_sinks, v_final_norm_gain):
    meta_bits = lax.bitcast_convert_type(meta_tokens, BF16).reshape(2 * N_META, BLOCK)
    mine = jnp.concatenate([_pack_shards(w_in, w_branch_sb, w_branch_swa, w_out, meta_tokens)[:-N_META].astype(BF16),
                            meta_bits], axis=0)
    full = all_gather(mine, "all_gather_weights")
    o = np.cumsum([0, ROWS_W_IN, ROWS_W_BSB, ROWS_W_ROWSHARD, ROWS_W_ROWSHARD, 2 * N_META])
    cols = lambda a: a.transpose(1, 0, 2).reshape(a.shape[1], -1)
    f_w_in = cols(full[:, o[0]:o[1]].reshape(N_DEV, D_MODEL, W_IN_SHARD))
    f_w_bsb = cols(full[:, o[1]:o[2]])
    f_w_bswa = full[:, o[2]:o[3]].reshape(D_MODEL, D_MODEL)
    f_w_out = full[:, o[3]:o[4]].reshape(D_MODEL, D_MODEL)
    f_meta = cols(lax.bitcast_convert_type(full[:, o[4]:o[5]].reshape(N_DEV, N_META, BLOCK, 2), F32))

    (loss, grad_x, dmeta, dgain, dw_in, dw_bsb, dw_bswa, dw_out, dsinks, dgf) = local_step(
        x[0], loss_target[0], f_meta, norm_gain, f_w_in, f_w_bsb, f_w_bswa, f_w_out, attn_sinks,
        final_norm_gain.reshape(1, D_MODEL))

    parts = exchange_partials(_pack_by_owner(dw_in, dw_bsb, dw_bswa, dw_out, dmeta))
    packs = [_pack_shards(a[0], b[0], c[0], d[0], e) for a, b, c, d, e in (
        (w_in, w_branch_sb, w_branch_swa, w_out, meta_tokens),
        (m_w_in, m_w_branch_sb, m_w_branch_swa, m_w_out, m_meta_tokens),
        (v_w_in, v_w_branch_sb, v_w_branch_swa, v_w_out, v_meta_tokens))]
    big = [_unpack_shards(p) for p in sum_and_adamw(parts, *packs, "sum_adamw_sharded")]

    small = all_gather(_pack_small(dgain, dgf, dsinks, loss), "all_gather_small")
    zero = jnp.zeros((), F32)
    spacks = [_pack_small(a, b, c, zero) for a, b, c in (
        (norm_gain, final_norm_gain, attn_sinks), (m_norm_gain, m_final_norm_gain, m_attn_sinks),
        (v_norm_gain, v_final_norm_gain, v_attn_sinks))]
    sm = [_unpack_small(p) for p in sum_and_adamw(small, *spacks, "sum_adamw_replicated")]

    def leaves(k):
        b, s = big[k], sm[k]
        return (b[4], s[0], b[0], b[1], b[2], b[3], s[2], s[1])

    return (sm[0][3], grad_x[None], *leaves(0), *leaves(1), *leaves(2), *leaves(3))
```

```python
import numpy as np
import jax
import jax.numpy as jnp
from jax import lax
from jax.experimental import pallas as pl
from jax.experimental.pallas import tpu as pltpu

F32 = jnp.float32
BF16 = jnp.bfloat16

D_MODEL = 1024
N_META = 16
BLOCK = 128
PAD = BLOCK - N_META
HEAD_DIM = 64
SB_HEADS = 8
SB_WIDTH = SB_HEADS * HEAD_DIM
SWA_Q_HEADS = 16
SWA_KV_HEADS = 2
SWA_WIDTH = SWA_Q_HEADS * HEAD_DIM
SWA_KV_WIDTH = SWA_KV_HEADS * HEAD_DIM
ROPE_THETA = 10000.0
RMS_EPS = 1e-6
SCALE = HEAD_DIM ** -0.5
SPLITS = (SB_WIDTH, SB_WIDTH, SB_WIDTH, SWA_WIDTH, SWA_KV_WIDTH, SWA_KV_WIDTH,
          SB_WIDTH, SWA_WIDTH, D_MODEL, D_MODEL)
IN_COLS = sum(SPLITS)
SB_COLS = 3 * SB_WIDTH
SWA_COLS = SWA_WIDTH + 2 * SWA_KV_WIDTH
GATE_COL0 = SB_COLS + SWA_COLS

N_DEV = 8
ADAM_LR = 0.001
ADAM_B1 = 0.9
ADAM_B2 = 0.999
ADAM_EPS = 1e-08
ADAM_WD = 0.01
ADAM_STEP = 10

VMEM_LIMIT = 56 * 1024 * 1024


def _params(sem, **kw):
    return pltpu.CompilerParams(dimension_semantics=sem, vmem_limit_bytes=VMEM_LIMIT, **kw)


def _dot(a, b):
    return jnp.dot(a, b, preferred_element_type=F32)


def _dot_nt(a, b):
    return lax.dot_general(a, b, (((1,), (1,)), ((), ())), preferred_element_type=F32)


def _dot_tn(a, b):
    return lax.dot_general(a, b, (((0,), (0,)), ((), ())), preferred_element_type=F32)


def _cat(xs, axis):
    return xs[0] if len(xs) == 1 else jnp.concatenate(xs, axis=axis)


def _split_bf16(x):
    hi = x.astype(BF16)
    lo = (x - hi.astype(F32)).astype(BF16)
    return jnp.concatenate([hi, lo], axis=1)


def _suffix_matrix():
    j = np.arange(BLOCK)[:, None]
    s = np.arange(BLOCK)[None, :]
    t = np.concatenate([(j >= s).astype(np.float32), np.ones((BLOCK, BLOCK), np.float32)], axis=1)
    return jnp.asarray(np.concatenate([t, t], axis=0), dtype=BF16)


def _softplus(z):
    return jnp.maximum(z, 0.0) + jnp.log(1.0 + jnp.exp(-jnp.abs(z)))


SB_GROUP = 4
SB_PAIRS = 2
SB_W = SB_PAIRS * BLOCK


def _sb_masks(i, diag):
    r = lax.broadcasted_iota(jnp.int32, (BLOCK, BLOCK), 0)
    c = lax.broadcasted_iota(jnp.int32, (BLOCK, BLOCK), 1)
    if diag:
        return (c < r) & (i * BLOCK + c >= PAD)
    return c >= PAD


def _sb_heads():
    return [(pr, a) for pr in range(SB_PAIRS) for a in range(2)]


def _lanes(pr):
    return slice(pr * BLOCK, (pr + 1) * BLOCK)


def _masked_heads(x, half0):
    out = []
    for pr, a in _sb_heads():
        xp = x[:, _lanes(pr)]
        out.append((jnp.where(half0, xp, 0.0) if a == 0 else jnp.where(half0, 0.0, xp)).astype(BF16))
    return out


def _sb_sweep(i, tile, carry):
    carry = tile(i, 1, _sb_masks(i, True), carry)
    n_mid = jnp.maximum(i - 1, 0)
    n_grp = n_mid // SB_GROUP
    n_one = n_mid - n_grp * SB_GROUP
    carry = lax.fori_loop(0, n_one, lambda t, cr: tile(i - 1 - t, 1, None, cr), carry)
    carry = lax.fori_loop(0, n_grp, lambda t, cr: tile(1 + (n_grp - 1 - t) * SB_GROUP, SB_GROUP, None, cr), carry)
    return lax.cond(i > 0, lambda cr: tile(0, 1, _sb_masks(i, False), cr), lambda cr: cr, carry)


def _sb_weights(zs, ss, cs, tt, nblk, mask):
    suf = _dot(_cat([_split_bf16(s[:, b * BLOCK:(b + 1) * BLOCK]) for s in ss for b in range(nblk)], 0), tt)
    ws, out_cs = [], []
    for h in range(len(zs)):
        c = cs[h]
        wb = [None] * nblk
        for b in reversed(range(nblk)):
            sab = suf[(h * nblk + b) * BLOCK:(h * nblk + b + 1) * BLOCK]
            wb[b] = jnp.exp(zs[h][:, b * BLOCK:(b + 1) * BLOCK] + c - sab[:, :BLOCK])
            c = c - sab[:, BLOCK:]
        w = _cat(wb, 1)
        if mask is not None:
            w = jnp.where(mask, w, 0.0)
        ws.append(w)
        out_cs.append(c)
    return ws, out_cs


def sb_attention_fwd(qkv, tt):
    L = qkv.shape[0]
    nq = SB_WIDTH // SB_W

    def body(q_ref, k_ref, v_ref, tt_ref, o_ref):
        i = pl.program_id(1)
        half0 = lax.broadcasted_iota(jnp.int32, (1, BLOCK), 1) < HEAD_DIM
        qh = _masked_heads(q_ref[...].astype(F32) * SCALE, half0)
        heads = _sb_heads()

        def tile(j0, nblk, mask, carry):
            accs, cs = carry
            rows = pl.ds(pl.multiple_of(j0 * BLOCK, BLOCK), nblk * BLOCK)
            zs = [_dot_nt(qh[h], k_ref[rows, _lanes(pr)]) for h, (pr, a) in enumerate(heads)]
            ss = [_softplus(z) for z in zs]
            if mask is not None:
                ss = [jnp.where(mask, s, 0.0) for s in ss]
            ws, cs = _sb_weights(zs, ss, cs, tt_ref[...], nblk, mask)
            accs = [accs[h] + _dot(ws[h].astype(BF16), v_ref[rows, _lanes(pr)]) for h, (pr, a) in enumerate(heads)]
            return tuple(accs), tuple(cs)

        zero = (jnp.zeros((BLOCK, BLOCK), F32),) * len(heads)
        accs, _ = _sb_sweep(i, tile, (zero, zero))
        o_ref[...] = _cat([jnp.where(half0, accs[2 * pr], accs[2 * pr + 1]) for pr in range(SB_PAIRS)], 1)

    return pl.pallas_call(
        body,
        name="sb_attention_fwd",
        grid=(nq, L // BLOCK),
        in_specs=[
            pl.BlockSpec((BLOCK, SB_W), lambda p, i: (i, p)),
            pl.BlockSpec((L, SB_W), lambda p, i: (0, nq + p)),
            pl.BlockSpec((L, SB_W), lambda p, i: (0, 2 * nq + p)),
            pl.BlockSpec((2 * BLOCK, 2 * BLOCK), lambda p, i: (0, 0)),
        ],
        out_specs=pl.BlockSpec((BLOCK, SB_W), lambda p, i: (i, p)),
        out_shape=jax.ShapeDtypeStruct((L, SB_WIDTH), F32),
        compiler_params=_params(("arbitrary", "arbitrary")),
    )(qkv, qkv, qkv, tt)


def sb_attention_bwd(qkv, o_sb, do_sb, tt):
    L = qkv.shape[0]
    nb = L // BLOCK
    nq = SB_WIDTH // SB_W

    def body(q_ref, k_ref, v_ref, o_ref, do_ref, tt_ref, dq_ref, dk_ref, dv_ref, dk_acc, dv_acc):
        i = pl.program_id(1)
        half0 = lax.broadcasted_iota(jnp.int32, (1, BLOCK), 1) < HEAD_DIM
        heads = _sb_heads()

        @pl.when(i == 0)
        def _():
            dk_acc[...] = jnp.zeros_like(dk_acc)
            dv_acc[...] = jnp.zeros_like(dv_acc)

        qh = _masked_heads(q_ref[...].astype(F32) * SCALE, half0)
        do = do_ref[...]
        doh = _masked_heads(do, half0)
        od = o_ref[...] * do.astype(F32)
        dsum = []
        for pr, a in heads:
            x = od[:, _lanes(pr)]
            x = jnp.where(half0, x, 0.0) if a == 0 else jnp.where(half0, 0.0, x)
            dsum.append(jnp.broadcast_to(jnp.sum(x, axis=1, keepdims=True), (BLOCK, BLOCK)))

        def tile(j0, nblk, mask, carry):
            accs, cs, ces = carry
            rows = pl.ds(pl.multiple_of(j0 * BLOCK, BLOCK), nblk * BLOCK)
            zs = [_dot_nt(qh[h], k_ref[rows, _lanes(pr)]) for h, (pr, a) in enumerate(heads)]
            dws = [_dot_nt(doh[h], v_ref[rows, _lanes(pr)]) for h, (pr, a) in enumerate(heads)]
            ss = [_softplus(z) for z in zs]
            sigs = [jnp.exp(z - s) for z, s in zip(zs, ss)]
            if mask is not None:
                ss = [jnp.where(mask, s, 0.0) for s in ss]
            ws, cs = _sb_weights(zs, ss, cs, tt_ref[...], nblk, mask)
            wbs = [w.astype(BF16) for w in ws]
            es = [wb.astype(F32) * dw for wb, dw in zip(wbs, dws)]
            esuf = _dot(_cat([_split_bf16(e[:, b * BLOCK:(b + 1) * BLOCK]) for e in es for b in range(nblk)], 0),
                        tt_ref[...])
            out_accs, out_ces = [], []
            dks, dvs = [None] * SB_PAIRS, [None] * SB_PAIRS
            for h, (pr, a) in enumerate(heads):
                ce = ces[h]
                dzs = [None] * nblk
                for b in reversed(range(nblk)):
                    eab = esuf[(h * nblk + b) * BLOCK:(h * nblk + b + 1) * BLOCK]
                    sl = slice(b * BLOCK, (b + 1) * BLOCK)
                    e = es[h][:, sl]
                    dzs[b] = e - sigs[h][:, sl] * (e + (ce - eab[:, :BLOCK]))
                    ce = ce - eab[:, BLOCK:]
                dz = _cat(dzs, 1)
                if mask is not None:
                    dz = jnp.where(mask, dz, 0.0)
                dzb = dz.astype(BF16)
                out_accs.append(accs[h] + _dot(dzb, k_ref[rows, _lanes(pr)]))
                dkh = _dot_tn(dzb, qh[h])
                dvh = _dot_tn(wbs[h], doh[h])
                dks[pr] = dkh if dks[pr] is None else dks[pr] + dkh
                dvs[pr] = dvh if dvs[pr] is None else dvs[pr] + dvh
                out_ces.append(ce)
            dk_acc[rows, :] += _cat(dks, 1)
            dv_acc[rows, :] += _cat(dvs, 1)
            return tuple(out_accs), tuple(cs), tuple(out_ces)

        zero = (jnp.zeros((BLOCK, BLOCK), F32),) * len(heads)
        accs = _sb_sweep(i, tile, (zero, zero, tuple(dsum)))[0]
        dq_ref[...] = (_cat([jnp.where(half0, accs[2 * pr], accs[2 * pr + 1]) for pr in range(SB_PAIRS)], 1)
                       * SCALE).astype(BF16)

        @pl.when(i == nb - 1)
        def _():
            dk_ref[...] = dk_acc[...].astype(BF16)
            dv_ref[...] = dv_acc[...].astype(BF16)

    blk = pl.BlockSpec((BLOCK, SB_W), lambda p, i: (i, p))
    panel = pl.BlockSpec((L, SB_W), lambda p, i: (0, p))
    return pl.pallas_call(
        body,
        name="sb_attention_bwd",
        grid=(nq, nb),
        in_specs=[
            blk,
            pl.BlockSpec((L, SB_W), lambda p, i: (0, nq + p)),
            pl.BlockSpec((L, SB_W), lambda p, i: (0, 2 * nq + p)),
            blk, blk,
            pl.BlockSpec((2 * BLOCK, 2 * BLOCK), lambda p, i: (0, 0)),
        ],
        out_specs=[blk, panel, panel],
        out_shape=[jax.ShapeDtypeStruct((L, SB_WIDTH), BF16)] * 3,
        scratch_shapes=[pltpu.VMEM((L, SB_W), F32), pltpu.VMEM((L, SB_W), F32)],
        compiler_params=_params(("arbitrary", "arbitrary")),
    )(qkv, qkv, qkv, o_sb, do_sb, tt)


SWA_PAIRS = SWA_WIDTH // BLOCK
PAIRS_PER_KV = SWA_PAIRS // SWA_KV_HEADS
CB_SWK = SWA_PAIRS
CB_SWV = SWA_PAIRS + 1


def rope_tables(L):
    half = HEAD_DIM // 2
    inv = ROPE_THETA ** (-jnp.arange(half, dtype=F32) / half)
    pos = (jnp.arange(L) - PAD).astype(F32)
    ang = pos[:, None] * inv[None, :]
    reps = BLOCK // half
    return jnp.tile(jnp.cos(ang), (1, reps)), jnp.tile(jnp.sin(ang), (1, reps))


def _rot_half(x):
    lane = lax.broadcasted_iota(jnp.int32, (1, BLOCK), 1)
    first = (lane % HEAD_DIM) < (HEAD_DIM // 2)
    return jnp.where(first, -pltpu.roll(x, BLOCK - HEAD_DIM // 2, axis=1), pltpu.roll(x, HEAD_DIM // 2, axis=1))


def _rope(x, cos, sin):
    return x * cos + _rot_half(x) * sin


def _unrope(x, cos, sin):
    return x * cos - _rot_half(x) * sin


def _swa_specs():
    prev = lambda n: jnp.maximum(n - 1, 0)
    cur = lambda n: n
    blk = lambda f, c: pl.BlockSpec((BLOCK, BLOCK), lambda n: (f(n), c))
    return [
        pl.BlockSpec((BLOCK, SWA_WIDTH), lambda n: (n, 0)),
        blk(prev, CB_SWK), blk(cur, CB_SWK), blk(prev, CB_SWV), blk(cur, CB_SWV),
        blk(prev, 0), blk(cur, 0), blk(prev, 0), blk(cur, 0),
        pl.BlockSpec(memory_space=pltpu.SMEM),
    ]


def _swa_probs(n, q_ref, kp_ref, kc_ref, vp_ref, vc_ref, cp_ref, cc_ref, sp_ref, sc_ref, sink_ref):
    lane = lax.broadcasted_iota(jnp.int32, (1, BLOCK), 1)
    halves = (lane < HEAD_DIM, lane >= HEAD_DIM)
    cosc, sinc = cc_ref[...], sc_ref[...]
    qs = [_rope(q_ref[:, p * BLOCK:(p + 1) * BLOCK], cosc, sinc) * SCALE for p in range(SWA_PAIRS)]
    kb = jnp.concatenate([_rope(kp_ref[...], cp_ref[...], sp_ref[...]), _rope(kc_ref[...], cosc, sinc)], axis=0)
    vb = jnp.concatenate([vp_ref[...], vc_ref[...]], axis=0)
    kv = {True: (kb, vb), False: (pltpu.roll(kb, HEAD_DIM, axis=1), pltpu.roll(vb, HEAD_DIM, axis=1))}
    rows = PAIRS_PER_KV * BLOCK
    r = lax.broadcasted_iota(jnp.int32, (rows, 2 * BLOCK), 0) % BLOCK
    c = lax.broadcasted_iota(jnp.int32, (rows, 2 * BLOCK), 1)
    valid = (c > r) & (c <= r + BLOCK) & ((n - 1) * BLOCK + c >= PAD)
    combos = [(g, a) for g in range(SWA_KV_HEADS) for a in range(2)]
    qst, ksel, vsel, scores = {}, {}, {}, {}
    for g, a in combos:
        qst[g, a] = jnp.concatenate(
            [jnp.where(halves[a], qs[g * PAIRS_PER_KV + j], 0.0) for j in range(PAIRS_PER_KV)], axis=0).astype(BF16)
        ksel[g, a], vsel[g, a] = kv[g == a]
    for g, a in combos:
        scores[g, a] = _dot_nt(qst[g, a], ksel[g, a].astype(BF16))
    out = {}
    for g, a in combos:
        s = jnp.where(valid, scores[g, a], -1e30)
        sink = jnp.concatenate([jnp.full((BLOCK, 1), sink_ref[0, 2 * (g * PAIRS_PER_KV + j) + a], F32)
                                for j in range(PAIRS_PER_KV)], axis=0)
        mx = jnp.maximum(jnp.max(s, axis=1, keepdims=True), sink)
        pe = jnp.exp(s - mx)
        es = jnp.exp(sink - mx)
        inv = 1.0 / (jnp.sum(pe, axis=1, keepdims=True) + es)
        out[g, a] = (qst[g, a], ksel[g, a], vsel[g, a], pe * inv, es * inv, halves[a])
    return combos, out


def swa_attention_fwd(proj, cos, sin, sinks):
    L = proj.shape[0]

    def body(q_ref, kp_ref, kc_ref, vp_ref, vc_ref, cp_ref, cc_ref, sp_ref, sc_ref, sink_ref, o_ref):
        n = pl.program_id(0)
        combos, parts = _swa_probs(n, q_ref, kp_ref, kc_ref, vp_ref, vc_ref, cp_ref, cc_ref, sp_ref, sc_ref, sink_ref)
        outs = {}
        for g, a in combos:
            qst, ksel, vsel, probs, psink, half = parts[g, a]
            outs[g, a] = _dot(probs.astype(BF16), jnp.where(half, vsel, 0.0).astype(BF16))
        for g in range(SWA_KV_HEADS):
            both = outs[g, 0] + outs[g, 1]
            for j in range(PAIRS_PER_KV):
                p = g * PAIRS_PER_KV + j
                o_ref[:, p * BLOCK:(p + 1) * BLOCK] = both[j * BLOCK:(j + 1) * BLOCK]

    return pl.pallas_call(
        body,
        name="swa_attention_fwd",
        grid=(L // BLOCK,),
        in_specs=_swa_specs(),
        out_specs=pl.BlockSpec((BLOCK, SWA_WIDTH), lambda n: (n, 0)),
        out_shape=jax.ShapeDtypeStruct((L, SWA_WIDTH), F32),
        compiler_params=_params(("arbitrary",)),
    )(proj, proj, proj, proj, proj, cos, cos, sin, sin, sinks)


def swa_attention_bwd(proj, cos, sin, sinks, do_sw):
    L = proj.shape[0]

    def body(q_ref, kp_ref, kc_ref, vp_ref, vc_ref, cp_ref, cc_ref, sp_ref, sc_ref, sink_ref, do_ref,
             dq_ref, dk_ref, dv_ref, ds_ref):
        n = pl.program_id(0)

        @pl.when(n == 0)
        def _():
            dk_ref[...] = jnp.zeros_like(dk_ref)
            dv_ref[...] = jnp.zeros_like(dv_ref)
            ds_ref[...] = jnp.zeros_like(ds_ref)

        combos, parts = _swa_probs(n, q_ref, kp_ref, kc_ref, vp_ref, vc_ref, cp_ref, cc_ref, sp_ref, sc_ref, sink_ref)
        lane8 = lax.broadcasted_iota(jnp.int32, (8, BLOCK), 1)
        dos, dps = {}, {}
        for g, a in combos:
            half = parts[g, a][5]
            dos[g, a] = jnp.concatenate(
                [jnp.where(half, do_ref[:, (g * PAIRS_PER_KV + j) * BLOCK:(g * PAIRS_PER_KV + j + 1) * BLOCK], 0.0)
                 for j in range(PAIRS_PER_KV)], axis=0).astype(BF16)
        for g, a in combos:
            dps[g, a] = _dot_nt(dos[g, a], parts[g, a][2].astype(BF16))
        dqs = {}
        dkb = jnp.zeros((2 * BLOCK, BLOCK), F32)
        dvb = jnp.zeros((2 * BLOCK, BLOCK), F32)
        dsk = jnp.zeros((8, BLOCK), F32)
        for g, a in combos:
            qst, ksel, vsel, probs, psink, half = parts[g, a]
            dp = dps[g, a]
            delta = jnp.sum(probs * dp, axis=1, keepdims=True)
            ds = (probs * (dp - delta)).astype(BF16)
            pd = psink * delta
            for j in range(PAIRS_PER_KV):
                head = 2 * (g * PAIRS_PER_KV + j) + a
                dsk = dsk + jnp.where(lane8 == head, -jnp.sum(pd[j * BLOCK:(j + 1) * BLOCK]), 0.0)
            dqs[g, a] = _dot(ds, jnp.where(half, ksel, 0.0).astype(BF16))
            dk_a = _dot_tn(ds, qst)
            dv_a = _dot_tn(probs.astype(BF16), dos[g, a])
            if g != a:
                dk_a = pltpu.roll(dk_a, HEAD_DIM, axis=1)
                dv_a = pltpu.roll(dv_a, HEAD_DIM, axis=1)
            dkb = dkb + dk_a
            dvb = dvb + dv_a
        cosc, sinc = cc_ref[...], sc_ref[...]
        for g in range(SWA_KV_HEADS):
            both = (dqs[g, 0] + dqs[g, 1]) * SCALE
            for j in range(PAIRS_PER_KV):
                p = g * PAIRS_PER_KV + j
                dq_ref[:, p * BLOCK:(p + 1) * BLOCK] = _unrope(both[j * BLOCK:(j + 1) * BLOCK], cosc, sinc).astype(BF16)
        ds_ref[...] += dsk
        cur = pl.ds(pl.multiple_of(n * BLOCK, BLOCK), BLOCK)
        dk_ref[cur, :] += _unrope(dkb[BLOCK:], cosc, sinc)
        dv_ref[cur, :] += dvb[BLOCK:]

        @pl.when(n > 0)
        def _():
            prv = pl.ds(pl.multiple_of((n - 1) * BLOCK, BLOCK), BLOCK)
            dk_ref[prv, :] += _unrope(dkb[:BLOCK], cp_ref[...], sp_ref[...])
            dv_ref[prv, :] += dvb[:BLOCK]

    whole = lambda n: (0, 0)
    row = pl.BlockSpec((BLOCK, SWA_WIDTH), lambda n: (n, 0))
    return pl.pallas_call(
        body,
        name="swa_attention_bwd",
        grid=(L // BLOCK,),
        in_specs=_swa_specs() + [row],
        out_specs=[row, pl.BlockSpec((L, BLOCK), whole), pl.BlockSpec((L, BLOCK), whole),
                   pl.BlockSpec((8, BLOCK), whole)],
        out_shape=[jax.ShapeDtypeStruct((L, SWA_WIDTH), BF16), jax.ShapeDtypeStruct((L, BLOCK), F32),
                   jax.ShapeDtypeStruct((L, BLOCK), F32), jax.ShapeDtypeStruct((8, BLOCK), F32)],
        compiler_params=_params(("arbitrary",)),
    )(proj, proj, proj, proj, proj, cos, cos, sin, sin, sinks, do_sw)


ROW_TILE = 640
TAIL_ROWS = 208


def _pick(n, cands):
    for c in cands:
        if n % c == 0:
            return c
    raise ValueError(f"no tile for {n}")


def in_proj(h0, gain, w, name, out_dtype):
    L, D = h0.shape
    N = w.shape[1]
    tm = _pick(L, (ROW_TILE, BLOCK))
    tn = _pick(N, (1792, 1536, 1280, 896, 640, 512, 384, 256, 128))

    def body(h_ref, g_ref, w_ref, o_ref, xn_ref):
        @pl.when(pl.program_id(1) == 0)
        def _():
            x = h_ref[...]
            r = lax.rsqrt(jnp.mean(x * x, axis=1, keepdims=True) + RMS_EPS)
            xn_ref[...] = ((x * r) * g_ref[...]).astype(BF16)
        o_ref[...] = _dot(xn_ref[...], w_ref[...]).astype(out_dtype)

    return pl.pallas_call(
        body,
        name=name,
        grid=(L // tm, N // tn),
        in_specs=[pl.BlockSpec((tm, D), lambda i, j: (i, 0)),
                  pl.BlockSpec((1, D), lambda i, j: (0, 0)),
                  pl.BlockSpec((D, tn), lambda i, j: (0, j))],
        out_specs=[pl.BlockSpec((tm, tn), lambda i, j: (i, j)),
                   pl.BlockSpec((tm, D), lambda i, j: (i, 0))],
        out_shape=[jax.ShapeDtypeStruct((L, N), out_dtype), jax.ShapeDtypeStruct((L, D), BF16)],
        compiler_params=_params(("arbitrary", "arbitrary")),
    )(h0, gain, w)


def matmul_nn(a, b, name):
    M, Kd = a.shape
    N = b.shape[1]
    tk = _pick(Kd, (ROW_TILE, BLOCK))
    tn = _pick(N, (1280, 1024, 896, 640, 512, 256, 128))
    nk = Kd // tk

    def body(a_ref, b_ref, o_ref):
        k = pl.program_id(1)

        @pl.when(k == 0)
        def _():
            o_ref[...] = jnp.zeros_like(o_ref)
        o_ref[...] += _dot(a_ref[...], b_ref[...])

    return pl.pallas_call(
        body,
        name=name,
        grid=(N // tn, nk),
        in_specs=[pl.BlockSpec((M, tk), lambda j, k: (0, k)),
                  pl.BlockSpec((tk, tn), lambda j, k: (k, j))],
        out_specs=pl.BlockSpec((M, tn), lambda j, k: (0, j)),
        out_shape=jax.ShapeDtypeStruct((M, N), F32),
        compiler_params=_params(("arbitrary", "arbitrary")),
    )(a, b)


def in_proj_bwd(dproj, w, h0, gain, dh1):
    L, N = dproj.shape
    D = w.shape[0]
    tm = _pick(L, (ROW_TILE, BLOCK))
    tk = _pick(N, (640, 512, 256, 128))
    nk = N // tk

    def body(dp_ref, w_ref, h_ref, g_ref, dh1_ref, dh0_ref, dg_ref, acc_ref):
        i, k = pl.program_id(0), pl.program_id(1)

        @pl.when(k == 0)
        def _():
            acc_ref[...] = jnp.zeros_like(acc_ref)

        @pl.when((i == 0) & (k == 0))
        def _():
            dg_ref[...] = jnp.zeros_like(dg_ref)

        acc_ref[...] += _dot_nt(dp_ref[...], w_ref[...])

        @pl.when(k == nk - 1)
        def _():
            x = h_ref[...]
            r = lax.rsqrt(jnp.mean(x * x, axis=1, keepdims=True) + RMS_EPS)
            xhat = x * r
            dxn = acc_ref[...]
            dg_ref[...] += jnp.sum(dxn * xhat, axis=0, keepdims=True)
            dxh = dxn * g_ref[...]
            dh0_ref[...] = r * (dxh - xhat * jnp.mean(dxh * xhat, axis=1, keepdims=True)) + dh1_ref[...]

    row = pl.BlockSpec((tm, D), lambda i, k: (i, 0))
    vec = pl.BlockSpec((1, D), lambda i, k: (0, 0))
    return pl.pallas_call(
        body,
        name="in_proj_bwd",
        grid=(L // tm, nk),
        in_specs=[pl.BlockSpec((tm, tk), lambda i, k: (i, k)),
                  pl.BlockSpec((D, tk), lambda i, k: (0, k)),
                  row, vec, row],
        out_specs=[row, vec],
        out_shape=[jax.ShapeDtypeStruct((L, D), F32), jax.ShapeDtypeStruct((1, D), F32)],
        scratch_shapes=[pltpu.VMEM((tm, D), F32)],
        compiler_params=_params(("arbitrary", "arbitrary")),
    )(dproj, w, h0, gain, dh1)


def tail_fwd_bwd(h0, tgt, o_sb, o_sw, gates, w_bsb, w_bswa, w_out, gain_f):
    L, D = h0.shape
    R = _pick(L, (TAIL_ROWS, BLOCK))
    z0, z1, z2, z3 = 0, SB_WIDTH, SB_WIDTH + SWA_WIDTH, SB_WIDTH + SWA_WIDTH + D_MODEL

    def body(h_ref, t_ref, osb_ref, osw_ref, g_ref, wsb_ref, wsw_ref, wo_ref, gf_ref,
             dosb_ref, dosw_ref, dg_ref, dh1_ref, mb_ref, usb_ref, usw_ref, dh1b_ref, dysb_ref, dysw_ref,
             dgf_ref, loss_ref):
        i = pl.program_id(0)

        @pl.when(i == 0)
        def _():
            dgf_ref[...] = jnp.zeros_like(dgf_ref)
            loss_ref[...] = jnp.zeros_like(loss_ref)

        sbz = g_ref[:, z0:z1]
        swz = g_ref[:, z1:z2]
        s1 = jax.nn.sigmoid(g_ref[:, z2:z3])
        s2 = jax.nn.sigmoid(g_ref[:, z3:])
        sg_sb = jax.nn.sigmoid(sbz)
        sg_sw = jax.nn.sigmoid(swz)
        silu_sb = sbz * sg_sb
        silu_sw = swz * sg_sw
        osb = osb_ref[...]
        osw = osw_ref[...]
        usb = (osb * silu_sb).astype(BF16)
        usw = (osw * silu_sw).astype(BF16)
        y_sb = _dot(usb, wsb_ref[...])
        y_sw = _dot(usw, wsw_ref[...])
        mb = (s1 * y_sb + s2 * y_sw).astype(BF16)
        h1 = h_ref[...] + _dot(mb, wo_ref[...])
        rf = lax.rsqrt(jnp.mean(h1 * h1, axis=1, keepdims=True) + RMS_EPS)
        hhat = h1 * rf
        gf = gf_ref[...]
        row = i * R + lax.broadcasted_iota(jnp.int32, (R, 1), 0)
        err = jnp.where(row >= BLOCK, hhat * gf - t_ref[...], 0.0)
        lane0 = (lax.broadcasted_iota(jnp.int32, (8, BLOCK), 0) == 0) & (lax.broadcasted_iota(jnp.int32, (8, BLOCK), 1) == 0)
        loss_ref[...] += jnp.where(lane0, (0.5 / D) * jnp.sum(err * err), 0.0)
        dy = err * (1.0 / D)
        dgf_ref[...] += jnp.sum(dy * hhat, axis=0, keepdims=True)
        dhh = dy * gf
        dh1 = rf * (dhh - hhat * jnp.mean(dhh * hhat, axis=1, keepdims=True))
        dh1b = dh1.astype(BF16)
        dm = _dot_nt(dh1b, wo_ref[...])
        dysb = (dm * s1).astype(BF16)
        dysw = (dm * s2).astype(BF16)
        dusb = _dot_nt(dysb, wsb_ref[...])
        dusw = _dot_nt(dysw, wsw_ref[...])
        dosb_ref[...] = (dusb * silu_sb).astype(BF16)
        dosw_ref[...] = (dusw * silu_sw).astype(BF16)
        dg_ref[:, z0:z1] = (dusb * osb * (sg_sb * (1.0 + sbz * (1.0 - sg_sb)))).astype(BF16)
        dg_ref[:, z1:z2] = (dusw * osw * (sg_sw * (1.0 + swz * (1.0 - sg_sw)))).astype(BF16)
        dg_ref[:, z2:z3] = (dm * y_sb * (s1 * (1.0 - s1))).astype(BF16)
        dg_ref[:, z3:] = (dm * y_sw * (s2 * (1.0 - s2))).astype(BF16)
        dh1_ref[...] = dh1
        mb_ref[...] = mb
        usb_ref[...] = usb
        usw_ref[...] = usw
        dh1b_ref[...] = dh1b
        dysb_ref[...] = dysb
        dysw_ref[...] = dysw

    def rows(n):
        return pl.BlockSpec((R, n), lambda i: (i, 0))

    def whole(shape):
        return pl.BlockSpec(shape, lambda i: (0, 0))

    GW = gates.shape[1]
    return pl.pallas_call(
        body,
        name="tail_fwd_bwd",
        grid=(L // R,),
        in_specs=[rows(D), rows(D), rows(SB_WIDTH), rows(SWA_WIDTH), rows(GW),
                  whole(w_bsb.shape), whole(w_bswa.shape), whole(w_out.shape), whole((1, D))],
        out_specs=[rows(SB_WIDTH), rows(SWA_WIDTH), rows(GW), rows(D),
                   rows(D), rows(SB_WIDTH), rows(SWA_WIDTH), rows(D), rows(D), rows(D),
                   whole((1, D)), whole((8, BLOCK))],
        out_shape=[jax.ShapeDtypeStruct((L, SB_WIDTH), BF16), jax.ShapeDtypeStruct((L, SWA_WIDTH), BF16),
                   jax.ShapeDtypeStruct((L, GW), BF16), jax.ShapeDtypeStruct((L, D), F32),
                   jax.ShapeDtypeStruct((L, D), BF16), jax.ShapeDtypeStruct((L, SB_WIDTH), BF16),
                   jax.ShapeDtypeStruct((L, SWA_WIDTH), BF16), jax.ShapeDtypeStruct((L, D), BF16),
                   jax.ShapeDtypeStruct((L, D), BF16), jax.ShapeDtypeStruct((L, D), BF16),
                   jax.ShapeDtypeStruct((1, D), F32), jax.ShapeDtypeStruct((8, BLOCK), F32)],
        compiler_params=_params(("arbitrary",)),
    )(h0, tgt, o_sb, o_sw, gates, w_bsb, w_bswa, w_out, gain_f)


def local_step(x, tgt, meta, gain, w_in, w_bsb, w_bswa, w_out, sinks, gain_f):
    S, D = x.shape
    L = S + BLOCK
    h0 = jnp.concatenate([jnp.zeros((PAD, D), F32), meta, x], axis=0)
    tgt_p = jnp.concatenate([jnp.zeros((BLOCK, D), F32), tgt], axis=0)
    tt = _suffix_matrix()
    cos, sin = rope_tables(L)
    qkv, xn = in_proj(h0, gain, w_in[:, :SB_COLS], "in_proj_sb", BF16)
    proj_sw, _ = in_proj(h0, gain, w_in[:, SB_COLS:GATE_COL0], "in_proj_swa", F32)
    gates, _ = in_proj(h0, gain, w_in[:, GATE_COL0:], "in_proj_gates", F32)
    o_sb = sb_attention_fwd(qkv, tt)
    o_sw = swa_attention_fwd(proj_sw, cos, sin, sinks)
    (do_sb, do_sw, dgates, dh1, mb, usb, usw, dh1b, dysb, dysw, dgf, loss) = tail_fwd_bwd(
        h0, tgt_p, o_sb, o_sw, gates, w_bsb, w_bswa, w_out, gain_f)
    dq_sb, dk_sb, dv_sb = sb_attention_bwd(qkv, o_sb, do_sb, tt)
    dq_sw, dk_sw, dv_sw, dsinks = swa_attention_bwd(proj_sw, cos, sin, sinks, do_sw)
    dproj = jnp.concatenate([dq_sb, dk_sb, dv_sb, dq_sw, dk_sw.astype(BF16), dv_sw.astype(BF16), dgates], axis=1)
    dw_in = matmul_nn(xn.T, dproj, "dw_in")
    dw_out = matmul_nn(mb.T, dh1b, "dw_out")
    dw_bsb = matmul_nn(usb.T, dysb, "dw_bsb")
    dw_bswa = matmul_nn(usw.T, dysw, "dw_bswa")
    dh0, dgain = in_proj_bwd(dproj, w_in, h0, gain, dh1)
    return (loss[0, 0], dh0[BLOCK:], dh0[PAD:BLOCK], dgain, dw_in, dw_bsb, dw_bswa, dw_out,
            dsinks[:1, :SWA_Q_HEADS], dgf)


MESH_IDS = pl.DeviceIdType.MESH
ANY = pl.BlockSpec(memory_space=pl.ANY)


def _place():
    return lax.axis_index("x"), lax.axis_index("y"), lax.axis_index("c")


def _index(x, y, c):
    return 4 * x + 2 * y + c


def all_gather(block, name):
    def body(x_ref, out_ref, send_sems, recv_sems, local_sem):
        x, y, c = _place()
        me, sibling = (x, y, c), (x, y, 1 - c)
        chips = [(1 - x, y), (x, 1 - y), (1 - x, 1 - y)]

        def copy(k, blk, to, src=None):
            dst = out_ref.at[_index(*blk)]
            return pltpu.make_async_remote_copy(
                src_ref=dst if src is None else src, dst_ref=dst,
                send_sem=send_sems.at[k], recv_sem=recv_sems.at[k], device_id=to, device_id_type=MESH_IDS)

        mine = pltpu.make_async_copy(x_ref, out_ref.at[_index(*me)], local_sem)
        mine.start()
        first = [copy(0, me, sibling, src=x_ref)]
        first += [copy(1 + j, me, (*chip, c), src=x_ref) for j, chip in enumerate(chips)]
        for cp in first:
            cp.start()
        passed = [copy(4 + j, (*chip, c), sibling) for j, chip in enumerate(chips)]
        for j, chip in enumerate(chips):
            copy(1 + j, (*chip, c), me).wait_recv()
            passed[j].start()
        copy(0, sibling, me).wait_recv()
        for j, chip in enumerate(chips):
            copy(4 + j, (*chip, 1 - c), me).wait_recv()
        for cp in first + passed:
            cp.wait_send()
        mine.wait()

    return pl.pallas_call(
        body,
        name=name,
        out_shape=jax.ShapeDtypeStruct((N_DEV,) + block.shape, block.dtype),
        in_specs=[ANY],
        out_specs=ANY,
        scratch_shapes=[pltpu.SemaphoreType.DMA((7,)), pltpu.SemaphoreType.DMA((7,)), pltpu.SemaphoreType.DMA],
    )(block)


def exchange_partials(parts):
    def body(g_ref, out_ref, send_sems, recv_sems, local_sem):
        x, y, c = _place()
        me = _index(x, y, c)
        mine = pltpu.make_async_copy(g_ref.at[me], out_ref.at[me], local_sem)
        mine.start()
        copies = []
        for m in range(1, N_DEV):
            px = 1 - x if m & 4 else x
            py = 1 - y if m & 2 else y
            pc = 1 - c if m & 1 else c
            cp = pltpu.make_async_remote_copy(
                src_ref=g_ref.at[_index(px, py, pc)], dst_ref=out_ref.at[me],
                send_sem=send_sems.at[m - 1], recv_sem=recv_sems.at[m - 1],
                device_id=(px, py, pc), device_id_type=MESH_IDS)
            cp.start()
            copies.append(cp)
        for cp in copies:
            cp.wait()
        mine.wait()

    return pl.pallas_call(
        body,
        name="exchange_partials",
        out_shape=jax.ShapeDtypeStruct(parts.shape, parts.dtype),
        in_specs=[ANY],
        out_specs=ANY,
        scratch_shapes=[pltpu.SemaphoreType.DMA((7,)), pltpu.SemaphoreType.DMA((7,)), pltpu.SemaphoreType.DMA],
    )(parts)


def _adamw(w, g, m, v):
    m = ADAM_B1 * m + (1.0 - ADAM_B1) * g
    v = ADAM_B2 * v + (1.0 - ADAM_B2) * (g * g)
    m_hat = m / (1.0 - ADAM_B1 ** ADAM_STEP)
    v_hat = v / (1.0 - ADAM_B2 ** ADAM_STEP)
    delta = -ADAM_LR * (m_hat / (jnp.sqrt(v_hat) + ADAM_EPS) + ADAM_WD * w)
    return delta, m, v


def sum_and_adamw(parts, w, m, v, name):
    _, R, C = parts.shape
    tr = _pick(R, (528, 512, 256, 128, 24, 8))

    def body(p_ref, w_ref, m_ref, v_ref, g_ref, d_ref, nm_ref, nv_ref):
        g = p_ref[0]
        for s in range(1, N_DEV):
            g = g + p_ref[s]
        d, nm, nv = _adamw(w_ref[...], g, m_ref[...], v_ref[...])
        g_ref[...] = g
        d_ref[...] = d
        nm_ref[...] = nm
        nv_ref[...] = nv

    row = pl.BlockSpec((tr, C), lambda i: (i, 0))
    return pl.pallas_call(
        body,
        name=name,
        grid=(R // tr,),
        in_specs=[pl.BlockSpec((N_DEV, tr, C), lambda i: (0, i, 0)), row, row, row],
        out_specs=[row, row, row, row],
        out_shape=[jax.ShapeDtypeStruct((R, C), F32)] * 4,
        compiler_params=_params(("arbitrary",)),
    )(parts, w, m, v)


W_IN_SHARD = IN_COLS // N_DEV
ROWS_W_IN = D_MODEL * W_IN_SHARD // BLOCK
ROWS_W_BSB = SB_WIDTH
ROWS_W_ROWSHARD = D_MODEL
SMALL_ROWS = 24


def _pack_shards(w_in, w_bsb, w_bswa, w_out, meta):
    return jnp.concatenate([w_in.reshape(ROWS_W_IN, BLOCK), w_bsb.reshape(ROWS_W_BSB, BLOCK),
                            w_bswa.reshape(ROWS_W_ROWSHARD, BLOCK), w_out.reshape(ROWS_W_ROWSHARD, BLOCK),
                            meta.reshape(N_META, BLOCK)], axis=0)


def _unpack_shards(p):
    o = np.cumsum([0, ROWS_W_IN, ROWS_W_BSB, ROWS_W_ROWSHARD, ROWS_W_ROWSHARD, N_META])
    return (p[o[0]:o[1]].reshape(1, D_MODEL, W_IN_SHARD), p[o[1]:o[2]].reshape(1, SB_WIDTH, BLOCK),
            p[o[2]:o[3]].reshape(1, BLOCK, D_MODEL), p[o[3]:o[4]].reshape(1, BLOCK, D_MODEL),
            p[o[4]:o[5]].reshape(N_META, BLOCK))


def _pack_by_owner(dw_in, dw_bsb, dw_bswa, dw_out, dmeta):
    cols = lambda a: a.reshape(a.shape[0], N_DEV, -1).transpose(1, 0, 2)
    return jnp.concatenate([cols(dw_in).reshape(N_DEV, ROWS_W_IN, BLOCK), cols(dw_bsb),
                            dw_bswa.reshape(N_DEV, ROWS_W_ROWSHARD, BLOCK), dw_out.reshape(N_DEV, ROWS_W_ROWSHARD, BLOCK),
                            cols(dmeta)], axis=1)


def _pack_small(gain, gain_f, sinks, loss):
    z = jnp.zeros((SMALL_ROWS - 16, BLOCK), F32)
    z = z.at[0, :SWA_Q_HEADS].set(sinks.reshape(-1)).at[1, 0].set(loss)
    return jnp.concatenate([gain.reshape(8, BLOCK), gain_f.reshape(8, BLOCK), z], axis=0)


def _unpack_small(p):
    return p[0:8].reshape(1, D_MODEL), p[8:16].reshape(D_MODEL), p[16:17, :SWA_Q_HEADS], p[17, 0]


def kernel(x, meta_tokens, norm_gain, w_in, w_branch_sb, w_branch_swa, w_out, attn_sinks, final_norm_gain, loss_target, m_meta_tokens, m_norm_gain, m_w_in, m_w_branch_sb, m_w_branch_swa, m_w_out, m_attn_sinks, m_final_norm_gain, v_meta_tokens, v_norm_gain, v_w_in, v_w_branch_sb, v_w_branch_swa, v_w_out, v_attn_sinks, v_final_norm_gain):
    meta_bits = lax.bitcast_convert_type(meta_tokens, BF16).reshape(2 * N_META, BLOCK)
    mine = jnp.concatenate([_pack_shards(w_in, w_branch_sb, w_branch_swa, w_out, meta_tokens)[:-N_META].astype(BF16),
                            meta_bits], axis=0)
    full = all_gather(mine, "all_gather_weights")
    o = np.cumsum([0, ROWS_W_IN, ROWS_W_BSB, ROWS_W_ROWSHARD, ROWS_W_ROWSHARD, 2 * N_META])
    cols = lambda a: a.transpose(1, 0, 2).reshape(a.shape[1], -1)
    f_w_in = cols(full[:, o[0]:o[1]].reshape(N_DEV, D_MODEL, W_IN_SHARD))
    f_w_bsb = cols(full[:, o[1]:o[2]])
    f_w_bswa = full[:, o[2]:o[3]].reshape(D_MODEL, D_MODEL)
    f_w_out = full[:, o[3]:o[4]].reshape(D_MODEL, D_MODEL)
    f_meta = cols(lax.bitcast_convert_type(full[:, o[4]:o[5]].reshape(N_DEV, N_META, BLOCK, 2), F32))

    (loss, grad_x, dmeta, dgain, dw_in, dw_bsb, dw_bswa, dw_out, dsinks, dgf) = local_step(
        x[0], loss_target[0], f_meta, norm_gain, f_w_in, f_w_bsb, f_w_bswa, f_w_out, attn_sinks,
        final_norm_gain.reshape(1, D_MODEL))

    parts = exchange_partials(_pack_by_owner(dw_in, dw_bsb, dw_bswa, dw_out, dmeta))
    packs = [_pack_shards(a[0], b[0], c[0], d[0], e) for a, b, c, d, e in (
        (w_in, w_branch_sb, w_branch_swa, w_out, meta_tokens),
        (m_w_in, m_w_branch_sb, m_w_branch_swa, m_w_out, m_meta_tokens),
        (v_w_in, v_w_branch_sb, v_w_branch_swa, v_w_out, v_meta_tokens))]
    big = [_unpack_shards(p) for p in sum_and_adamw(parts, *packs, "sum_adamw_sharded")]

    small = all_gather(_pack_small(dgain, dgf, dsinks, loss), "all_gather_small")
    zero = jnp.zeros((), F32)
    spacks = [_pack_small(a, b, c, zero) for a, b, c in (
        (norm_gain, final_norm_gain, attn_sinks), (m_norm_gain, m_final_norm_gain, m_attn_sinks),
        (v_norm_gain, v_final_norm_gain, v_attn_sinks))]
    sm = [_unpack_small(p) for p in sum_and_adamw(small, *spacks, "sum_adamw_replicated")]

    def leaves(k):
        b, s = big[k], sm[k]
        return (b[4], s[0], b[0], b[1], b[2], b[3], s[2], s[1])

    return (sm[0][3], grad_x[None], *leaves(0), *leaves(1), *leaves(2), *leaves(3))
```

```python
import numpy as np
import jax
import jax.numpy as jnp
from jax import lax
from jax.experimental import pallas as pl
from jax.experimental.pallas import tpu as pltpu

F32 = jnp.float32
BF16 = jnp.bfloat16

D_MODEL = 1024
N_META = 16
BLOCK = 128
PAD = BLOCK - N_META
HEAD_DIM = 64
SB_HEADS = 8
SB_WIDTH = SB_HEADS * HEAD_DIM
SWA_Q_HEADS = 16
SWA_KV_HEADS = 2
SWA_WIDTH = SWA_Q_HEADS * HEAD_DIM
SWA_KV_WIDTH = SWA_KV_HEADS * HEAD_DIM
ROPE_THETA = 10000.0
RMS_EPS = 1e-6
SCALE = HEAD_DIM ** -0.5
SPLITS = (SB_WIDTH, SB_WIDTH, SB_WIDTH, SWA_WIDTH, SWA_KV_WIDTH, SWA_KV_WIDTH,
          SB_WIDTH, SWA_WIDTH, D_MODEL, D_MODEL)
IN_COLS = sum(SPLITS)
SB_COLS = 3 * SB_WIDTH
SWA_COLS = SWA_WIDTH + 2 * SWA_KV_WIDTH
GATE_COL0 = SB_COLS + SWA_COLS

N_DEV = 8
ADAM_LR = 0.001
ADAM_B1 = 0.9
ADAM_B2 = 0.999
ADAM_EPS = 1e-08
ADAM_WD = 0.01
ADAM_STEP = 10

VMEM_LIMIT = 56 * 1024 * 1024


def _params(sem, **kw):
    return pltpu.CompilerParams(dimension_semantics=sem, vmem_limit_bytes=VMEM_LIMIT, **kw)


def _dot(a, b):
    return jnp.dot(a, b, preferred_element_type=F32)


def _dot_nt(a, b):
    return lax.dot_general(a, b, (((1,), (1,)), ((), ())), preferred_element_type=F32)


def _dot_tn(a, b):
    return lax.dot_general(a, b, (((0,), (0,)), ((), ())), preferred_element_type=F32)


def _cat(xs, axis):
    return xs[0] if len(xs) == 1 else jnp.concatenate(xs, axis=axis)


def _split_bf16(x):
    hi = x.astype(BF16)
    lo = (x - hi.astype(F32)).astype(BF16)
    return jnp.concatenate([hi, lo], axis=1)


def _suffix_matrix():
    j = np.arange(BLOCK)[:, None]
    s = np.arange(BLOCK)[None, :]
    t = np.concatenate([(j >= s).astype(np.float32), np.ones((BLOCK, BLOCK), np.float32)], axis=1)
    return jnp.asarray(np.concatenate([t, t], axis=0), dtype=BF16)


def _softplus(z):
    return jnp.maximum(z, 0.0) + jnp.log(1.0 + jnp.exp(-jnp.abs(z)))


SB_SMALL = 4
SB_PAIRS_FWD = 4
SB_PAIRS_BWD = 2
SB_BIG_FWD = 4
SB_BIG_BWD = 4


def _sb_masks(i, nblk):
    if nblk == 0:
        r = lax.broadcasted_iota(jnp.int32, (BLOCK, BLOCK), 0)
        c = lax.broadcasted_iota(jnp.int32, (BLOCK, BLOCK), 1)
        return (c < r) & (i * BLOCK + c >= PAD)
    return lax.broadcasted_iota(jnp.int32, (BLOCK, nblk * BLOCK), 1) >= PAD


def _sb_heads(npairs):
    return [(pr, a) for pr in range(npairs) for a in range(2)]


def _lanes(pr):
    return slice(pr * BLOCK, (pr + 1) * BLOCK)


def _masked_heads(x, half0, npairs):
    out = []
    for pr, a in _sb_heads(npairs):
        xp = x[:, _lanes(pr)]
        out.append((jnp.where(half0, xp, 0.0) if a == 0 else jnp.where(half0, 0.0, xp)).astype(BF16))
    return out


def _sb_sweep(i, tile, carry, big):
    same = lambda cr: cr
    small = SB_SMALL
    carry = tile(i, 1, _sb_masks(i, 0), carry)
    n_big = i // big
    rest = i - n_big * big
    n_small = rest // small
    n_one = rest - n_small * small
    low_is_big = n_big > 0
    low_is_small = jnp.logical_not(low_is_big) & (n_small > 0)
    low_is_one = jnp.logical_not(low_is_big) & (n_small == 0) & (n_one > 0)

    n_plain = n_one - jnp.where(low_is_one, 1, 0)
    carry = lax.fori_loop(0, n_plain, lambda t, cr: tile(i - 1 - t, 1, None, cr), carry)
    carry = lax.cond(low_is_one, lambda cr: tile(0, 1, _sb_masks(i, 1), cr), same, carry)
    if big != small:
        base = n_big * big
        n_plain = n_small - jnp.where(low_is_small, 1, 0)
        carry = lax.fori_loop(0, n_plain, lambda t, cr: tile(base + (n_small - 1 - t) * small, small, None, cr), carry)
        carry = lax.cond(low_is_small, lambda cr: tile(0, small, _sb_masks(i, small), cr), same, carry)
    carry = lax.fori_loop(0, jnp.maximum(n_big - 1, 0), lambda t, cr: tile((n_big - 1 - t) * big, big, None, cr), carry)
    return lax.cond(low_is_big, lambda cr: tile(0, big, _sb_masks(i, big), cr), same, carry)


def _sb_weights(zs, ss, cs, tt, nblk, mask):
    suf = _dot(_cat([_split_bf16(s[:, b * BLOCK:(b + 1) * BLOCK]) for s in ss for b in range(nblk)], 0), tt)
    ws, out_cs = [], []
    for h in range(len(zs)):
        c = cs[h]
        wb = [None] * nblk
        for b in reversed(range(nblk)):
            sab = suf[(h * nblk + b) * BLOCK:(h * nblk + b + 1) * BLOCK]
            wb[b] = jnp.exp(zs[h][:, b * BLOCK:(b + 1) * BLOCK] + c - sab[:, :BLOCK])
            c = c - sab[:, BLOCK:]
        w = _cat(wb, 1)
        if mask is not None:
            w = jnp.where(mask, w, 0.0)
        ws.append(w)
        out_cs.append(c)
    return ws, out_cs


def sb_attention_fwd(qkv, tt):
    L = qkv.shape[0]
    NP = SB_PAIRS_FWD
    SB_W = NP * BLOCK
    nq = SB_WIDTH // SB_W

    def body(q_ref, k_ref, v_ref, tt_ref, o_ref):
        i = pl.program_id(1)
        half0 = lax.broadcasted_iota(jnp.int32, (1, BLOCK), 1) < HEAD_DIM
        qh = _masked_heads(q_ref[...].astype(F32) * SCALE, half0, NP)
        heads = _sb_heads(NP)

        def rows_of(j0, nblk):
            return pl.ds(pl.multiple_of(j0 * BLOCK, BLOCK), nblk * BLOCK)

        def tile(j0, nblk, mask, carry):
            accs, cs = carry
            rows = rows_of(j0, nblk)
            zs = [_dot_nt(qh[h], k_ref[rows, _lanes(pr)]) for h, (pr, a) in enumerate(heads)]
            ss = [_softplus(z) for z in zs]
            if mask is not None:
                ss = [jnp.where(mask, s, 0.0) for s in ss]
            ws, cs = _sb_weights(zs, ss, cs, tt_ref[...], nblk, mask)
            accs = [accs[h] + _dot(ws[h].astype(BF16), v_ref[rows, _lanes(pr)]) for h, (pr, a) in enumerate(heads)]
            return tuple(accs), tuple(cs)

        zero = (jnp.zeros((BLOCK, BLOCK), F32),) * len(heads)
        accs = _sb_sweep(i, tile, (zero, zero), SB_BIG_FWD)[0]
        o_ref[...] = _cat([jnp.where(half0, accs[2 * pr], accs[2 * pr + 1]) for pr in range(NP)], 1)

    return pl.pallas_call(
        body,
        name="sb_attention_fwd",
        grid=(nq, L // BLOCK),
        in_specs=[
            pl.BlockSpec((BLOCK, SB_W), lambda p, i: (i, p)),
            pl.BlockSpec((L, SB_W), lambda p, i: (0, nq + p)),
            pl.BlockSpec((L, SB_W), lambda p, i: (0, 2 * nq + p)),
            pl.BlockSpec((2 * BLOCK, 2 * BLOCK), lambda p, i: (0, 0)),
        ],
        out_specs=pl.BlockSpec((BLOCK, SB_W), lambda p, i: (i, p)),
        out_shape=jax.ShapeDtypeStruct((L, SB_WIDTH), F32),
        compiler_params=_params(("arbitrary", "arbitrary")),
    )(qkv, qkv, qkv, tt)


def sb_attention_bwd(qkv, o_sb, do_sb, tt):
    L = qkv.shape[0]
    nb = L // BLOCK
    NP = SB_PAIRS_BWD
    SB_W = NP * BLOCK
    nq = SB_WIDTH // SB_W

    def body(q_ref, k_ref, v_ref, o_ref, do_ref, tt_ref, dq_ref, dk_ref, dv_ref, dk_acc, dv_acc):
        i = pl.program_id(1)
        half0 = lax.broadcasted_iota(jnp.int32, (1, BLOCK), 1) < HEAD_DIM
        heads = _sb_heads(NP)

        @pl.when(i == 0)
        def _():
            dk_acc[...] = jnp.zeros_like(dk_acc)
            dv_acc[...] = jnp.zeros_like(dv_acc)

        qh = _masked_heads(q_ref[...].astype(F32) * SCALE, half0, NP)
        do = do_ref[...]
        doh = _masked_heads(do, half0, NP)
        od = o_ref[...] * do.astype(F32)
        dsum = []
        for pr, a in heads:
            x = od[:, _lanes(pr)]
            x = jnp.where(half0, x, 0.0) if a == 0 else jnp.where(half0, 0.0, x)
            dsum.append(jnp.broadcast_to(jnp.sum(x, axis=1, keepdims=True), (BLOCK, BLOCK)))

        def rows_of(j0, nblk):
            return pl.ds(pl.multiple_of(j0 * BLOCK, BLOCK), nblk * BLOCK)

        def tile(j0, nblk, mask, carry):
            accs, cs, ces = carry
            rows = rows_of(j0, nblk)
            zs = [_dot_nt(qh[h], k_ref[rows, _lanes(pr)]) for h, (pr, a) in enumerate(heads)]
            dws = [_dot_nt(doh[h], v_ref[rows, _lanes(pr)]) for h, (pr, a) in enumerate(heads)]
            ss = [_softplus(z) for z in zs]
            sigs = [jnp.exp(z - s) for z, s in zip(zs, ss)]
            if mask is not None:
                ss = [jnp.where(mask, s, 0.0) for s in ss]
            ws, cs = _sb_weights(zs, ss, cs, tt_ref[...], nblk, mask)
            wbs = [w.astype(BF16) for w in ws]
            es = [wb.astype(F32) * dw for wb, dw in zip(wbs, dws)]
            esuf = _dot(_cat([_split_bf16(e[:, b * BLOCK:(b + 1) * BLOCK]) for e in es for b in range(nblk)], 0),
                        tt_ref[...])
            out_ces, dzbs = [], []
            for h, (pr, a) in enumerate(heads):
                ce = ces[h]
                dzs = [None] * nblk
                for b in reversed(range(nblk)):
                    eab = esuf[(h * nblk + b) * BLOCK:(h * nblk + b + 1) * BLOCK]
                    sl = slice(b * BLOCK, (b + 1) * BLOCK)
                    e = es[h][:, sl]
                    dzs[b] = e - sigs[h][:, sl] * (e + (ce - eab[:, :BLOCK]))
                    ce = ce - eab[:, BLOCK:]
                dz = _cat(dzs, 1)
                if mask is not None:
                    dz = jnp.where(mask, dz, 0.0)
                dzbs.append(dz.astype(BF16))
                out_ces.append(ce)
            accs = tuple(accs[h] + _dot(dzbs[h], k_ref[rows, _lanes(pr)]) for h, (pr, a) in enumerate(heads))
            dkv = [_dot_tn(_cat([dzbs[2 * pr], wbs[2 * pr], dzbs[2 * pr + 1], wbs[2 * pr + 1]], 0), qdo[pr])
                   for pr in range(NP)]
            dk_acc[rows, :] += _cat([x[:, :BLOCK] for x in dkv], 1)
            dv_acc[rows, :] += _cat([x[:, BLOCK:] for x in dkv], 1)
            return accs, tuple(cs), tuple(out_ces)

        zb = jnp.zeros((BLOCK, BLOCK), BF16)
        qdo = [_cat([_cat([qh[h], zb], 1) if kind == 0 else _cat([zb, doh[h]], 1)
                     for h in (2 * pr, 2 * pr + 1) for kind in (0, 1)], 0) for pr in range(NP)]
        zero = (jnp.zeros((BLOCK, BLOCK), F32),) * len(heads)
        accs = _sb_sweep(i, tile, (zero, zero, tuple(dsum)), SB_BIG_BWD)[0]
        dq_ref[...] = (_cat([jnp.where(half0, accs[2 * pr], accs[2 * pr + 1]) for pr in range(NP)], 1)
                       * SCALE).astype(BF16)

        @pl.when(i == nb - 1)
        def _():
            dk_ref[...] = dk_acc[...].astype(BF16)
            dv_ref[...] = dv_acc[...].astype(BF16)

    blk = pl.BlockSpec((BLOCK, SB_W), lambda p, i: (i, p))
    panel = pl.BlockSpec((L, SB_W), lambda p, i: (0, p))
    return pl.pallas_call(
        body,
        name="sb_attention_bwd",
        grid=(nq, nb),
        in_specs=[
            blk,
            pl.BlockSpec((L, SB_W), lambda p, i: (0, nq + p), pipeline_mode=pl.Buffered(1)),
            pl.BlockSpec((L, SB_W), lambda p, i: (0, 2 * nq + p), pipeline_mode=pl.Buffered(1)),
            blk, blk,
            pl.BlockSpec((2 * BLOCK, 2 * BLOCK), lambda p, i: (0, 0)),
        ],
        out_specs=[blk, panel, panel],
        out_shape=[jax.ShapeDtypeStruct((L, SB_WIDTH), BF16)] * 3,
        scratch_shapes=[pltpu.VMEM((L, SB_W), F32), pltpu.VMEM((L, SB_W), F32)],
        compiler_params=_params(("arbitrary", "arbitrary")),
    )(qkv, qkv, qkv, o_sb, do_sb, tt)


SWA_PAIRS = SWA_WIDTH // BLOCK
PAIRS_PER_KV = SWA_PAIRS // SWA_KV_HEADS
CB_SWK = SWA_PAIRS
CB_SWV = SWA_PAIRS + 1


def rope_tables(L):
    half = HEAD_DIM // 2
    inv = ROPE_THETA ** (-jnp.arange(half, dtype=F32) / half)
    pos = (jnp.arange(L) - PAD).astype(F32)
    ang = pos[:, None] * inv[None, :]
    reps = BLOCK // half
    return jnp.tile(jnp.cos(ang), (1, reps)), jnp.tile(jnp.sin(ang), (1, reps))


def _rot_half(x):
    lane = lax.broadcasted_iota(jnp.int32, (1, BLOCK), 1)
    first = (lane % HEAD_DIM) < (HEAD_DIM // 2)
    return jnp.where(first, -pltpu.roll(x, BLOCK - HEAD_DIM // 2, axis=1), pltpu.roll(x, HEAD_DIM // 2, axis=1))


def _rope(x, cos, sin):
    return x * cos + _rot_half(x) * sin


def _unrope(x, cos, sin):
    return x * cos - _rot_half(x) * sin


def _swa_specs():
    prev = lambda n: jnp.maximum(n - 1, 0)
    cur = lambda n: n
    blk = lambda f, c: pl.BlockSpec((BLOCK, BLOCK), lambda n: (f(n), c))
    return [
        pl.BlockSpec((BLOCK, SWA_WIDTH), lambda n: (n, 0)),
        blk(prev, CB_SWK), blk(cur, CB_SWK), blk(prev, CB_SWV), blk(cur, CB_SWV),
        blk(prev, 0), blk(cur, 0), blk(prev, 0), blk(cur, 0),
        pl.BlockSpec(memory_space=pltpu.SMEM),
    ]


def _swa_probs(n, q_ref, kp_ref, kc_ref, vp_ref, vc_ref, cp_ref, cc_ref, sp_ref, sc_ref, sink_ref):
    lane = lax.broadcasted_iota(jnp.int32, (1, BLOCK), 1)
    halves = (lane < HEAD_DIM, lane >= HEAD_DIM)
    cosc, sinc = cc_ref[...], sc_ref[...]
    qs = [_rope(q_ref[:, p * BLOCK:(p + 1) * BLOCK], cosc, sinc) * SCALE for p in range(SWA_PAIRS)]
    kb = jnp.concatenate([_rope(kp_ref[...], cp_ref[...], sp_ref[...]), _rope(kc_ref[...], cosc, sinc)], axis=0)
    vb = jnp.concatenate([vp_ref[...], vc_ref[...]], axis=0)
    kv = {True: (kb, vb), False: (pltpu.roll(kb, HEAD_DIM, axis=1), pltpu.roll(vb, HEAD_DIM, axis=1))}
    rows = PAIRS_PER_KV * BLOCK
    r = lax.broadcasted_iota(jnp.int32, (rows, 2 * BLOCK), 0) % BLOCK
    c = lax.broadcasted_iota(jnp.int32, (rows, 2 * BLOCK), 1)
    valid = (c > r) & (c <= r + BLOCK) & ((n - 1) * BLOCK + c >= PAD)
    combos = [(g, a) for g in range(SWA_KV_HEADS) for a in range(2)]
    qst, ksel, vsel, scores = {}, {}, {}, {}
    for g, a in combos:
        qst[g, a] = jnp.concatenate(
            [jnp.where(halves[a], qs[g * PAIRS_PER_KV + j], 0.0) for j in range(PAIRS_PER_KV)], axis=0).astype(BF16)
        ksel[g, a], vsel[g, a] = kv[g == a]
    for g, a in combos:
        scores[g, a] = _dot_nt(qst[g, a], ksel[g, a].astype(BF16))
    out = {}
    for g, a in combos:
        s = jnp.where(valid, scores[g, a], -1e30)
        sink = jnp.concatenate([jnp.full((BLOCK, 1), sink_ref[0, 2 * (g * PAIRS_PER_KV + j) + a], F32)
                                for j in range(PAIRS_PER_KV)], axis=0)
        mx = jnp.maximum(jnp.max(s, axis=1, keepdims=True), sink)
        pe = jnp.exp(s - mx)
        es = jnp.exp(sink - mx)
        inv = 1.0 / (jnp.sum(pe, axis=1, keepdims=True) + es)
        out[g, a] = (qst[g, a], ksel[g, a], vsel[g, a], pe * inv, es * inv, halves[a])
    return combos, out


def swa_attention_fwd(proj, cos, sin, sinks):
    L = proj.shape[0]

    def body(q_ref, kp_ref, kc_ref, vp_ref, vc_ref, cp_ref, cc_ref, sp_ref, sc_ref, sink_ref, o_ref):
        n = pl.program_id(0)
        combos, parts = _swa_probs(n, q_ref, kp_ref, kc_ref, vp_ref, vc_ref, cp_ref, cc_ref, sp_ref, sc_ref, sink_ref)
        outs = {}
        for g, a in combos:
            qst, ksel, vsel, probs, psink, half = parts[g, a]
            outs[g, a] = _dot(probs.astype(BF16), jnp.where(half, vsel, 0.0).astype(BF16))
        for g in range(SWA_KV_HEADS):
            both = outs[g, 0] + outs[g, 1]
            for j in range(PAIRS_PER_KV):
                p = g * PAIRS_PER_KV + j
                o_ref[:, p * BLOCK:(p + 1) * BLOCK] = both[j * BLOCK:(j + 1) * BLOCK]

    return pl.pallas_call(
        body,
        name="swa_attention_fwd",
        grid=(L // BLOCK,),
        in_specs=_swa_specs(),
        out_specs=pl.BlockSpec((BLOCK, SWA_WIDTH), lambda n: (n, 0)),
        out_shape=jax.ShapeDtypeStruct((L, SWA_WIDTH), F32),
        compiler_params=_params(("arbitrary",)),
    )(proj, proj, proj, proj, proj, cos, cos, sin, sin, sinks)


def swa_attention_bwd(proj, cos, sin, sinks, do_sw):
    L = proj.shape[0]

    def body(q_ref, kp_ref, kc_ref, vp_ref, vc_ref, cp_ref, cc_ref, sp_ref, sc_ref, sink_ref, do_ref,
             dq_ref, dk_ref, dv_ref, ds_ref):
        n = pl.program_id(0)

        @pl.when(n == 0)
        def _():
            dk_ref[...] = jnp.zeros_like(dk_ref)
            dv_ref[...] = jnp.zeros_like(dv_ref)
            ds_ref[...] = jnp.zeros_like(ds_ref)

        combos, parts = _swa_probs(n, q_ref, kp_ref, kc_ref, vp_ref, vc_ref, cp_ref, cc_ref, sp_ref, sc_ref, sink_ref)
        lane8 = lax.broadcasted_iota(jnp.int32, (8, BLOCK), 1)
        dos, dps = {}, {}
        for g, a in combos:
            half = parts[g, a][5]
            dos[g, a] = jnp.concatenate(
                [jnp.where(half, do_ref[:, (g * PAIRS_PER_KV + j) * BLOCK:(g * PAIRS_PER_KV + j + 1) * BLOCK], 0.0)
                 for j in range(PAIRS_PER_KV)], axis=0).astype(BF16)
        for g, a in combos:
            dps[g, a] = _dot_nt(dos[g, a], parts[g, a][2].astype(BF16))
        dqs = {}
        dkb = jnp.zeros((2 * BLOCK, BLOCK), F32)
        dvb = jnp.zeros((2 * BLOCK, BLOCK), F32)
        dsk = jnp.zeros((8, BLOCK), F32)
        for g, a in combos:
            qst, ksel, vsel, probs, psink, half = parts[g, a]
            dp = dps[g, a]
            delta = jnp.sum(probs * dp, axis=1, keepdims=True)
            ds = (probs * (dp - delta)).astype(BF16)
            pd = psink * delta
            for j in range(PAIRS_PER_KV):
                head = 2 * (g * PAIRS_PER_KV + j) + a
                dsk = dsk + jnp.where(lane8 == head, -jnp.sum(pd[j * BLOCK:(j + 1) * BLOCK]), 0.0)
            dqs[g, a] = _dot(ds, jnp.where(half, ksel, 0.0).astype(BF16))
            dk_a = _dot_tn(ds, qst)
            dv_a = _dot_tn(probs.astype(BF16), dos[g, a])
            if g != a:
                dk_a = pltpu.roll(dk_a, HEAD_DIM, axis=1)
                dv_a = pltpu.roll(dv_a, HEAD_DIM, axis=1)
            dkb = dkb + dk_a
            dvb = dvb + dv_a
        cosc, sinc = cc_ref[...], sc_ref[...]
        for g in range(SWA_KV_HEADS):
            both = (dqs[g, 0] + dqs[g, 1]) * SCALE
            for j in range(PAIRS_PER_KV):
                p = g * PAIRS_PER_KV + j
                dq_ref[:, p * BLOCK:(p + 1) * BLOCK] = _unrope(both[j * BLOCK:(j + 1) * BLOCK], cosc, sinc).astype(BF16)
        ds_ref[...] += dsk
        cur = pl.ds(pl.multiple_of(n * BLOCK, BLOCK), BLOCK)
        dk_ref[cur, :] += _unrope(dkb[BLOCK:], cosc, sinc)
        dv_ref[cur, :] += dvb[BLOCK:]

        @pl.when(n > 0)
        def _():
            prv = pl.ds(pl.multiple_of((n - 1) * BLOCK, BLOCK), BLOCK)
            dk_ref[prv, :] += _unrope(dkb[:BLOCK], cp_ref[...], sp_ref[...])
            dv_ref[prv, :] += dvb[:BLOCK]

    whole = lambda n: (0, 0)
    row = pl.BlockSpec((BLOCK, SWA_WIDTH), lambda n: (n, 0))
    return pl.pallas_call(
        body,
        name="swa_attention_bwd",
        grid=(L // BLOCK,),
        in_specs=_swa_specs() + [row],
        out_specs=[row, pl.BlockSpec((L, BLOCK), whole), pl.BlockSpec((L, BLOCK), whole),
                   pl.BlockSpec((8, BLOCK), whole)],
        out_shape=[jax.ShapeDtypeStruct((L, SWA_WIDTH), BF16), jax.ShapeDtypeStruct((L, BLOCK), F32),
                   jax.ShapeDtypeStruct((L, BLOCK), F32), jax.ShapeDtypeStruct((8, BLOCK), F32)],
        compiler_params=_params(("arbitrary",)),
    )(proj, proj, proj, proj, proj, cos, cos, sin, sin, sinks, do_sw)


ROW_TILE = 640
TAIL_ROWS = 208


def _pick(n, cands):
    for c in cands:
        if n % c == 0:
            return c
    raise ValueError(f"no tile for {n}")


def in_proj(h0, gain, w, name, out_dtype):
    L, D = h0.shape
    N = w.shape[1]
    tm = _pick(L, (ROW_TILE, BLOCK))
    tn = _pick(N, (1792, 1536, 1280, 896, 640, 512, 384, 256, 128))

    def body(h_ref, g_ref, w_ref, o_ref, xn_ref):
        @pl.when(pl.program_id(1) == 0)
        def _():
            x = h_ref[...]
            r = lax.rsqrt(jnp.mean(x * x, axis=1, keepdims=True) + RMS_EPS)
            xn_ref[...] = ((x * r) * g_ref[...]).astype(BF16)
        o_ref[...] = _dot(xn_ref[...], w_ref[...]).astype(out_dtype)

    return pl.pallas_call(
        body,
        name=name,
        grid=(L // tm, N // tn),
        in_specs=[pl.BlockSpec((tm, D), lambda i, j: (i, 0)),
                  pl.BlockSpec((1, D), lambda i, j: (0, 0)),
                  pl.BlockSpec((D, tn), lambda i, j: (0, j))],
        out_specs=[pl.BlockSpec((tm, tn), lambda i, j: (i, j)),
                   pl.BlockSpec((tm, D), lambda i, j: (i, 0))],
        out_shape=[jax.ShapeDtypeStruct((L, N), out_dtype), jax.ShapeDtypeStruct((L, D), BF16)],
        compiler_params=_params(("arbitrary", "arbitrary")),
    )(h0, gain, w)


def matmul_tn(a, b, name):
    Kd, M = a.shape
    N = b.shape[1]
    tk = _pick(Kd, (ROW_TILE, BLOCK))
    tn = _pick(N, (1280, 1024, 896, 640, 512, 256, 128))
    nk = Kd // tk

    def body(a_ref, b_ref, o_ref):
        k = pl.program_id(1)

        @pl.when(k == 0)
        def _():
            o_ref[...] = jnp.zeros_like(o_ref)
        o_ref[...] += _dot_tn(a_ref[...], b_ref[...])

    return pl.pallas_call(
        body,
        name=name,
        grid=(N // tn, nk),
        in_specs=[pl.BlockSpec((tk, M), lambda j, k: (k, 0)),
                  pl.BlockSpec((tk, tn), lambda j, k: (k, j))],
        out_specs=pl.BlockSpec((M, tn), lambda j, k: (0, j)),
        out_shape=jax.ShapeDtypeStruct((M, N), F32),
        compiler_params=_params(("arbitrary", "arbitrary")),
    )(a, b)


def in_proj_bwd(dproj, w, h0, gain, dh1):
    L, N = dproj.shape
    D = w.shape[0]
    tm = _pick(L, (ROW_TILE, BLOCK))
    tk = _pick(N, (1280, 640, 512, 256, 128))
    nk = N // tk

    def body(dp_ref, w_ref, h_ref, g_ref, dh1_ref, dh0_ref, dg_ref, acc_ref):
        i, k = pl.program_id(0), pl.program_id(1)

        @pl.when(k == 0)
        def _():
            acc_ref[...] = jnp.zeros_like(acc_ref)

        @pl.when((i == 0) & (k == 0))
        def _():
            dg_ref[...] = jnp.zeros_like(dg_ref)

        acc_ref[...] += _dot_nt(dp_ref[...], w_ref[...])

        @pl.when(k == nk - 1)
        def _():
            x = h_ref[...]
            r = lax.rsqrt(jnp.mean(x * x, axis=1, keepdims=True) + RMS_EPS)
            xhat = x * r
            dxn = acc_ref[...]
            dg_ref[...] += jnp.sum(dxn * xhat, axis=0, keepdims=True)
            dxh = dxn * g_ref[...]
            dh0_ref[...] = r * (dxh - xhat * jnp.mean(dxh * xhat, axis=1, keepdims=True)) + dh1_ref[...]

    row = pl.BlockSpec((tm, D), lambda i, k: (i, 0))
    vec = pl.BlockSpec((1, D), lambda i, k: (0, 0))
    return pl.pallas_call(
        body,
        name="in_proj_bwd",
        grid=(L // tm, nk),
        in_specs=[pl.BlockSpec((tm, tk), lambda i, k: (i, k)),
                  pl.BlockSpec((D, tk), lambda i, k: (0, k)),
                  row, vec, row],
        out_specs=[row, vec],
        out_shape=[jax.ShapeDtypeStruct((L, D), F32), jax.ShapeDtypeStruct((1, D), F32)],
        scratch_shapes=[pltpu.VMEM((tm, D), F32)],
        compiler_params=_params(("arbitrary", "arbitrary")),
    )(dproj, w, h0, gain, dh1)


def tail_fwd_bwd(h0, tgt, o_sb, o_sw, gates, w_bsb, w_bswa, w_out, gain_f):
    L, D = h0.shape
    R = _pick(L, (TAIL_ROWS, BLOCK))
    z0, z1, z2, z3 = 0, SB_WIDTH, SB_WIDTH + SWA_WIDTH, SB_WIDTH + SWA_WIDTH + D_MODEL

    def body(h_ref, t_ref, osb_ref, osw_ref, g_ref, wsb_ref, wsw_ref, wo_ref, gf_ref,
             dosb_ref, dosw_ref, dg_ref, dh1_ref, mb_ref, usb_ref, usw_ref, dh1b_ref, dysb_ref, dysw_ref,
             dgf_ref, loss_ref):
        i = pl.program_id(0)

        @pl.when(i == 0)
        def _():
            dgf_ref[...] = jnp.zeros_like(dgf_ref)
            loss_ref[...] = jnp.zeros_like(loss_ref)

        sbz = g_ref[:, z0:z1]
        swz = g_ref[:, z1:z2]
        s1 = jax.nn.sigmoid(g_ref[:, z2:z3])
        s2 = jax.nn.sigmoid(g_ref[:, z3:])
        sg_sb = jax.nn.sigmoid(sbz)
        sg_sw = jax.nn.sigmoid(swz)
        silu_sb = sbz * sg_sb
        silu_sw = swz * sg_sw
        osb = osb_ref[...]
        osw = osw_ref[...]
        usb = (osb * silu_sb).astype(BF16)
        usw = (osw * silu_sw).astype(BF16)
        y_sb = _dot(usb, wsb_ref[...])
        y_sw = _dot(usw, wsw_ref[...])
        mb = (s1 * y_sb + s2 * y_sw).astype(BF16)
        h1 = h_ref[...] + _dot(mb, wo_ref[...])
        rf = lax.rsqrt(jnp.mean(h1 * h1, axis=1, keepdims=True) + RMS_EPS)
        hhat = h1 * rf
        gf = gf_ref[...]
        row = i * R + lax.broadcasted_iota(jnp.int32, (R, 1), 0)
        err = jnp.where(row >= BLOCK, hhat * gf - t_ref[...], 0.0)
        lane0 = (lax.broadcasted_iota(jnp.int32, (8, BLOCK), 0) == 0) & (lax.broadcasted_iota(jnp.int32, (8, BLOCK), 1) == 0)
        loss_ref[...] += jnp.where(lane0, (0.5 / D) * jnp.sum(err * err), 0.0)
        dy = err * (1.0 / D)
        dgf_ref[...] += jnp.sum(dy * hhat, axis=0, keepdims=True)
        dhh = dy * gf
        dh1 = rf * (dhh - hhat * jnp.mean(dhh * hhat, axis=1, keepdims=True))
        dh1b = dh1.astype(BF16)
        dm = _dot_nt(dh1b, wo_ref[...])
        dysb = (dm * s1).astype(BF16)
        dysw = (dm * s2).astype(BF16)
        dusb = _dot_nt(dysb, wsb_ref[...])
        dusw = _dot_nt(dysw, wsw_ref[...])
        dosb_ref[...] = (dusb * silu_sb).astype(BF16)
        dosw_ref[...] = (dusw * silu_sw).astype(BF16)
        dg_ref[:, z0:z1] = (dusb * osb * (sg_sb * (1.0 + sbz * (1.0 - sg_sb)))).astype(BF16)
        dg_ref[:, z1:z2] = (dusw * osw * (sg_sw * (1.0 + swz * (1.0 - sg_sw)))).astype(BF16)
        dg_ref[:, z2:z3] = (dm * y_sb * (s1 * (1.0 - s1))).astype(BF16)
        dg_ref[:, z3:] = (dm * y_sw * (s2 * (1.0 - s2))).astype(BF16)
        dh1_ref[...] = dh1
        mb_ref[...] = mb
        usb_ref[...] = usb
        usw_ref[...] = usw
        dh1b_ref[...] = dh1b
        dysb_ref[...] = dysb
        dysw_ref[...] = dysw

    def rows(n):
        return pl.BlockSpec((R, n), lambda i: (i, 0))

    def whole(shape):
        return pl.BlockSpec(shape, lambda i: (0, 0))

    GW = gates.shape[1]
    return pl.pallas_call(
        body,
        name="tail_fwd_bwd",
        grid=(L // R,),
        in_specs=[rows(D), rows(D), rows(SB_WIDTH), rows(SWA_WIDTH), rows(GW),
                  whole(w_bsb.shape), whole(w_bswa.shape), whole(w_out.shape), whole((1, D))],
        out_specs=[rows(SB_WIDTH), rows(SWA_WIDTH), rows(GW), rows(D),
                   rows(D), rows(SB_WIDTH), rows(SWA_WIDTH), rows(D), rows(D), rows(D),
                   whole((1, D)), whole((8, BLOCK))],
        out_shape=[jax.ShapeDtypeStruct((L, SB_WIDTH), BF16), jax.ShapeDtypeStruct((L, SWA_WIDTH), BF16),
                   jax.ShapeDtypeStruct((L, GW), BF16), jax.ShapeDtypeStruct((L, D), F32),
                   jax.ShapeDtypeStruct((L, D), BF16), jax.ShapeDtypeStruct((L, SB_WIDTH), BF16),
                   jax.ShapeDtypeStruct((L, SWA_WIDTH), BF16), jax.ShapeDtypeStruct((L, D), BF16),
                   jax.ShapeDtypeStruct((L, D), BF16), jax.ShapeDtypeStruct((L, D), BF16),
                   jax.ShapeDtypeStruct((1, D), F32), jax.ShapeDtypeStruct((8, BLOCK), F32)],
        compiler_params=_params(("arbitrary",)),
    )(h0, tgt, o_sb, o_sw, gates, w_bsb, w_bswa, w_out, gain_f)


def local_step(x, tgt, meta, gain, w_in, w_bsb, w_bswa, w_out, sinks, gain_f):
    S, D = x.shape
    L = S + BLOCK
    h0 = jnp.concatenate([jnp.zeros((PAD, D), F32), meta, x], axis=0)
    tgt_p = jnp.concatenate([jnp.zeros((BLOCK, D), F32), tgt], axis=0)
    tt = _suffix_matrix()
    cos, sin = rope_tables(L)
    qkv, xn = in_proj(h0, gain, w_in[:, :SB_COLS], "in_proj_sb", BF16)
    proj_sw, _ = in_proj(h0, gain, w_in[:, SB_COLS:GATE_COL0], "in_proj_swa", F32)
    gates, _ = in_proj(h0, gain, w_in[:, GATE_COL0:], "in_proj_gates", F32)
    o_sb = sb_attention_fwd(qkv, tt)
    o_sw = swa_attention_fwd(proj_sw, cos, sin, sinks)
    (do_sb, do_sw, dgates, dh1, mb, usb, usw, dh1b, dysb, dysw, dgf, loss) = tail_fwd_bwd(
        h0, tgt_p, o_sb, o_sw, gates, w_bsb, w_bswa, w_out, gain_f)
    dq_sb, dk_sb, dv_sb = sb_attention_bwd(qkv, o_sb, do_sb, tt)
    dq_sw, dk_sw, dv_sw, dsinks = swa_attention_bwd(proj_sw, cos, sin, sinks, do_sw)
    dproj = jnp.concatenate([dq_sb, dk_sb, dv_sb, dq_sw, dk_sw.astype(BF16), dv_sw.astype(BF16), dgates], axis=1)
    dw_in = matmul_tn(xn, dproj, "dw_in")
    dw_out = matmul_tn(mb, dh1b, "dw_out")
    dw_bsb = matmul_tn(usb, dysb, "dw_bsb")
    dw_bswa = matmul_tn(usw, dysw, "dw_bswa")
    dh0, dgain = in_proj_bwd(dproj, w_in, h0, gain, dh1)
    return (loss[0, 0], dh0[BLOCK:], dh0[PAD:BLOCK], dgain, dw_in, dw_bsb, dw_bswa, dw_out,
            dsinks[:1, :SWA_Q_HEADS], dgf)


MESH_IDS = pl.DeviceIdType.MESH
ANY = pl.BlockSpec(memory_space=pl.ANY)


def _place():
    return lax.axis_index("x"), lax.axis_index("y"), lax.axis_index("c")


def _index(x, y, c):
    return 4 * x + 2 * y + c


def all_gather(block, name):
    def body(x_ref, out_ref, send_sems, recv_sems, local_sem):
        x, y, c = _place()
        me, sibling = (x, y, c), (x, y, 1 - c)
        chips = [(1 - x, y), (x, 1 - y), (1 - x, 1 - y)]

        def copy(k, blk, to, src=None):
            dst = out_ref.at[_index(*blk)]
            return pltpu.make_async_remote_copy(
                src_ref=dst if src is None else src, dst_ref=dst,
                send_sem=send_sems.at[k], recv_sem=recv_sems.at[k], device_id=to, device_id_type=MESH_IDS)

        mine = pltpu.make_async_copy(x_ref, out_ref.at[_index(*me)], local_sem)
        mine.start()
        first = [copy(0, me, sibling, src=x_ref)]
        first += [copy(1 + j, me, (*chip, c), src=x_ref) for j, chip in enumerate(chips)]
        for cp in first:
            cp.start()
        passed = [copy(4 + j, (*chip, c), sibling) for j, chip in enumerate(chips)]
        for j, chip in enumerate(chips):
            copy(1 + j, (*chip, c), me).wait_recv()
            passed[j].start()
        copy(0, sibling, me).wait_recv()
        for j, chip in enumerate(chips):
            copy(4 + j, (*chip, 1 - c), me).wait_recv()
        for cp in first + passed:
            cp.wait_send()
        mine.wait()

    return pl.pallas_call(
        body,
        name=name,
        out_shape=jax.ShapeDtypeStruct((N_DEV,) + block.shape, block.dtype),
        in_specs=[ANY],
        out_specs=ANY,
        scratch_shapes=[pltpu.SemaphoreType.DMA((7,)), pltpu.SemaphoreType.DMA((7,)), pltpu.SemaphoreType.DMA],
    )(block)


def exchange_partials(parts):
    def body(g_ref, out_ref, send_sems, recv_sems, local_sem):
        x, y, c = _place()
        me = _index(x, y, c)
        mine = pltpu.make_async_copy(g_ref.at[me], out_ref.at[me], local_sem)
        mine.start()
        copies = []
        for m in range(1, N_DEV):
            px = 1 - x if m & 4 else x
            py = 1 - y if m & 2 else y
            pc = 1 - c if m & 1 else c
            cp = pltpu.make_async_remote_copy(
                src_ref=g_ref.at[_index(px, py, pc)], dst_ref=out_ref.at[me],
                send_sem=send_sems.at[m - 1], recv_sem=recv_sems.at[m - 1],
                device_id=(px, py, pc), device_id_type=MESH_IDS)
            cp.start()
            copies.append(cp)
        for cp in copies:
            cp.wait()
        mine.wait()

    return pl.pallas_call(
        body,
        name="exchange_partials",
        out_shape=jax.ShapeDtypeStruct(parts.shape, parts.dtype),
        in_specs=[ANY],
        out_specs=ANY,
        scratch_shapes=[pltpu.SemaphoreType.DMA((7,)), pltpu.SemaphoreType.DMA((7,)), pltpu.SemaphoreType.DMA],
    )(parts)


def _adamw(w, g, m, v):
    m = ADAM_B1 * m + (1.0 - ADAM_B1) * g
    v = ADAM_B2 * v + (1.0 - ADAM_B2) * (g * g)
    m_hat = m / (1.0 - ADAM_B1 ** ADAM_STEP)
    v_hat = v / (1.0 - ADAM_B2 ** ADAM_STEP)
    delta = -ADAM_LR * (m_hat / (jnp.sqrt(v_hat) + ADAM_EPS) + ADAM_WD * w)
    return delta, m, v


def sum_and_adamw(parts, w, m, v, name):
    _, R, C = parts.shape
    tr = _pick(R, (528, 512, 256, 128, 24, 8))

    def body(p_ref, w_ref, m_ref, v_ref, g_ref, d_ref, nm_ref, nv_ref):
        g = p_ref[0].astype(F32)
        for s in range(1, N_DEV):
            g = g + p_ref[s].astype(F32)
        d, nm, nv = _adamw(w_ref[...], g, m_ref[...], v_ref[...])
        g_ref[...] = g
        d_ref[...] = d
        nm_ref[...] = nm
        nv_ref[...] = nv

    row = pl.BlockSpec((tr, C), lambda i: (i, 0))
    return pl.pallas_call(
        body,
        name=name,
        grid=(R // tr,),
        in_specs=[pl.BlockSpec((N_DEV, tr, C), lambda i: (0, i, 0)), row, row, row],
        out_specs=[row, row, row, row],
        out_shape=[jax.ShapeDtypeStruct((R, C), F32)] * 4,
        compiler_params=_params(("arbitrary",)),
    )(parts, w, m, v)


W_IN_SHARD = IN_COLS // N_DEV
ROWS_W_IN = D_MODEL * W_IN_SHARD // BLOCK
ROWS_W_BSB = SB_WIDTH
ROWS_W_ROWSHARD = D_MODEL
SMALL_ROWS = 24


def _pack_shards(w_in, w_bsb, w_bswa, w_out, meta):
    return jnp.concatenate([w_in.reshape(ROWS_W_IN, BLOCK), w_bsb.reshape(ROWS_W_BSB, BLOCK),
                            w_bswa.reshape(ROWS_W_ROWSHARD, BLOCK), w_out.reshape(ROWS_W_ROWSHARD, BLOCK),
                            meta.reshape(N_META, BLOCK)], axis=0)


def _unpack_shards(p):
    o = np.cumsum([0, ROWS_W_IN, ROWS_W_BSB, ROWS_W_ROWSHARD, ROWS_W_ROWSHARD, N_META])
    return (p[o[0]:o[1]].reshape(1, D_MODEL, W_IN_SHARD), p[o[1]:o[2]].reshape(1, SB_WIDTH, BLOCK),
            p[o[2]:o[3]].reshape(1, BLOCK, D_MODEL), p[o[3]:o[4]].reshape(1, BLOCK, D_MODEL),
            p[o[4]:o[5]].reshape(N_META, BLOCK))


def _pack_by_owner(dw_in, dw_bsb, dw_bswa, dw_out, dmeta):
    cols = lambda a: a.reshape(a.shape[0], N_DEV, -1).transpose(1, 0, 2)
    return jnp.concatenate([cols(dw_in).reshape(N_DEV, ROWS_W_IN, BLOCK), cols(dw_bsb),
                            dw_bswa.reshape(N_DEV, ROWS_W_ROWSHARD, BLOCK), dw_out.reshape(N_DEV, ROWS_W_ROWSHARD, BLOCK),
                            cols(dmeta)], axis=1)


def _pack_small(gain, gain_f, sinks, loss):
    z = jnp.zeros((SMALL_ROWS - 16, BLOCK), F32)
    z = z.at[0, :SWA_Q_HEADS].set(sinks.reshape(-1)).at[1, 0].set(loss)
    return jnp.concatenate([gain.reshape(8, BLOCK), gain_f.reshape(8, BLOCK), z], axis=0)


def _unpack_small(p):
    return p[0:8].reshape(1, D_MODEL), p[8:16].reshape(D_MODEL), p[16:17, :SWA_Q_HEADS], p[17, 0]


def kernel(x, meta_tokens, norm_gain, w_in, w_branch_sb, w_branch_swa, w_out, attn_sinks, final_norm_gain, loss_target, m_meta_tokens, m_norm_gain, m_w_in, m_w_branch_sb, m_w_branch_swa, m_w_out, m_attn_sinks, m_final_norm_gain, v_meta_tokens, v_norm_gain, v_w_in, v_w_branch_sb, v_w_branch_swa, v_w_out, v_attn_sinks, v_final_norm_gain):
    meta_bits = lax.bitcast_convert_type(meta_tokens, BF16).reshape(2 * N_META, BLOCK)
    mine = jnp.concatenate([_pack_shards(w_in, w_branch_sb, w_branch_swa, w_out, meta_tokens)[:-N_META].astype(BF16),
                            meta_bits], axis=0)
    full = all_gather(mine, "all_gather_weights")
    o = np.cumsum([0, ROWS_W_IN, ROWS_W_BSB, ROWS_W_ROWSHARD, ROWS_W_ROWSHARD, 2 * N_META])
    cols = lambda a: a.transpose(1, 0, 2).reshape(a.shape[1], -1)
    f_w_in = cols(full[:, o[0]:o[1]].reshape(N_DEV, D_MODEL, W_IN_SHARD))
    f_w_bsb = cols(full[:, o[1]:o[2]])
    f_w_bswa = full[:, o[2]:o[3]].reshape(D_MODEL, D_MODEL)
    f_w_out = full[:, o[3]:o[4]].reshape(D_MODEL, D_MODEL)
    f_meta = cols(lax.bitcast_convert_type(full[:, o[4]:o[5]].reshape(N_DEV, N_META, BLOCK, 2), F32))

    (loss, grad_x, dmeta, dgain, dw_in, dw_bsb, dw_bswa, dw_out, dsinks, dgf) = local_step(
        x[0], loss_target[0], f_meta, norm_gain, f_w_in, f_w_bsb, f_w_bswa, f_w_out, attn_sinks,
        final_norm_gain.reshape(1, D_MODEL))

    parts = exchange_partials(_pack_by_owner(dw_in, dw_bsb, dw_bswa, dw_out, dmeta).astype(BF16))
    packs = [_pack_shards(a[0], b[0], c[0], d[0], e) for a, b, c, d, e in (
        (w_in, w_branch_sb, w_branch_swa, w_out, meta_tokens),
        (m_w_in, m_w_branch_sb, m_w_branch_swa, m_w_out, m_meta_tokens),
        (v_w_in, v_w_branch_sb, v_w_branch_swa, v_w_out, v_meta_tokens))]
    big = [_unpack_shards(p) for p in sum_and_adamw(parts, *packs, "sum_adamw_sharded")]

    small = all_gather(_pack_small(dgain, dgf, dsinks, loss), "all_gather_small")
    zero = jnp.zeros((), F32)
    spacks = [_pack_small(a, b, c, zero) for a, b, c in (
        (norm_gain, final_norm_gain, attn_sinks), (m_norm_gain, m_final_norm_gain, m_attn_sinks),
        (v_norm_gain, v_final_norm_gain, v_attn_sinks))]
    sm = [_unpack_small(p) for p in sum_and_adamw(small, *spacks, "sum_adamw_replicated")]

    def leaves(k):
        b, s = big[k], sm[k]
        return (b[4], s[0], b[0], b[1], b[2], b[3], s[2], s[1])

    return (sm[0][3], grad_x[None], *leaves(0), *leaves(1), *leaves(2), *leaves(3))
```

```python
import numpy as np
import jax
import jax.numpy as jnp
from jax import lax
from jax.experimental import pallas as pl
from jax.experimental.pallas import tpu as pltpu

F32 = jnp.float32
BF16 = jnp.bfloat16

D_MODEL = 1024
N_META = 16
BLOCK = 128
PAD = BLOCK - N_META
HEAD_DIM = 64
SB_HEADS = 8
SB_WIDTH = SB_HEADS * HEAD_DIM
SWA_Q_HEADS = 16
SWA_KV_HEADS = 2
SWA_WIDTH = SWA_Q_HEADS * HEAD_DIM
SWA_KV_WIDTH = SWA_KV_HEADS * HEAD_DIM
ROPE_THETA = 10000.0
RMS_EPS = 1e-6
SCALE = HEAD_DIM ** -0.5
SPLITS = (SB_WIDTH, SB_WIDTH, SB_WIDTH, SWA_WIDTH, SWA_KV_WIDTH, SWA_KV_WIDTH,
          SB_WIDTH, SWA_WIDTH, D_MODEL, D_MODEL)
IN_COLS = sum(SPLITS)
SB_COLS = 3 * SB_WIDTH
SWA_COLS = SWA_WIDTH + 2 * SWA_KV_WIDTH
GATE_COL0 = SB_COLS + SWA_COLS

N_DEV = 8
ADAM_LR = 0.001
ADAM_B1 = 0.9
ADAM_B2 = 0.999
ADAM_EPS = 1e-08
ADAM_WD = 0.01
ADAM_STEP = 10

VMEM_LIMIT = 56 * 1024 * 1024


def _params(sem, **kw):
    return pltpu.CompilerParams(dimension_semantics=sem, vmem_limit_bytes=VMEM_LIMIT, **kw)


def _dot(a, b):
    return jnp.dot(a, b, preferred_element_type=F32)


def _dot_nt(a, b):
    return lax.dot_general(a, b, (((1,), (1,)), ((), ())), preferred_element_type=F32)


def _dot_tn(a, b):
    return lax.dot_general(a, b, (((0,), (0,)), ((), ())), preferred_element_type=F32)


def _cat(xs, axis):
    return xs[0] if len(xs) == 1 else jnp.concatenate(xs, axis=axis)


def _split_bf16(x):
    hi = x.astype(BF16)
    lo = (x - hi.astype(F32)).astype(BF16)
    return jnp.concatenate([hi, lo], axis=1)


def _suffix_matrix():
    j = np.arange(BLOCK)[:, None]
    s = np.arange(BLOCK)[None, :]
    t = np.concatenate([(j >= s).astype(np.float32), np.ones((BLOCK, BLOCK), np.float32)], axis=1)
    return jnp.asarray(np.concatenate([t, t], axis=0), dtype=BF16)


def _softplus(z):
    return jnp.maximum(z, 0.0) + jnp.log(1.0 + jnp.exp(-jnp.abs(z)))


SB_SMALL = 4
SB_PAIRS_FWD = 2
SB_PAIRS_BWD = 1
SB_BIG_FWD = 8
SB_BIG_BWD = 8
SB_CHUNK = 2


def _sb_masks(i, nblk):
    if nblk == 0:
        r = lax.broadcasted_iota(jnp.int32, (BLOCK, BLOCK), 0)
        c = lax.broadcasted_iota(jnp.int32, (BLOCK, BLOCK), 1)
        return (c < r) & (i * BLOCK + c >= PAD)
    return lax.broadcasted_iota(jnp.int32, (BLOCK, nblk * BLOCK), 1) >= PAD


def _sb_heads(npairs):
    return [(pr, a) for pr in range(npairs) for a in range(2)]


def _lanes(pr):
    return slice(pr * BLOCK, (pr + 1) * BLOCK)


def _masked_heads(x, half0, npairs):
    out = []
    for pr, a in _sb_heads(npairs):
        xp = x[:, _lanes(pr)]
        out.append((jnp.where(half0, xp, 0.0) if a == 0 else jnp.where(half0, 0.0, xp)).astype(BF16))
    return out


def _sb_sweep(i, tile, carry, big):
    same = lambda cr: cr
    small = SB_SMALL
    carry = tile(i, 1, _sb_masks(i, 0), carry)
    n_big = i // big
    rest = i - n_big * big
    n_small = rest // small
    n_one = rest - n_small * small
    low_is_big = n_big > 0
    low_is_small = jnp.logical_not(low_is_big) & (n_small > 0)
    low_is_one = jnp.logical_not(low_is_big) & (n_small == 0) & (n_one > 0)

    n_plain = n_one - jnp.where(low_is_one, 1, 0)
    carry = lax.fori_loop(0, n_plain, lambda t, cr: tile(i - 1 - t, 1, None, cr), carry)
    carry = lax.cond(low_is_one, lambda cr: tile(0, 1, _sb_masks(i, 1), cr), same, carry)
    if big != small:
        base = n_big * big
        n_plain = n_small - jnp.where(low_is_small, 1, 0)
        carry = lax.fori_loop(0, n_plain, lambda t, cr: tile(base + (n_small - 1 - t) * small, small, None, cr), carry)
        carry = lax.cond(low_is_small, lambda cr: tile(0, small, _sb_masks(i, small), cr), same, carry)
    carry = lax.fori_loop(0, jnp.maximum(n_big - 1, 0), lambda t, cr: tile((n_big - 1 - t) * big, big, None, cr), carry)
    return lax.cond(low_is_big, lambda cr: tile(0, big, _sb_masks(i, big), cr), same, carry)


def _sb_weights(zs, ss, cs, tt, nblk, mask):
    suf = _dot(_cat([_split_bf16(s[:, b * BLOCK:(b + 1) * BLOCK]) for s in ss for b in range(nblk)], 0), tt)
    ws, out_cs = [], []
    for h in range(len(zs)):
        c = cs[h]
        wb = [None] * nblk
        for b in reversed(range(nblk)):
            sab = suf[(h * nblk + b) * BLOCK:(h * nblk + b + 1) * BLOCK]
            wb[b] = jnp.exp(zs[h][:, b * BLOCK:(b + 1) * BLOCK] + c - sab[:, :BLOCK])
            c = c - sab[:, BLOCK:]
        w = _cat(wb, 1)
        if mask is not None:
            w = jnp.where(mask, w, 0.0)
        ws.append(w)
        out_cs.append(c)
    return ws, out_cs


def sb_attention_fwd(qkv, tt):
    L = qkv.shape[0]
    nb = L // BLOCK
    NP = SB_PAIRS_FWD
    SB_W = NP * BLOCK
    nq = SB_WIDTH // SB_W

    def body(q_ref, k_ref, v_ref, tt_ref, o_ref, w_ref):
        i = pl.program_id(1)
        half0 = lax.broadcasted_iota(jnp.int32, (1, BLOCK), 1) < HEAD_DIM
        qh = _masked_heads(q_ref[...].astype(F32) * SCALE, half0, NP)
        heads = _sb_heads(NP)

        def rows_of(j0, nblk):
            return pl.ds(pl.multiple_of(j0 * BLOCK, BLOCK), nblk * BLOCK)

        def tile(j0, nblk, mask, carry):
            accs, cs = carry
            rows = rows_of(j0, nblk)
            zs = [_dot_nt(qh[h], k_ref[rows, _lanes(pr)]) for h, (pr, a) in enumerate(heads)]
            ss = [_softplus(z) for z in zs]
            if mask is not None:
                ss = [jnp.where(mask, s, 0.0) for s in ss]
            ws, cs = _sb_weights(zs, ss, cs, tt_ref[...], nblk, mask)
            wbs = [w.astype(BF16) for w in ws]
            for h in range(len(heads)):
                for b in range(nblk):
                    w_ref[0, h, j0 + b] = wbs[h][:, b * BLOCK:(b + 1) * BLOCK]
            accs = [accs[h] + _dot(wbs[h], v_ref[rows, _lanes(pr)]) for h, (pr, a) in enumerate(heads)]
            return tuple(accs), tuple(cs)

        zero = (jnp.zeros((BLOCK, BLOCK), F32),) * len(heads)
        accs = _sb_sweep(i, tile, (zero, zero), SB_BIG_FWD)[0]
        o_ref[...] = _cat([jnp.where(half0, accs[2 * pr], accs[2 * pr + 1]) for pr in range(NP)], 1)

    panel = lambda c: pl.BlockSpec((L, SB_W), lambda p, i: (0, c * nq + p), pipeline_mode=pl.Buffered(1))
    return pl.pallas_call(
        body,
        name="sb_attention_fwd",
        grid=(nq, nb),
        in_specs=[
            pl.BlockSpec((BLOCK, SB_W), lambda p, i: (i, p)),
            panel(1), panel(2),
            pl.BlockSpec((2 * BLOCK, 2 * BLOCK), lambda p, i: (0, 0)),
        ],
        out_specs=[pl.BlockSpec((BLOCK, SB_W), lambda p, i: (i, p)),
                   pl.BlockSpec((1, 2 * NP, nb, BLOCK, BLOCK), lambda p, i: (i, p, 0, 0, 0))],
        out_shape=[jax.ShapeDtypeStruct((L, SB_WIDTH), F32),
                   jax.ShapeDtypeStruct((nb, SB_HEADS, nb, BLOCK, BLOCK), BF16)],
        compiler_params=_params(("arbitrary", "arbitrary")),
    )(qkv, qkv, qkv, tt)


def sb_attention_bwd(qkv, o_sb, w_sb, do_sb, tt):
    L = qkv.shape[0]
    nb = L // BLOCK
    NP = SB_PAIRS_BWD
    SB_W = NP * BLOCK
    nq = SB_WIDTH // SB_W

    def body(q_ref, k_ref, v_ref, o_ref, w_ref, do_ref, tt_ref, dq_ref, dk_ref, dv_ref, dk_acc, dv_acc):
        i = pl.program_id(1)
        half0 = lax.broadcasted_iota(jnp.int32, (1, BLOCK), 1) < HEAD_DIM
        heads = _sb_heads(NP)

        @pl.when(i == 0)
        def _():
            dk_acc[...] = jnp.zeros_like(dk_acc)
            dv_acc[...] = jnp.zeros_like(dv_acc)

        qh = _masked_heads(q_ref[...].astype(F32) * SCALE, half0, NP)
        do = do_ref[...]
        doh = _masked_heads(do, half0, NP)
        od = o_ref[...] * do.astype(F32)
        dsum = []
        for pr, a in heads:
            x = od[:, _lanes(pr)]
            x = jnp.where(half0, x, 0.0) if a == 0 else jnp.where(half0, 0.0, x)
            dsum.append(jnp.broadcast_to(jnp.sum(x, axis=1, keepdims=True), (BLOCK, BLOCK)))

        def rows_of(j0, nblk):
            return pl.ds(pl.multiple_of(j0 * BLOCK, BLOCK), nblk * BLOCK)

        def tile(j0, nblk, mask, carry):
            accs, ces = carry
            rows = rows_of(j0, nblk)
            sig_c, wb_c, e_c = [[] for _ in heads], [[] for _ in heads], [[] for _ in heads]
            for b0 in range(0, nblk, SB_CHUNK):
                nb_c = min(SB_CHUNK, nblk - b0)
                rows_c = rows_of(j0 + b0, nb_c)
                zs = [_dot_nt(qh[h], k_ref[rows_c, _lanes(pr)]) for h, (pr, a) in enumerate(heads)]
                dws = [_dot_nt(doh[h], v_ref[rows_c, _lanes(pr)]) for h, (pr, a) in enumerate(heads)]
                for h in range(len(heads)):
                    ez = jnp.exp(-jnp.abs(zs[h]))
                    sig_c[h].append(jnp.where(zs[h] >= 0.0, 1.0, ez) / (1.0 + ez))
                    wb = _cat([w_ref[0, h, j0 + b0 + b] for b in range(nb_c)], 1)
                    wb_c[h].append(wb)
                    e_c[h].append(wb.astype(F32) * dws[h])
            sigs = [_cat(x, 1) for x in sig_c]
            wbs = [_cat(x, 1) for x in wb_c]
            es = [_cat(x, 1) for x in e_c]
            esuf = _dot(_cat([_split_bf16(e[:, b * BLOCK:(b + 1) * BLOCK]) for e in es for b in range(nblk)], 0),
                        tt_ref[...])
            out_ces, dzbs = [], []
            for h, (pr, a) in enumerate(heads):
                ce = ces[h]
                dzs = [None] * nblk
                for b in reversed(range(nblk)):
                    eab = esuf[(h * nblk + b) * BLOCK:(h * nblk + b + 1) * BLOCK]
                    sl = slice(b * BLOCK, (b + 1) * BLOCK)
                    e = es[h][:, sl]
                    dzs[b] = e - sigs[h][:, sl] * (e + (ce - eab[:, :BLOCK]))
                    ce = ce - eab[:, BLOCK:]
                dz = _cat(dzs, 1)
                if mask is not None:
                    dz = jnp.where(mask, dz, 0.0)
                dzbs.append(dz.astype(BF16))
                out_ces.append(ce)
            accs = tuple(accs[h] + _dot(dzbs[h], k_ref[rows, _lanes(pr)]) for h, (pr, a) in enumerate(heads))
            dkv = [_dot_tn(_cat([dzbs[2 * pr], wbs[2 * pr], dzbs[2 * pr + 1], wbs[2 * pr + 1]], 0), qdo[pr])
                   for pr in range(NP)]
            dk_acc[rows, :] += _cat([x[:, :BLOCK] for x in dkv], 1)
            dv_acc[rows, :] += _cat([x[:, BLOCK:] for x in dkv], 1)
            return accs, tuple(out_ces)

        zb = jnp.zeros((BLOCK, BLOCK), BF16)
        qdo = [_cat([_cat([qh[h], zb], 1) if kind == 0 else _cat([zb, doh[h]], 1)
                     for h in (2 * pr, 2 * pr + 1) for kind in (0, 1)], 0) for pr in range(NP)]
        zero = (jnp.zeros((BLOCK, BLOCK), F32),) * len(heads)
        accs = _sb_sweep(i, tile, (zero, tuple(dsum)), SB_BIG_BWD)[0]
        dq_ref[...] = (_cat([jnp.where(half0, accs[2 * pr], accs[2 * pr + 1]) for pr in range(NP)], 1)
                       * SCALE).astype(BF16)

        @pl.when(i == nb - 1)
        def _():
            dk_ref[...] = dk_acc[...].astype(BF16)
            dv_ref[...] = dv_acc[...].astype(BF16)

    blk = pl.BlockSpec((BLOCK, SB_W), lambda p, i: (i, p))
    panel = pl.BlockSpec((L, SB_W), lambda p, i: (0, p))
    return pl.pallas_call(
        body,
        name="sb_attention_bwd",
        grid=(nq, nb),
        in_specs=[
            blk,
            pl.BlockSpec((L, SB_W), lambda p, i: (0, nq + p), pipeline_mode=pl.Buffered(1)),
            pl.BlockSpec((L, SB_W), lambda p, i: (0, 2 * nq + p), pipeline_mode=pl.Buffered(1)),
            blk,
            pl.BlockSpec((1, 2 * NP, nb, BLOCK, BLOCK), lambda p, i: (i, p, 0, 0, 0)),
            blk,
            pl.BlockSpec((2 * BLOCK, 2 * BLOCK), lambda p, i: (0, 0)),
        ],
        out_specs=[blk, panel, panel],
        out_shape=[jax.ShapeDtypeStruct((L, SB_WIDTH), BF16)] * 3,
        scratch_shapes=[pltpu.VMEM((L, SB_W), F32), pltpu.VMEM((L, SB_W), F32)],
        compiler_params=_params(("arbitrary", "arbitrary")),
    )(qkv, qkv, qkv, o_sb, w_sb, do_sb, tt)


SWA_PAIRS = SWA_WIDTH // BLOCK
PAIRS_PER_KV = SWA_PAIRS // SWA_KV_HEADS
CB_SWK = SWA_PAIRS
CB_SWV = SWA_PAIRS + 1


def rope_tables(L):
    half = HEAD_DIM // 2
    inv = ROPE_THETA ** (-jnp.arange(half, dtype=F32) / half)
    pos = (jnp.arange(L) - PAD).astype(F32)
    ang = pos[:, None] * inv[None, :]
    reps = BLOCK // half
    return jnp.tile(jnp.cos(ang), (1, reps)), jnp.tile(jnp.sin(ang), (1, reps))


def _rot_half(x):
    lane = lax.broadcasted_iota(jnp.int32, (1, BLOCK), 1)
    first = (lane % HEAD_DIM) < (HEAD_DIM // 2)
    return jnp.where(first, -pltpu.roll(x, BLOCK - HEAD_DIM // 2, axis=1), pltpu.roll(x, HEAD_DIM // 2, axis=1))


def _rope(x, cos, sin):
    return x * cos + _rot_half(x) * sin


def _unrope(x, cos, sin):
    return x * cos - _rot_half(x) * sin


def _swa_specs():
    prev = lambda n: jnp.maximum(n - 1, 0)
    cur = lambda n: n
    blk = lambda f, c: pl.BlockSpec((BLOCK, BLOCK), lambda n: (f(n), c))
    return [
        pl.BlockSpec((BLOCK, SWA_WIDTH), lambda n: (n, 0)),
        blk(prev, CB_SWK), blk(cur, CB_SWK), blk(prev, CB_SWV), blk(cur, CB_SWV),
        blk(prev, 0), blk(cur, 0), blk(prev, 0), blk(cur, 0),
        pl.BlockSpec(memory_space=pltpu.SMEM),
    ]


def _swa_probs(n, q_ref, kp_ref, kc_ref, vp_ref, vc_ref, cp_ref, cc_ref, sp_ref, sc_ref, sink_ref):
    lane = lax.broadcasted_iota(jnp.int32, (1, BLOCK), 1)
    halves = (lane < HEAD_DIM, lane >= HEAD_DIM)
    cosc, sinc = cc_ref[...], sc_ref[...]
    qs = [_rope(q_ref[:, p * BLOCK:(p + 1) * BLOCK], cosc, sinc) * SCALE for p in range(SWA_PAIRS)]
    kb = jnp.concatenate([_rope(kp_ref[...], cp_ref[...], sp_ref[...]), _rope(kc_ref[...], cosc, sinc)], axis=0)
    vb = jnp.concatenate([vp_ref[...], vc_ref[...]], axis=0)
    kv = {True: (kb, vb), False: (pltpu.roll(kb, HEAD_DIM, axis=1), pltpu.roll(vb, HEAD_DIM, axis=1))}
    rows = PAIRS_PER_KV * BLOCK
    r = lax.broadcasted_iota(jnp.int32, (rows, 2 * BLOCK), 0) % BLOCK
    c = lax.broadcasted_iota(jnp.int32, (rows, 2 * BLOCK), 1)
    valid = (c > r) & (c <= r + BLOCK) & ((n - 1) * BLOCK + c >= PAD)
    combos = [(g, a) for g in range(SWA_KV_HEADS) for a in range(2)]
    qst, ksel, vsel, scores = {}, {}, {}, {}
    for g, a in combos:
        qst[g, a] = jnp.concatenate(
            [jnp.where(halves[a], qs[g * PAIRS_PER_KV + j], 0.0) for j in range(PAIRS_PER_KV)], axis=0).astype(BF16)
        ksel[g, a], vsel[g, a] = kv[g == a]
    for g, a in combos:
        scores[g, a] = _dot_nt(qst[g, a], ksel[g, a].astype(BF16))
    out = {}
    for g, a in combos:
        s = jnp.where(valid, scores[g, a], -1e30)
        sink = jnp.concatenate([jnp.full((BLOCK, BLOCK), sink_ref[0, 2 * (g * PAIRS_PER_KV + j) + a], F32)
                                for j in range(PAIRS_PER_KV)], axis=0)
        mx = jnp.maximum(jnp.max(s, axis=1, keepdims=True), sink)
        pe = jnp.exp(s - _twice(mx))
        es = jnp.exp(sink - mx)
        inv = 1.0 / (_row_sums(pe) + es)
        out[g, a] = (qst[g, a], ksel[g, a], vsel[g, a], pe * _twice(inv), es * inv, halves[a])
    return combos, out


def _twice(x):
    return jnp.concatenate([x, x], axis=1)


def _row_sums(x):
    return _dot(_split_bf16(x), jnp.ones((4 * BLOCK, BLOCK), BF16))


def swa_attention_fwd(proj, cos, sin, sinks):
    L = proj.shape[0]

    def body(q_ref, kp_ref, kc_ref, vp_ref, vc_ref, cp_ref, cc_ref, sp_ref, sc_ref, sink_ref, o_ref):
        n = pl.program_id(0)
        combos, parts = _swa_probs(n, q_ref, kp_ref, kc_ref, vp_ref, vc_ref, cp_ref, cc_ref, sp_ref, sc_ref, sink_ref)
        outs = {}
        for g, a in combos:
            qst, ksel, vsel, probs, psink, half = parts[g, a]
            outs[g, a] = _dot(probs.astype(BF16), jnp.where(half, vsel, 0.0).astype(BF16))
        for g in range(SWA_KV_HEADS):
            both = outs[g, 0] + outs[g, 1]
            for j in range(PAIRS_PER_KV):
                p = g * PAIRS_PER_KV + j
                o_ref[:, p * BLOCK:(p + 1) * BLOCK] = both[j * BLOCK:(j + 1) * BLOCK]

    return pl.pallas_call(
        body,
        name="swa_attention_fwd",
        grid=(L // BLOCK,),
        in_specs=_swa_specs(),
        out_specs=pl.BlockSpec((BLOCK, SWA_WIDTH), lambda n: (n, 0)),
        out_shape=jax.ShapeDtypeStruct((L, SWA_WIDTH), F32),
        compiler_params=_params(("arbitrary",)),
    )(proj, proj, proj, proj, proj, cos, cos, sin, sin, sinks)


def swa_attention_bwd(proj, cos, sin, sinks, do_sw):
    L = proj.shape[0]

    def body(q_ref, kp_ref, kc_ref, vp_ref, vc_ref, cp_ref, cc_ref, sp_ref, sc_ref, sink_ref, do_ref,
             dq_ref, dk_ref, dv_ref, ds_ref):
        n = pl.program_id(0)

        @pl.when(n == 0)
        def _():
            dk_ref[...] = jnp.zeros_like(dk_ref)
            dv_ref[...] = jnp.zeros_like(dv_ref)
            ds_ref[...] = jnp.zeros_like(ds_ref)

        combos, parts = _swa_probs(n, q_ref, kp_ref, kc_ref, vp_ref, vc_ref, cp_ref, cc_ref, sp_ref, sc_ref, sink_ref)
        lane8 = lax.broadcasted_iota(jnp.int32, (8, BLOCK), 1)
        dos, dps = {}, {}
        for g, a in combos:
            half = parts[g, a][5]
            dos[g, a] = jnp.concatenate(
                [jnp.where(half, do_ref[:, (g * PAIRS_PER_KV + j) * BLOCK:(g * PAIRS_PER_KV + j + 1) * BLOCK], 0.0)
                 for j in range(PAIRS_PER_KV)], axis=0).astype(BF16)
        for g, a in combos:
            dps[g, a] = _dot_nt(dos[g, a], parts[g, a][2].astype(BF16))
        dqs = {}
        dkb = jnp.zeros((2 * BLOCK, BLOCK), F32)
        dvb = jnp.zeros((2 * BLOCK, BLOCK), F32)
        dsk = jnp.zeros((8, BLOCK), F32)
        for g, a in combos:
            qst, ksel, vsel, probs, psink, half = parts[g, a]
            dp = dps[g, a]
            delta = _row_sums(probs * dp)
            ds = (probs * (dp - _twice(delta))).astype(BF16)
            pd = psink * delta
            for j in range(PAIRS_PER_KV):
                head = 2 * (g * PAIRS_PER_KV + j) + a
                dsk = dsk + jnp.where(lane8 == head, -jnp.sum(pd[j * BLOCK:(j + 1) * BLOCK, :1]), 0.0)
            dqs[g, a] = _dot(ds, jnp.where(half, ksel, 0.0).astype(BF16))
            dk_a = _dot_tn(ds, qst)
            dv_a = _dot_tn(probs.astype(BF16), dos[g, a])
            if g != a:
                dk_a = pltpu.roll(dk_a, HEAD_DIM, axis=1)
                dv_a = pltpu.roll(dv_a, HEAD_DIM, axis=1)
            dkb = dkb + dk_a
            dvb = dvb + dv_a
        cosc, sinc = cc_ref[...], sc_ref[...]
        for g in range(SWA_KV_HEADS):
            both = (dqs[g, 0] + dqs[g, 1]) * SCALE
            for j in range(PAIRS_PER_KV):
                p = g * PAIRS_PER_KV + j
                dq_ref[:, p * BLOCK:(p + 1) * BLOCK] = _unrope(both[j * BLOCK:(j + 1) * BLOCK], cosc, sinc).astype(BF16)
        ds_ref[...] += dsk
        cur = pl.ds(pl.multiple_of(n * BLOCK, BLOCK), BLOCK)
        dk_ref[cur, :] += _unrope(dkb[BLOCK:], cosc, sinc)
        dv_ref[cur, :] += dvb[BLOCK:]

        @pl.when(n > 0)
        def _():
            prv = pl.ds(pl.multiple_of((n - 1) * BLOCK, BLOCK), BLOCK)
            dk_ref[prv, :] += _unrope(dkb[:BLOCK], cp_ref[...], sp_ref[...])
            dv_ref[prv, :] += dvb[:BLOCK]

    whole = lambda n: (0, 0)
    row = pl.BlockSpec((BLOCK, SWA_WIDTH), lambda n: (n, 0))
    return pl.pallas_call(
        body,
        name="swa_attention_bwd",
        grid=(L // BLOCK,),
        in_specs=_swa_specs() + [row],
        out_specs=[row, pl.BlockSpec((L, BLOCK), whole), pl.BlockSpec((L, BLOCK), whole),
                   pl.BlockSpec((8, BLOCK), whole)],
        out_shape=[jax.ShapeDtypeStruct((L, SWA_WIDTH), BF16), jax.ShapeDtypeStruct((L, BLOCK), F32),
                   jax.ShapeDtypeStruct((L, BLOCK), F32), jax.ShapeDtypeStruct((8, BLOCK), F32)],
        compiler_params=_params(("arbitrary",)),
    )(proj, proj, proj, proj, proj, cos, cos, sin, sin, sinks, do_sw)


ROW_TILE = 640
TAIL_ROWS = 208


def _pick(n, cands):
    for c in cands:
        if n % c == 0:
            return c
    raise ValueError(f"no tile for {n}")


def in_proj(h0, gain, w, name, out_dtype):
    L, D = h0.shape
    N = w.shape[1]
    tm = _pick(L, (ROW_TILE, BLOCK))
    tn = _pick(N, (1792, 1536, 1280, 896, 640, 512, 384, 256, 128))

    def body(h_ref, g_ref, w_ref, o_ref, xn_ref):
        @pl.when(pl.program_id(1) == 0)
        def _():
            x = h_ref[...]
            r = lax.rsqrt(jnp.mean(x * x, axis=1, keepdims=True) + RMS_EPS)
            xn_ref[...] = ((x * r) * g_ref[...]).astype(BF16)
        o_ref[...] = _dot(xn_ref[...], w_ref[...]).astype(out_dtype)

    return pl.pallas_call(
        body,
        name=name,
        grid=(L // tm, N // tn),
        in_specs=[pl.BlockSpec((tm, D), lambda i, j: (i, 0)),
                  pl.BlockSpec((1, D), lambda i, j: (0, 0)),
                  pl.BlockSpec((D, tn), lambda i, j: (0, j))],
        out_specs=[pl.BlockSpec((tm, tn), lambda i, j: (i, j)),
                   pl.BlockSpec((tm, D), lambda i, j: (i, 0))],
        out_shape=[jax.ShapeDtypeStruct((L, N), out_dtype), jax.ShapeDtypeStruct((L, D), BF16)],
        compiler_params=_params(("arbitrary", "arbitrary")),
    )(h0, gain, w)


def matmul_tn(a, b, name):
    Kd, M = a.shape
    N = b.shape[1]
    tk = _pick(Kd, (ROW_TILE, BLOCK))
    tn = _pick(N, (1280, 1024, 896, 640, 512, 256, 128))
    nk = Kd // tk

    def body(a_ref, b_ref, o_ref):
        k = pl.program_id(1)

        @pl.when(k == 0)
        def _():
            o_ref[...] = jnp.zeros_like(o_ref)
        o_ref[...] += _dot_tn(a_ref[...], b_ref[...])

    return pl.pallas_call(
        body,
        name=name,
        grid=(N // tn, nk),
        in_specs=[pl.BlockSpec((tk, M), lambda j, k: (k, 0)),
                  pl.BlockSpec((tk, tn), lambda j, k: (k, j))],
        out_specs=pl.BlockSpec((M, tn), lambda j, k: (0, j)),
        out_shape=jax.ShapeDtypeStruct((M, N), F32),
        compiler_params=_params(("arbitrary", "arbitrary")),
    )(a, b)


def in_proj_bwd(dproj, w, h0, gain, dh1):
    L, N = dproj.shape
    D = w.shape[0]
    tm = _pick(L, (ROW_TILE, BLOCK))
    tk = _pick(N, (1280, 640, 512, 256, 128))
    nk = N // tk

    def body(dp_ref, w_ref, h_ref, g_ref, dh1_ref, dh0_ref, dg_ref, acc_ref):
        i, k = pl.program_id(0), pl.program_id(1)

        @pl.when(k == 0)
        def _():
            acc_ref[...] = jnp.zeros_like(acc_ref)

        @pl.when((i == 0) & (k == 0))
        def _():
            dg_ref[...] = jnp.zeros_like(dg_ref)

        acc_ref[...] += _dot_nt(dp_ref[...], w_ref[...])

        @pl.when(k == nk - 1)
        def _():
            x = h_ref[...]
            r = lax.rsqrt(jnp.mean(x * x, axis=1, keepdims=True) + RMS_EPS)
            xhat = x * r
            dxn = acc_ref[...]
            dg_ref[...] += jnp.sum(dxn * xhat, axis=0, keepdims=True)
            dxh = dxn * g_ref[...]
            dh0_ref[...] = r * (dxh - xhat * jnp.mean(dxh * xhat, axis=1, keepdims=True)) + dh1_ref[...]

    row = pl.BlockSpec((tm, D), lambda i, k: (i, 0))
    vec = pl.BlockSpec((1, D), lambda i, k: (0, 0))
    return pl.pallas_call(
        body,
        name="in_proj_bwd",
        grid=(L // tm, nk),
        in_specs=[pl.BlockSpec((tm, tk), lambda i, k: (i, k)),
                  pl.BlockSpec((D, tk), lambda i, k: (0, k)),
                  row, vec, row],
        out_specs=[row, vec],
        out_shape=[jax.ShapeDtypeStruct((L, D), F32), jax.ShapeDtypeStruct((1, D), F32)],
        scratch_shapes=[pltpu.VMEM((tm, D), F32)],
        compiler_params=_params(("arbitrary", "arbitrary")),
    )(dproj, w, h0, gain, dh1)


def tail_fwd_bwd(h0, tgt, o_sb, o_sw, gates, w_bsb, w_bswa, w_out, gain_f):
    L, D = h0.shape
    R = _pick(L, (TAIL_ROWS, BLOCK))
    z0, z1, z2, z3 = 0, SB_WIDTH, SB_WIDTH + SWA_WIDTH, SB_WIDTH + SWA_WIDTH + D_MODEL

    def body(h_ref, t_ref, osb_ref, osw_ref, g_ref, wsb_ref, wsw_ref, wo_ref, gf_ref,
             dosb_ref, dosw_ref, dg_ref, dh1_ref, mb_ref, usb_ref, usw_ref, dh1b_ref, dysb_ref, dysw_ref,
             dgf_ref, loss_ref):
        i = pl.program_id(0)

        @pl.when(i == 0)
        def _():
            dgf_ref[...] = jnp.zeros_like(dgf_ref)
            loss_ref[...] = jnp.zeros_like(loss_ref)

        sbz = g_ref[:, z0:z1]
        swz = g_ref[:, z1:z2]
        s1 = jax.nn.sigmoid(g_ref[:, z2:z3])
        s2 = jax.nn.sigmoid(g_ref[:, z3:])
        sg_sb = jax.nn.sigmoid(sbz)
        sg_sw = jax.nn.sigmoid(swz)
        silu_sb = sbz * sg_sb
        silu_sw = swz * sg_sw
        osb = osb_ref[...]
        osw = osw_ref[...]
        usb = (osb * silu_sb).astype(BF16)
        usw = (osw * silu_sw).astype(BF16)
        y_sb = _dot(usb, wsb_ref[...])
        y_sw = _dot(usw, wsw_ref[...])
        mb = (s1 * y_sb + s2 * y_sw).astype(BF16)
        h1 = h_ref[...] + _dot(mb, wo_ref[...])
        rf = lax.rsqrt(jnp.mean(h1 * h1, axis=1, keepdims=True) + RMS_EPS)
        hhat = h1 * rf
        gf = gf_ref[...]
        row = i * R + lax.broadcasted_iota(jnp.int32, (R, 1), 0)
        err = jnp.where(row >= BLOCK, hhat * gf - t_ref[...], 0.0)
        lane0 = (lax.broadcasted_iota(jnp.int32, (8, BLOCK), 0) == 0) & (lax.broadcasted_iota(jnp.int32, (8, BLOCK), 1) == 0)
        loss_ref[...] += jnp.where(lane0, (0.5 / D) * jnp.sum(err * err), 0.0)
        dy = err * (1.0 / D)
        dgf_ref[...] += jnp.sum(dy * hhat, axis=0, keepdims=True)
        dhh = dy * gf
        dh1 = rf * (dhh - hhat * jnp.mean(dhh * hhat, axis=1, keepdims=True))
        dh1b = dh1.astype(BF16)
        dm = _dot_nt(dh1b, wo_ref[...])
        dysb = (dm * s1).astype(BF16)
        dysw = (dm * s2).astype(BF16)
        dusb = _dot_nt(dysb, wsb_ref[...])
        dusw = _dot_nt(dysw, wsw_ref[...])
        dosb_ref[...] = (dusb * silu_sb).astype(BF16)
        dosw_ref[...] = (dusw * silu_sw).astype(BF16)
        dg_ref[:, z0:z1] = (dusb * osb * (sg_sb * (1.0 + sbz * (1.0 - sg_sb)))).astype(BF16)
        dg_ref[:, z1:z2] = (dusw * osw * (sg_sw * (1.0 + swz * (1.0 - sg_sw)))).astype(BF16)
        dg_ref[:, z2:z3] = (dm * y_sb * (s1 * (1.0 - s1))).astype(BF16)
        dg_ref[:, z3:] = (dm * y_sw * (s2 * (1.0 - s2))).astype(BF16)
        dh1_ref[...] = dh1
        mb_ref[...] = mb
        usb_ref[...] = usb
        usw_ref[...] = usw
        dh1b_ref[...] = dh1b
        dysb_ref[...] = dysb
        dysw_ref[...] = dysw

    def rows(n):
        return pl.BlockSpec((R, n), lambda i: (i, 0))

    def whole(shape):
        return pl.BlockSpec(shape, lambda i: (0, 0))

    GW = gates.shape[1]
    return pl.pallas_call(
        body,
        name="tail_fwd_bwd",
        grid=(L // R,),
        in_specs=[rows(D), rows(D), rows(SB_WIDTH), rows(SWA_WIDTH), rows(GW),
                  whole(w_bsb.shape), whole(w_bswa.shape), whole(w_out.shape), whole((1, D))],
        out_specs=[rows(SB_WIDTH), rows(SWA_WIDTH), rows(GW), rows(D),
                   rows(D), rows(SB_WIDTH), rows(SWA_WIDTH), rows(D), rows(D), rows(D),
                   whole((1, D)), whole((8, BLOCK))],
        out_shape=[jax.ShapeDtypeStruct((L, SB_WIDTH), BF16), jax.ShapeDtypeStruct((L, SWA_WIDTH), BF16),
                   jax.ShapeDtypeStruct((L, GW), BF16), jax.ShapeDtypeStruct((L, D), F32),
                   jax.ShapeDtypeStruct((L, D), BF16), jax.ShapeDtypeStruct((L, SB_WIDTH), BF16),
                   jax.ShapeDtypeStruct((L, SWA_WIDTH), BF16), jax.ShapeDtypeStruct((L, D), BF16),
                   jax.ShapeDtypeStruct((L, D), BF16), jax.ShapeDtypeStruct((L, D), BF16),
                   jax.ShapeDtypeStruct((1, D), F32), jax.ShapeDtypeStruct((8, BLOCK), F32)],
        compiler_params=_params(("arbitrary",)),
    )(h0, tgt, o_sb, o_sw, gates, w_bsb, w_bswa, w_out, gain_f)


def local_step(x, tgt, meta, gain, w_in, w_bsb, w_bswa, w_out, sinks, gain_f):
    S, D = x.shape
    L = S + BLOCK
    h0 = jnp.concatenate([jnp.zeros((PAD, D), F32), meta, x], axis=0)
    tgt_p = jnp.concatenate([jnp.zeros((BLOCK, D), F32), tgt], axis=0)
    tt = _suffix_matrix()
    cos, sin = rope_tables(L)
    qkv, xn = in_proj(h0, gain, w_in[:, :SB_COLS], "in_proj_sb", BF16)
    proj_sw, _ = in_proj(h0, gain, w_in[:, SB_COLS:GATE_COL0], "in_proj_swa", F32)
    gates, _ = in_proj(h0, gain, w_in[:, GATE_COL0:], "in_proj_gates", F32)
    o_sb, w_sb = sb_attention_fwd(qkv, tt)
    o_sw = swa_attention_fwd(proj_sw, cos, sin, sinks)
    (do_sb, do_sw, dgates, dh1, mb, usb, usw, dh1b, dysb, dysw, dgf, loss) = tail_fwd_bwd(
        h0, tgt_p, o_sb, o_sw, gates, w_bsb, w_bswa, w_out, gain_f)
    dq_sb, dk_sb, dv_sb = sb_attention_bwd(qkv, o_sb, w_sb, do_sb, tt)
    dq_sw, dk_sw, dv_sw, dsinks = swa_attention_bwd(proj_sw, cos, sin, sinks, do_sw)
    dproj = jnp.concatenate([dq_sb, dk_sb, dv_sb, dq_sw, dk_sw.astype(BF16), dv_sw.astype(BF16), dgates], axis=1)
    dw_in = matmul_tn(xn, dproj, "dw_in")
    dw_out = matmul_tn(mb, dh1b, "dw_out")
    dw_bsb = matmul_tn(usb, dysb, "dw_bsb")
    dw_bswa = matmul_tn(usw, dysw, "dw_bswa")
    dh0, dgain = in_proj_bwd(dproj, w_in, h0, gain, dh1)
    return (loss[0, 0], dh0[BLOCK:], dh0[PAD:BLOCK], dgain, dw_in, dw_bsb, dw_bswa, dw_out,
            dsinks[:1, :SWA_Q_HEADS], dgf)


MESH_IDS = pl.DeviceIdType.MESH
ANY = pl.BlockSpec(memory_space=pl.ANY)


def _place():
    return lax.axis_index("x"), lax.axis_index("y"), lax.axis_index("c")


def _index(x, y, c):
    return 4 * x + 2 * y + c


def all_gather(block, name):
    def body(x_ref, out_ref, send_sems, recv_sems, local_sem):
        x, y, c = _place()
        me, sibling = (x, y, c), (x, y, 1 - c)
        chips = [(1 - x, y), (x, 1 - y), (1 - x, 1 - y)]

        def copy(k, blk, to, src=None):
            dst = out_ref.at[_index(*blk)]
            return pltpu.make_async_remote_copy(
                src_ref=dst if src is None else src, dst_ref=dst,
                send_sem=send_sems.at[k], recv_sem=recv_sems.at[k], device_id=to, device_id_type=MESH_IDS)

        mine = pltpu.make_async_copy(x_ref, out_ref.at[_index(*me)], local_sem)
        mine.start()
        first = [copy(0, me, sibling, src=x_ref)]
        first += [copy(1 + j, me, (*chip, c), src=x_ref) for j, chip in enumerate(chips)]
        for cp in first:
            cp.start()
        passed = [copy(4 + j, (*chip, c), sibling) for j, chip in enumerate(chips)]
        for j, chip in enumerate(chips):
            copy(1 + j, (*chip, c), me).wait_recv()
            passed[j].start()
        copy(0, sibling, me).wait_recv()
        for j, chip in enumerate(chips):
            copy(4 + j, (*chip, 1 - c), me).wait_recv()
        for cp in first + passed:
            cp.wait_send()
        mine.wait()

    return pl.pallas_call(
        body,
        name=name,
        out_shape=jax.ShapeDtypeStruct((N_DEV,) + block.shape, block.dtype),
        in_specs=[ANY],
        out_specs=ANY,
        scratch_shapes=[pltpu.SemaphoreType.DMA((7,)), pltpu.SemaphoreType.DMA((7,)), pltpu.SemaphoreType.DMA],
    )(block)


def exchange_partials(parts):
    def body(g_ref, out_ref, send_sems, recv_sems, local_sem):
        x, y, c = _place()
        me = _index(x, y, c)
        mine = pltpu.make_async_copy(g_ref.at[me], out_ref.at[me], local_sem)
        mine.start()
        copies = []
        for m in range(1, N_DEV):
            px = 1 - x if m & 4 else x
            py = 1 - y if m & 2 else y
            pc = 1 - c if m & 1 else c
            cp = pltpu.make_async_remote_copy(
                src_ref=g_ref.at[_index(px, py, pc)], dst_ref=out_ref.at[me],
                send_sem=send_sems.at[m - 1], recv_sem=recv_sems.at[m - 1],
                device_id=(px, py, pc), device_id_type=MESH_IDS)
            cp.start()
            copies.append(cp)
        for cp in copies:
            cp.wait()
        mine.wait()

    return pl.pallas_call(
        body,
        name="exchange_partials",
        out_shape=jax.ShapeDtypeStruct(parts.shape, parts.dtype),
        in_specs=[ANY],
        out_specs=ANY,
        scratch_shapes=[pltpu.SemaphoreType.DMA((7,)), pltpu.SemaphoreType.DMA((7,)), pltpu.SemaphoreType.DMA],
    )(parts)


def _adamw(w, g, m, v):
    m = ADAM_B1 * m + (1.0 - ADAM_B1) * g
    v = ADAM_B2 * v + (1.0 - ADAM_B2) * (g * g)
    m_hat = m / (1.0 - ADAM_B1 ** ADAM_STEP)
    v_hat = v / (1.0 - ADAM_B2 ** ADAM_STEP)
    delta = -ADAM_LR * (m_hat / (jnp.sqrt(v_hat) + ADAM_EPS) + ADAM_WD * w)
    return delta, m, v


def sum_and_adamw(parts, w, m, v, name):
    _, R, C = parts.shape
    tr = _pick(R, (528, 512, 256, 128, 24, 8))

    def body(p_ref, w_ref, m_ref, v_ref, g_ref, d_ref, nm_ref, nv_ref):
        g = p_ref[0].astype(F32)
        for s in range(1, N_DEV):
            g = g + p_ref[s].astype(F32)
        d, nm, nv = _adamw(w_ref[...], g, m_ref[...], v_ref[...])
        g_ref[...] = g
        d_ref[...] = d
        nm_ref[...] = nm
        nv_ref[...] = nv

    row = pl.BlockSpec((tr, C), lambda i: (i, 0))
    return pl.pallas_call(
        body,
        name=name,
        grid=(R // tr,),
        in_specs=[pl.BlockSpec((N_DEV, tr, C), lambda i: (0, i, 0)), row, row, row],
        out_specs=[row, row, row, row],
        out_shape=[jax.ShapeDtypeStruct((R, C), F32)] * 4,
        compiler_params=_params(("arbitrary",)),
    )(parts, w, m, v)


W_IN_SHARD = IN_COLS // N_DEV
ROWS_W_IN = D_MODEL * W_IN_SHARD // BLOCK
ROWS_W_BSB = SB_WIDTH
ROWS_W_ROWSHARD = D_MODEL
SMALL_ROWS = 24


def _pack_shards(w_in, w_bsb, w_bswa, w_out, meta):
    return jnp.concatenate([w_in.reshape(ROWS_W_IN, BLOCK), w_bsb.reshape(ROWS_W_BSB, BLOCK),
                            w_bswa.reshape(ROWS_W_ROWSHARD, BLOCK), w_out.reshape(ROWS_W_ROWSHARD, BLOCK),
                            meta.reshape(N_META, BLOCK)], axis=0)


def _unpack_shards(p):
    o = np.cumsum([0, ROWS_W_IN, ROWS_W_BSB, ROWS_W_ROWSHARD, ROWS_W_ROWSHARD, N_META])
    return (p[o[0]:o[1]].reshape(1, D_MODEL, W_IN_SHARD), p[o[1]:o[2]].reshape(1, SB_WIDTH, BLOCK),
            p[o[2]:o[3]].reshape(1, BLOCK, D_MODEL), p[o[3]:o[4]].reshape(1, BLOCK, D_MODEL),
            p[o[4]:o[5]].reshape(N_META, BLOCK))


def _pack_by_owner(dw_in, dw_bsb, dw_bswa, dw_out, dmeta):
    cols = lambda a: a.reshape(a.shape[0], N_DEV, -1).transpose(1, 0, 2)
    return jnp.concatenate([cols(dw_in).reshape(N_DEV, ROWS_W_IN, BLOCK), cols(dw_bsb),
                            dw_bswa.reshape(N_DEV, ROWS_W_ROWSHARD, BLOCK), dw_out.reshape(N_DEV, ROWS_W_ROWSHARD, BLOCK),
                            cols(dmeta)], axis=1)


def _pack_small(gain, gain_f, sinks, loss):
    z = jnp.zeros((SMALL_ROWS - 16, BLOCK), F32)
    z = z.at[0, :SWA_Q_HEADS].set(sinks.reshape(-1)).at[1, 0].set(loss)
    return jnp.concatenate([gain.reshape(8, BLOCK), gain_f.reshape(8, BLOCK), z], axis=0)


def _unpack_small(p):
    return p[0:8].reshape(1, D_MODEL), p[8:16].reshape(D_MODEL), p[16:17, :SWA_Q_HEADS], p[17, 0]


def kernel(x, meta_tokens, norm_gain, w_in, w_branch_sb, w_branch_swa, w_out, attn_sinks, final_norm_gain, loss_target, m_meta_tokens, m_norm_gain, m_w_in, m_w_branch_sb, m_w_branch_swa, m_w_out, m_attn_sinks, m_final_norm_gain, v_meta_tokens, v_norm_gain, v_w_in, v_w_branch_sb, v_w_branch_swa, v_w_out, v_attn_sinks, v_final_norm_gain):
    meta_bits = lax.bitcast_convert_type(meta_tokens, BF16).reshape(2 * N_META, BLOCK)
    mine = jnp.concatenate([_pack_shards(w_in, w_branch_sb, w_branch_swa, w_out, meta_tokens)[:-N_META].astype(BF16),
                            meta_bits], axis=0)
    full = all_gather(mine, "all_gather_weights")
    o = np.cumsum([0, ROWS_W_IN, ROWS_W_BSB, ROWS_W_ROWSHARD, ROWS_W_ROWSHARD, 2 * N_META])
    cols = lambda a: a.transpose(1, 0, 2).reshape(a.shape[1], -1)
    f_w_in = cols(full[:, o[0]:o[1]].reshape(N_DEV, D_MODEL, W_IN_SHARD))
    f_w_bsb = cols(full[:, o[1]:o[2]])
    f_w_bswa = full[:, o[2]:o[3]].reshape(D_MODEL, D_MODEL)
    f_w_out = full[:, o[3]:o[4]].reshape(D_MODEL, D_MODEL)
    f_meta = cols(lax.bitcast_convert_type(full[:, o[4]:o[5]].reshape(N_DEV, N_META, BLOCK, 2), F32))

    (loss, grad_x, dmeta, dgain, dw_in, dw_bsb, dw_bswa, dw_out, dsinks, dgf) = local_step(
        x[0], loss_target[0], f_meta, norm_gain, f_w_in, f_w_bsb, f_w_bswa, f_w_out, attn_sinks,
        final_norm_gain.reshape(1, D_MODEL))

    parts = exchange_partials(_pack_by_owner(dw_in, dw_bsb, dw_bswa, dw_out, dmeta).astype(BF16))
    packs = [_pack_shards(a[0], b[0], c[0], d[0], e) for a, b, c, d, e in (
        (w_in, w_branch_sb, w_branch_swa, w_out, meta_tokens),
        (m_w_in, m_w_branch_sb, m_w_branch_swa, m_w_out, m_meta_tokens),
        (v_w_in, v_w_branch_sb, v_w_branch_swa, v_w_out, v_meta_tokens))]
    big = [_unpack_shards(p) for p in sum_and_adamw(parts, *packs, "sum_adamw_sharded")]

    small = all_gather(_pack_small(dgain, dgf, dsinks, loss), "all_gather_small")
    zero = jnp.zeros((), F32)
    spacks = [_pack_small(a, b, c, zero) for a, b, c in (
        (norm_gain, final_norm_gain, attn_sinks), (m_norm_gain, m_final_norm_gain, m_attn_sinks),
        (v_norm_gain, v_final_norm_gain, v_attn_sinks))]
    sm = [_unpack_small(p) for p in sum_and_adamw(small, *spacks, "sum_adamw_replicated")]

    def leaves(k):
        b, s = big[k], sm[k]
        return (b[4], s[0], b[0], b[1], b[2], b[3], s[2], s[1])

    return (sm[0][3], grad_x[None], *leaves(0), *leaves(1), *leaves(2), *leaves(3))
```

```python
import numpy as np
import jax
import jax.numpy as jnp
from jax import lax
from jax.experimental import pallas as pl
from jax.experimental.pallas import tpu as pltpu

F32 = jnp.float32
BF16 = jnp.bfloat16

D_MODEL = 1024
N_META = 16
BLOCK = 128
PAD = BLOCK - N_META
HEAD_DIM = 64
SB_HEADS = 8
SB_WIDTH = SB_HEADS * HEAD_DIM
SWA_Q_HEADS = 16
SWA_KV_HEADS = 2
SWA_WIDTH = SWA_Q_HEADS * HEAD_DIM
SWA_KV_WIDTH = SWA_KV_HEADS * HEAD_DIM
ROPE_THETA = 10000.0
RMS_EPS = 1e-6
SCALE = HEAD_DIM ** -0.5
SPLITS = (SB_WIDTH, SB_WIDTH, SB_WIDTH, SWA_WIDTH, SWA_KV_WIDTH, SWA_KV_WIDTH,
          SB_WIDTH, SWA_WIDTH, D_MODEL, D_MODEL)
IN_COLS = sum(SPLITS)
SB_COLS = 3 * SB_WIDTH
SWA_COLS = SWA_WIDTH + 2 * SWA_KV_WIDTH
GATE_COL0 = SB_COLS + SWA_COLS

N_DEV = 8
ADAM_LR = 0.001
ADAM_B1 = 0.9
ADAM_B2 = 0.999
ADAM_EPS = 1e-08
ADAM_WD = 0.01
ADAM_STEP = 10

VMEM_LIMIT = 56 * 1024 * 1024


def _params(sem, **kw):
    return pltpu.CompilerParams(dimension_semantics=sem, vmem_limit_bytes=VMEM_LIMIT, **kw)


def _dot(a, b):
    return jnp.dot(a, b, preferred_element_type=F32)


def _dot_nt(a, b):
    return lax.dot_general(a, b, (((1,), (1,)), ((), ())), preferred_element_type=F32)


def _dot_tn(a, b):
    return lax.dot_general(a, b, (((0,), (0,)), ((), ())), preferred_element_type=F32)


def _cat(xs, axis):
    return xs[0] if len(xs) == 1 else jnp.concatenate(xs, axis=axis)


def _split_bf16(x):
    hi = x.astype(BF16)
    lo = (x - hi.astype(F32)).astype(BF16)
    return jnp.concatenate([hi, lo], axis=1)


def _suffix_matrix():
    j = np.arange(BLOCK)[:, None]
    s = np.arange(BLOCK)[None, :]
    t = np.concatenate([(j >= s).astype(np.float32), np.ones((BLOCK, BLOCK), np.float32)], axis=1)
    return jnp.asarray(np.concatenate([t, t], axis=0), dtype=BF16)


def _softplus(z):
    return jnp.maximum(z, 0.0) + jnp.log(1.0 + jnp.exp(-jnp.abs(z)))


SB_SMALL = 4
SB_PAIRS_FWD = 4
SB_PAIRS_BWD = 2
SB_BIG_FWD = 4
SB_BIG_BWD = 4


def _sb_masks(i, nblk):
    if nblk == 0:
        r = lax.broadcasted_iota(jnp.int32, (BLOCK, BLOCK), 0)
        c = lax.broadcasted_iota(jnp.int32, (BLOCK, BLOCK), 1)
        return (c < r) & (i * BLOCK + c >= PAD)
    return lax.broadcasted_iota(jnp.int32, (BLOCK, nblk * BLOCK), 1) >= PAD


def _sb_heads(npairs):
    return [(pr, a) for pr in range(npairs) for a in range(2)]


def _lanes(pr):
    return slice(pr * BLOCK, (pr + 1) * BLOCK)


def _masked_heads(x, half0, npairs):
    out = []
    for pr, a in _sb_heads(npairs):
        xp = x[:, _lanes(pr)]
        out.append((jnp.where(half0, xp, 0.0) if a == 0 else jnp.where(half0, 0.0, xp)).astype(BF16))
    return out


def _sb_sweep(i, tile, carry, big):
    same = lambda cr: cr
    small = SB_SMALL
    live_tile = tile
    tile = lambda j0, nblk, mask, cr: lax.cond(cr[0], lambda c: live_tile(j0, nblk, mask, c), same, cr)
    carry = live_tile(i, 1, _sb_masks(i, 0), carry)
    n_big = i // big
    rest = i - n_big * big
    n_small = rest // small
    n_one = rest - n_small * small
    low_is_big = n_big > 0
    low_is_small = jnp.logical_not(low_is_big) & (n_small > 0)
    low_is_one = jnp.logical_not(low_is_big) & (n_small == 0) & (n_one > 0)

    n_plain = n_one - jnp.where(low_is_one, 1, 0)
    carry = lax.fori_loop(0, n_plain, lambda t, cr: tile(i - 1 - t, 1, None, cr), carry)
    carry = lax.cond(low_is_one, lambda cr: tile(0, 1, _sb_masks(i, 1), cr), same, carry)
    if big != small:
        base = n_big * big
        n_plain = n_small - jnp.where(low_is_small, 1, 0)
        carry = lax.fori_loop(0, n_plain, lambda t, cr: tile(base + (n_small - 1 - t) * small, small, None, cr), carry)
        carry = lax.cond(low_is_small, lambda cr: tile(0, small, _sb_masks(i, small), cr), same, carry)
    carry = lax.fori_loop(0, jnp.maximum(n_big - 1, 0), lambda t, cr: tile((n_big - 1 - t) * big, big, None, cr), carry)
    return lax.cond(low_is_big, lambda cr: tile(0, big, _sb_masks(i, big), cr), same, carry)


def _sb_weights(zs, ss, cs, tt, nblk, mask):
    suf = _dot(_cat([_split_bf16(s[:, b * BLOCK:(b + 1) * BLOCK]) for s in ss for b in range(nblk)], 0), tt)
    ws, out_cs = [], []
    for h in range(len(zs)):
        c = cs[h]
        wb = [None] * nblk
        for b in reversed(range(nblk)):
            sab = suf[(h * nblk + b) * BLOCK:(h * nblk + b + 1) * BLOCK]
            wb[b] = jnp.exp(zs[h][:, b * BLOCK:(b + 1) * BLOCK] + c - sab[:, :BLOCK])
            c = c - sab[:, BLOCK:]
        w = _cat(wb, 1)
        if mask is not None:
            w = jnp.where(mask, w, 0.0)
        ws.append(w)
        out_cs.append(c)
    return ws, out_cs


SB_DEAD = -110.0


def _row_norm_max(x, half0, a):
    xf = x.astype(F32)
    sq = jnp.where(half0, xf * xf, 0.0) if a == 0 else jnp.where(half0, 0.0, xf * xf)
    return jnp.sqrt(jnp.max(jnp.sum(sq, axis=1, keepdims=True)))


def _sb_logit_bounds(i, q_heads, k_ref, kmax_ref, half0, heads):
    @pl.when(i == 0)
    def _():
        for h, (pr, a) in enumerate(heads):
            kmax_ref[h] = _row_norm_max(k_ref[:, _lanes(pr)], half0, a)
    return [_row_norm_max(q_heads[h], half0, a) * kmax_ref[h] * 1.001 + 0.01 for h, (pr, a) in enumerate(heads)]


def _sb_alive(cs, zmax):
    worst = cs[0] + zmax[0]
    for c, zm in zip(cs[1:], zmax[1:]):
        worst = jnp.maximum(worst, c + zm)
    return jnp.max(worst) > SB_DEAD


def sb_attention_fwd(qkv, tt):
    L = qkv.shape[0]
    nb = L // BLOCK
    NP = SB_PAIRS_FWD
    SB_W = NP * BLOCK
    nq = SB_WIDTH // SB_W

    def body(q_ref, k_ref, v_ref, tt_ref, o_ref, kmax_ref):
        i = pl.program_id(1)
        half0 = lax.broadcasted_iota(jnp.int32, (1, BLOCK), 1) < HEAD_DIM
        heads = _sb_heads(NP)
        qh = _masked_heads(q_ref[...].astype(F32) * SCALE, half0, NP)
        zmax = _sb_logit_bounds(i, qh, k_ref, kmax_ref, half0, heads)

        def rows_of(j0, nblk):
            return pl.ds(pl.multiple_of(j0 * BLOCK, BLOCK), nblk * BLOCK)

        def tile(j0, nblk, mask, carry):
            _, accs, cs = carry
            rows = rows_of(j0, nblk)
            zs = [_dot_nt(qh[h], k_ref[rows, _lanes(pr)]) for h, (pr, a) in enumerate(heads)]
            ss = [_softplus(z) for z in zs]
            if mask is not None:
                ss = [jnp.where(mask, s, 0.0) for s in ss]
            ws, cs = _sb_weights(zs, ss, cs, tt_ref[...], nblk, mask)
            accs = [accs[h] + _dot(ws[h].astype(BF16), v_ref[rows, _lanes(pr)]) for h, (pr, a) in enumerate(heads)]
            return _sb_alive(cs, zmax), tuple(accs), tuple(cs)

        zero = (jnp.zeros((BLOCK, BLOCK), F32),) * len(heads)
        accs = _sb_sweep(i, tile, (jnp.bool_(True), zero, zero), SB_BIG_FWD)[1]
        o_ref[...] = _cat([jnp.where(half0, accs[2 * pr], accs[2 * pr + 1]) for pr in range(NP)], 1)

    panel = lambda c: pl.BlockSpec((L, SB_W), lambda p, i: (0, c * nq + p), pipeline_mode=pl.Buffered(1))
    return pl.pallas_call(
        body,
        name="sb_attention_fwd",
        grid=(nq, nb),
        in_specs=[
            pl.BlockSpec((BLOCK, SB_W), lambda p, i: (i, p)),
            panel(1), panel(2),
            pl.BlockSpec((2 * BLOCK, 2 * BLOCK), lambda p, i: (0, 0)),
        ],
        out_specs=pl.BlockSpec((BLOCK, SB_W), lambda p, i: (i, p)),
        out_shape=jax.ShapeDtypeStruct((L, SB_WIDTH), F32),
        scratch_shapes=[pltpu.SMEM((2 * NP,), F32)],
        compiler_params=_params(("arbitrary", "arbitrary")),
    )(qkv, qkv, qkv, tt)


def sb_attention_bwd(qkv, o_sb, do_sb, tt):
    L = qkv.shape[0]
    nb = L // BLOCK
    NP = SB_PAIRS_BWD
    SB_W = NP * BLOCK
    nq = SB_WIDTH // SB_W

    def body(q_ref, k_ref, v_ref, o_ref, do_ref, tt_ref, dq_ref, dk_ref, dv_ref, dk_acc, dv_acc, kmax_ref):
        i = pl.program_id(1)
        half0 = lax.broadcasted_iota(jnp.int32, (1, BLOCK), 1) < HEAD_DIM
        heads = _sb_heads(NP)

        @pl.when(i == 0)
        def _():
            dk_acc[...] = jnp.zeros_like(dk_acc)
            dv_acc[...] = jnp.zeros_like(dv_acc)

        qh = _masked_heads(q_ref[...].astype(F32) * SCALE, half0, NP)
        zmax = _sb_logit_bounds(i, qh, k_ref, kmax_ref, half0, heads)
        do = do_ref[...]
        doh = _masked_heads(do, half0, NP)
        od = o_ref[...] * do.astype(F32)
        dsum = []
        for pr, a in heads:
            x = od[:, _lanes(pr)]
            x = jnp.where(half0, x, 0.0) if a == 0 else jnp.where(half0, 0.0, x)
            dsum.append(jnp.broadcast_to(jnp.sum(x, axis=1, keepdims=True), (BLOCK, BLOCK)))

        def rows_of(j0, nblk):
            return pl.ds(pl.multiple_of(j0 * BLOCK, BLOCK), nblk * BLOCK)

        def tile(j0, nblk, mask, carry):
            _, accs, cs, ces = carry
            rows = rows_of(j0, nblk)
            zs = [_dot_nt(qh[h], k_ref[rows, _lanes(pr)]) for h, (pr, a) in enumerate(heads)]
            dws = [_dot_nt(doh[h], v_ref[rows, _lanes(pr)]) for h, (pr, a) in enumerate(heads)]
            ss = [_softplus(z) for z in zs]
            sigs = [jnp.exp(z - s) for z, s in zip(zs, ss)]
            if mask is not None:
                ss = [jnp.where(mask, s, 0.0) for s in ss]
            ws, cs = _sb_weights(zs, ss, cs, tt_ref[...], nblk, mask)
            wbs = [w.astype(BF16) for w in ws]
            es = [wb.astype(F32) * dw for wb, dw in zip(wbs, dws)]
            esuf = _dot(_cat([_split_bf16(e[:, b * BLOCK:(b + 1) * BLOCK]) for e in es for b in range(nblk)], 0),
                        tt_ref[...])
            out_ces, dzbs = [], []
            for h, (pr, a) in enumerate(heads):
                ce = ces[h]
                dzs = [None] * nblk
                for b in reversed(range(nblk)):
                    eab = esuf[(h * nblk + b) * BLOCK:(h * nblk + b + 1) * BLOCK]
                    sl = slice(b * BLOCK, (b + 1) * BLOCK)
                    e = es[h][:, sl]
                    dzs[b] = e - sigs[h][:, sl] * (e + (ce - eab[:, :BLOCK]))
                    ce = ce - eab[:, BLOCK:]
                dz = _cat(dzs, 1)
                if mask is not None:
                    dz = jnp.where(mask, dz, 0.0)
                dzbs.append(dz.astype(BF16))
                out_ces.append(ce)
            accs = tuple(accs[h] + _dot(dzbs[h], k_ref[rows, _lanes(pr)]) for h, (pr, a) in enumerate(heads))
            dkv = [_dot_tn(_cat([dzbs[2 * pr], wbs[2 * pr], dzbs[2 * pr + 1], wbs[2 * pr + 1]], 0), qdo[pr])
                   for pr in range(NP)]
            dk_acc[rows, :] += _cat([x[:, :BLOCK] for x in dkv], 1)
            dv_acc[rows, :] += _cat([x[:, BLOCK:] for x in dkv], 1)
            return _sb_alive(cs, zmax), accs, tuple(cs), tuple(out_ces)

        zb = jnp.zeros((BLOCK, BLOCK), BF16)
        qdo = [_cat([_cat([qh[h], zb], 1) if kind == 0 else _cat([zb, doh[h]], 1)
                     for h in (2 * pr, 2 * pr + 1) for kind in (0, 1)], 0) for pr in range(NP)]
        zero = (jnp.zeros((BLOCK, BLOCK), F32),) * len(heads)
        accs = _sb_sweep(i, tile, (jnp.bool_(True), zero, zero, tuple(dsum)), SB_BIG_BWD)[1]
        dq_ref[...] = (_cat([jnp.where(half0, accs[2 * pr], accs[2 * pr + 1]) for pr in range(NP)], 1)
                       * SCALE).astype(BF16)

        @pl.when(i == nb - 1)
        def _():
            dk_ref[...] = dk_acc[...].astype(BF16)
            dv_ref[...] = dv_acc[...].astype(BF16)

    blk = pl.BlockSpec((BLOCK, SB_W), lambda p, i: (i, p))
    panel = pl.BlockSpec((L, SB_W), lambda p, i: (0, p))
    return pl.pallas_call(
        body,
        name="sb_attention_bwd",
        grid=(nq, nb),
        in_specs=[
            blk,
            pl.BlockSpec((L, SB_W), lambda p, i: (0, nq + p), pipeline_mode=pl.Buffered(1)),
            pl.BlockSpec((L, SB_W), lambda p, i: (0, 2 * nq + p), pipeline_mode=pl.Buffered(1)),
            blk, blk,
            pl.BlockSpec((2 * BLOCK, 2 * BLOCK), lambda p, i: (0, 0)),
        ],
        out_specs=[blk, panel, panel],
        out_shape=[jax.ShapeDtypeStruct((L, SB_WIDTH), BF16)] * 3,
        scratch_shapes=[pltpu.VMEM((L, SB_W), F32), pltpu.VMEM((L, SB_W), F32), pltpu.SMEM((2 * NP,), F32)],
        compiler_params=_params(("arbitrary", "arbitrary")),
    )(qkv, qkv, qkv, o_sb, do_sb, tt)


SWA_PAIRS = SWA_WIDTH // BLOCK
PAIRS_PER_KV = SWA_PAIRS // SWA_KV_HEADS
CB_SWK = SWA_PAIRS
CB_SWV = SWA_PAIRS + 1


def rope_tables(L):
    half = HEAD_DIM // 2
    inv = ROPE_THETA ** (-jnp.arange(half, dtype=F32) / half)
    pos = (jnp.arange(L) - PAD).astype(F32)
    ang = pos[:, None] * inv[None, :]
    reps = BLOCK // half
    return jnp.tile(jnp.cos(ang), (1, reps)), jnp.tile(jnp.sin(ang), (1, reps))


def _rot_half(x):
    lane = lax.broadcasted_iota(jnp.int32, (1, BLOCK), 1)
    first = (lane % HEAD_DIM) < (HEAD_DIM // 2)
    return jnp.where(first, -pltpu.roll(x, BLOCK - HEAD_DIM // 2, axis=1), pltpu.roll(x, HEAD_DIM // 2, axis=1))


def _rope(x, cos, sin):
    return x * cos + _rot_half(x) * sin


def _unrope(x, cos, sin):
    return x * cos - _rot_half(x) * sin


def _swa_specs():
    prev = lambda n: jnp.maximum(n - 1, 0)
    cur = lambda n: n
    blk = lambda f, c: pl.BlockSpec((BLOCK, BLOCK), lambda n: (f(n), c))
    return [
        pl.BlockSpec((BLOCK, SWA_WIDTH), lambda n: (n, 0)),
        blk(prev, CB_SWK), blk(cur, CB_SWK), blk(prev, CB_SWV), blk(cur, CB_SWV),
        blk(prev, 0), blk(cur, 0), blk(prev, 0), blk(cur, 0),
        pl.BlockSpec(memory_space=pltpu.SMEM),
    ]


def _swa_probs(n, q_ref, kp_ref, kc_ref, vp_ref, vc_ref, cp_ref, cc_ref, sp_ref, sc_ref, sink_ref):
    lane = lax.broadcasted_iota(jnp.int32, (1, BLOCK), 1)
    halves = (lane < HEAD_DIM, lane >= HEAD_DIM)
    cosc, sinc = cc_ref[...], sc_ref[...]
    qs = [_rope(q_ref[:, p * BLOCK:(p + 1) * BLOCK], cosc, sinc) * SCALE for p in range(SWA_PAIRS)]
    kb = jnp.concatenate([_rope(kp_ref[...], cp_ref[...], sp_ref[...]), _rope(kc_ref[...], cosc, sinc)], axis=0)
    vb = jnp.concatenate([vp_ref[...], vc_ref[...]], axis=0)
    kv = {True: (kb, vb), False: (pltpu.roll(kb, HEAD_DIM, axis=1), pltpu.roll(vb, HEAD_DIM, axis=1))}
    rows = PAIRS_PER_KV * BLOCK
    r = lax.broadcasted_iota(jnp.int32, (rows, 2 * BLOCK), 0) % BLOCK
    c = lax.broadcasted_iota(jnp.int32, (rows, 2 * BLOCK), 1)
    valid = (c > r) & (c <= r + BLOCK) & ((n - 1) * BLOCK + c >= PAD)
    combos = [(g, a) for g in range(SWA_KV_HEADS) for a in range(2)]
    qst, ksel, vsel, scores = {}, {}, {}, {}
    for g, a in combos:
        qst[g, a] = jnp.concatenate(
            [jnp.where(halves[a], qs[g * PAIRS_PER_KV + j], 0.0) for j in range(PAIRS_PER_KV)], axis=0).astype(BF16)
        ksel[g, a], vsel[g, a] = kv[g == a]
    for g, a in combos:
        scores[g, a] = _dot_nt(qst[g, a], ksel[g, a].astype(BF16))
    out = {}
    for g, a in combos:
        s = jnp.where(valid, scores[g, a], -1e30)
        sink = jnp.concatenate([jnp.full((BLOCK, BLOCK), sink_ref[0, 2 * (g * PAIRS_PER_KV + j) + a], F32)
                                for j in range(PAIRS_PER_KV)], axis=0)
        mx = jnp.maximum(jnp.max(s, axis=1, keepdims=True), sink)
        pe = jnp.exp(s - _twice(mx))
        es = jnp.exp(sink - mx)
        inv = 1.0 / (_row_sums(pe) + es)
        out[g, a] = (qst[g, a], ksel[g, a], vsel[g, a], pe * _twice(inv), es * inv, halves[a])
    return combos, out


def _twice(x):
    return jnp.concatenate([x, x], axis=1)


def _row_sums(x):
    return _dot(_split_bf16(x), jnp.ones((4 * BLOCK, BLOCK), BF16))


def swa_attention_fwd(proj, cos, sin, sinks):
    L = proj.shape[0]

    def body(q_ref, kp_ref, kc_ref, vp_ref, vc_ref, cp_ref, cc_ref, sp_ref, sc_ref, sink_ref, o_ref):
        n = pl.program_id(0)
        combos, parts = _swa_probs(n, q_ref, kp_ref, kc_ref, vp_ref, vc_ref, cp_ref, cc_ref, sp_ref, sc_ref, sink_ref)
        outs = {}
        for g, a in combos:
            qst, ksel, vsel, probs, psink, half = parts[g, a]
            outs[g, a] = _dot(probs.astype(BF16), jnp.where(half, vsel, 0.0).astype(BF16))
        for g in range(SWA_KV_HEADS):
            both = outs[g, 0] + outs[g, 1]
            for j in range(PAIRS_PER_KV):
                p = g * PAIRS_PER_KV + j
                o_ref[:, p * BLOCK:(p + 1) * BLOCK] = both[j * BLOCK:(j + 1) * BLOCK]

    return pl.pallas_call(
        body,
        name="swa_attention_fwd",
        grid=(L // BLOCK,),
        in_specs=_swa_specs(),
        out_specs=pl.BlockSpec((BLOCK, SWA_WIDTH), lambda n: (n, 0)),
        out_shape=jax.ShapeDtypeStruct((L, SWA_WIDTH), F32),
        compiler_params=_params(("arbitrary",)),
    )(proj, proj, proj, proj, proj, cos, cos, sin, sin, sinks)


def swa_attention_bwd(proj, cos, sin, sinks, do_sw):
    L = proj.shape[0]

    def body(q_ref, kp_ref, kc_ref, vp_ref, vc_ref, cp_ref, cc_ref, sp_ref, sc_ref, sink_ref, do_ref,
             dq_ref, dk_ref, dv_ref, ds_ref):
        n = pl.program_id(0)

        @pl.when(n == 0)
        def _():
            dk_ref[...] = jnp.zeros_like(dk_ref)
            dv_ref[...] = jnp.zeros_like(dv_ref)
            ds_ref[...] = jnp.zeros_like(ds_ref)

        combos, parts = _swa_probs(n, q_ref, kp_ref, kc_ref, vp_ref, vc_ref, cp_ref, cc_ref, sp_ref, sc_ref, sink_ref)
        lane8 = lax.broadcasted_iota(jnp.int32, (8, BLOCK), 1)
        dos, dps = {}, {}
        for g, a in combos:
            half = parts[g, a][5]
            dos[g, a] = jnp.concatenate(
                [jnp.where(half, do_ref[:, (g * PAIRS_PER_KV + j) * BLOCK:(g * PAIRS_PER_KV + j + 1) * BLOCK], 0.0)
                 for j in range(PAIRS_PER_KV)], axis=0).astype(BF16)
        for g, a in combos:
            dps[g, a] = _dot_nt(dos[g, a], parts[g, a][2].astype(BF16))
        dqs = {}
        dkb = jnp.zeros((2 * BLOCK, BLOCK), F32)
        dvb = jnp.zeros((2 * BLOCK, BLOCK), F32)
        dsk = jnp.zeros((8, BLOCK), F32)
        for g, a in combos:
            qst, ksel, vsel, probs, psink, half = parts[g, a]
            dp = dps[g, a]
            delta = _row_sums(probs * dp)
            ds = (probs * (dp - _twice(delta))).astype(BF16)
            pd = psink * delta
            for j in range(PAIRS_PER_KV):
                head = 2 * (g * PAIRS_PER_KV + j) + a
                dsk = dsk + jnp.where(lane8 == head, -jnp.sum(pd[j * BLOCK:(j + 1) * BLOCK, :1]), 0.0)
            dqs[g, a] = _dot(ds, jnp.where(half, ksel, 0.0).astype(BF16))
            dk_a = _dot_tn(ds, qst)
            dv_a = _dot_tn(probs.astype(BF16), dos[g, a])
            if g != a:
                dk_a = pltpu.roll(dk_a, HEAD_DIM, axis=1)
                dv_a = pltpu.roll(dv_a, HEAD_DIM, axis=1)
            dkb = dkb + dk_a
            dvb = dvb + dv_a
        cosc, sinc = cc_ref[...], sc_ref[...]
        for g in range(SWA_KV_HEADS):
            both = (dqs[g, 0] + dqs[g, 1]) * SCALE
            for j in range(PAIRS_PER_KV):
                p = g * PAIRS_PER_KV + j
                dq_ref[:, p * BLOCK:(p + 1) * BLOCK] = _unrope(both[j * BLOCK:(j + 1) * BLOCK], cosc, sinc).astype(BF16)
        ds_ref[...] += dsk
        cur = pl.ds(pl.multiple_of(n * BLOCK, BLOCK), BLOCK)
        dk_ref[cur, :] += _unrope(dkb[BLOCK:], cosc, sinc)
        dv_ref[cur, :] += dvb[BLOCK:]

        @pl.when(n > 0)
        def _():
            prv = pl.ds(pl.multiple_of((n - 1) * BLOCK, BLOCK), BLOCK)
            dk_ref[prv, :] += _unrope(dkb[:BLOCK], cp_ref[...], sp_ref[...])
            dv_ref[prv, :] += dvb[:BLOCK]

    whole = lambda n: (0, 0)
    row = pl.BlockSpec((BLOCK, SWA_WIDTH), lambda n: (n, 0))
    return pl.pallas_call(
        body,
        name="swa_attention_bwd",
        grid=(L // BLOCK,),
        in_specs=_swa_specs() + [row],
        out_specs=[row, pl.BlockSpec((L, BLOCK), whole), pl.BlockSpec((L, BLOCK), whole),
                   pl.BlockSpec((8, BLOCK), whole)],
        out_shape=[jax.ShapeDtypeStruct((L, SWA_WIDTH), BF16), jax.ShapeDtypeStruct((L, BLOCK), F32),
                   jax.ShapeDtypeStruct((L, BLOCK), F32), jax.ShapeDtypeStruct((8, BLOCK), F32)],
        compiler_params=_params(("arbitrary",)),
    )(proj, proj, proj, proj, proj, cos, cos, sin, sin, sinks, do_sw)


ROW_TILE = 640
TAIL_ROWS = 208


def _pick(n, cands):
    for c in cands:
        if n % c == 0:
            return c
    raise ValueError(f"no tile for {n}")


def in_proj(h0, gain, w, name, out_dtype):
    L, D = h0.shape
    N = w.shape[1]
    tm = _pick(L, (ROW_TILE, BLOCK))
    tn = _pick(N, (1792, 1536, 1280, 896, 640, 512, 384, 256, 128))

    def body(h_ref, g_ref, w_ref, o_ref, xn_ref):
        @pl.when(pl.program_id(1) == 0)
        def _():
            x = h_ref[...]
            r = lax.rsqrt(jnp.mean(x * x, axis=1, keepdims=True) + RMS_EPS)
            xn_ref[...] = ((x * r) * g_ref[...]).astype(BF16)
        o_ref[...] = _dot(xn_ref[...], w_ref[...]).astype(out_dtype)

    return pl.pallas_call(
        body,
        name=name,
        grid=(L // tm, N // tn),
        in_specs=[pl.BlockSpec((tm, D), lambda i, j: (i, 0)),
                  pl.BlockSpec((1, D), lambda i, j: (0, 0)),
                  pl.BlockSpec((D, tn), lambda i, j: (0, j))],
        out_specs=[pl.BlockSpec((tm, tn), lambda i, j: (i, j)),
                   pl.BlockSpec((tm, D), lambda i, j: (i, 0))],
        out_shape=[jax.ShapeDtypeStruct((L, N), out_dtype), jax.ShapeDtypeStruct((L, D), BF16)],
        compiler_params=_params(("arbitrary", "arbitrary")),
    )(h0, gain, w)


def matmul_tn(a, b, name):
    Kd, M = a.shape
    N = b.shape[1]
    tk = _pick(Kd, (ROW_TILE, BLOCK))
    tn = _pick(N, (1280, 1024, 896, 640, 512, 256, 128))
    nk = Kd // tk

    def body(a_ref, b_ref, o_ref):
        k = pl.program_id(1)

        @pl.when(k == 0)
        def _():
            o_ref[...] = jnp.zeros_like(o_ref)
        o_ref[...] += _dot_tn(a_ref[...], b_ref[...])

    return pl.pallas_call(
        body,
        name=name,
        grid=(N // tn, nk),
        in_specs=[pl.BlockSpec((tk, M), lambda j, k: (k, 0)),
                  pl.BlockSpec((tk, tn), lambda j, k: (k, j))],
        out_specs=pl.BlockSpec((M, tn), lambda j, k: (0, j)),
        out_shape=jax.ShapeDtypeStruct((M, N), F32),
        compiler_params=_params(("arbitrary", "arbitrary")),
    )(a, b)


def in_proj_bwd(dproj, w, h0, gain, dh1):
    L, N = dproj.shape
    D = w.shape[0]
    tm = _pick(L, (ROW_TILE, BLOCK))
    tk = _pick(N, (1280, 640, 512, 256, 128))
    nk = N // tk

    def body(dp_ref, w_ref, h_ref, g_ref, dh1_ref, dh0_ref, dg_ref, acc_ref):
        i, k = pl.program_id(0), pl.program_id(1)

        @pl.when(k == 0)
        def _():
            acc_ref[...] = jnp.zeros_like(acc_ref)

        @pl.when((i == 0) & (k == 0))
        def _():
            dg_ref[...] = jnp.zeros_like(dg_ref)

        acc_ref[...] += _dot_nt(dp_ref[...], w_ref[...])

        @pl.when(k == nk - 1)
        def _():
            x = h_ref[...]
            r = lax.rsqrt(jnp.mean(x * x, axis=1, keepdims=True) + RMS_EPS)
            xhat = x * r
            dxn = acc_ref[...]
            dg_ref[...] += jnp.sum(dxn * xhat, axis=0, keepdims=True)
            dxh = dxn * g_ref[...]
            dh0_ref[...] = r * (dxh - xhat * jnp.mean(dxh * xhat, axis=1, keepdims=True)) + dh1_ref[...]

    row = pl.BlockSpec((tm, D), lambda i, k: (i, 0))
    vec = pl.BlockSpec((1, D), lambda i, k: (0, 0))
    return pl.pallas_call(
        body,
        name="in_proj_bwd",
        grid=(L // tm, nk),
        in_specs=[pl.BlockSpec((tm, tk), lambda i, k: (i, k)),
                  pl.BlockSpec((D, tk), lambda i, k: (0, k)),
                  row, vec, row],
        out_specs=[row, vec],
        out_shape=[jax.ShapeDtypeStruct((L, D), F32), jax.ShapeDtypeStruct((1, D), F32)],
        scratch_shapes=[pltpu.VMEM((tm, D), F32)],
        compiler_params=_params(("arbitrary", "arbitrary")),
    )(dproj, w, h0, gain, dh1)


def tail_fwd_bwd(h0, tgt, o_sb, o_sw, gates, w_bsb, w_bswa, w_out, gain_f):
    L, D = h0.shape
    R = _pick(L, (TAIL_ROWS, BLOCK))
    z0, z1, z2, z3 = 0, SB_WIDTH, SB_WIDTH + SWA_WIDTH, SB_WIDTH + SWA_WIDTH + D_MODEL

    def body(h_ref, t_ref, osb_ref, osw_ref, g_ref, wsb_ref, wsw_ref, wo_ref, gf_ref,
             dosb_ref, dosw_ref, dg_ref, dh1_ref, mb_ref, usb_ref, usw_ref, dh1b_ref, dysb_ref, dysw_ref,
             dgf_ref, loss_ref):
        i = pl.program_id(0)

        @pl.when(i == 0)
        def _():
            dgf_ref[...] = jnp.zeros_like(dgf_ref)
            loss_ref[...] = jnp.zeros_like(loss_ref)

        sbz = g_ref[:, z0:z1]
        swz = g_ref[:, z1:z2]
        s1 = jax.nn.sigmoid(g_ref[:, z2:z3])
        s2 = jax.nn.sigmoid(g_ref[:, z3:])
        sg_sb = jax.nn.sigmoid(sbz)
        sg_sw = jax.nn.sigmoid(swz)
        silu_sb = sbz * sg_sb
        silu_sw = swz * sg_sw
        osb = osb_ref[...]
        osw = osw_ref[...]
        usb = (osb * silu_sb).astype(BF16)
        usw = (osw * silu_sw).astype(BF16)
        y_sb = _dot(usb, wsb_ref[...])
        y_sw = _dot(usw, wsw_ref[...])
        mb = (s1 * y_sb + s2 * y_sw).astype(BF16)
        h1 = h_ref[...] + _dot(mb, wo_ref[...])
        rf = lax.rsqrt(jnp.mean(h1 * h1, axis=1, keepdims=True) + RMS_EPS)
        hhat = h1 * rf
        gf = gf_ref[...]
        row = i * R + lax.broadcasted_iota(jnp.int32, (R, 1), 0)
        err = jnp.where(row >= BLOCK, hhat * gf - t_ref[...], 0.0)
        lane0 = (lax.broadcasted_iota(jnp.int32, (8, BLOCK), 0) == 0) & (lax.broadcasted_iota(jnp.int32, (8, BLOCK), 1) == 0)
        loss_ref[...] += jnp.where(lane0, (0.5 / D) * jnp.sum(err * err), 0.0)
        dy = err * (1.0 / D)
        dgf_ref[...] += jnp.sum(dy * hhat, axis=0, keepdims=True)
        dhh = dy * gf
        dh1 = rf * (dhh - hhat * jnp.mean(dhh * hhat, axis=1, keepdims=True))
        dh1b = dh1.astype(BF16)
        dm = _dot_nt(dh1b, wo_ref[...])
        dysb = (dm * s1).astype(BF16)
        dysw = (dm * s2).astype(BF16)
        dusb = _dot_nt(dysb, wsb_ref[...])
        dusw = _dot_nt(dysw, wsw_ref[...])
        dosb_ref[...] = (dusb * silu_sb).astype(BF16)
        dosw_ref[...] = (dusw * silu_sw).astype(BF16)
        dg_ref[:, z0:z1] = (dusb * osb * (sg_sb * (1.0 + sbz * (1.0 - sg_sb)))).astype(BF16)
        dg_ref[:, z1:z2] = (dusw * osw * (sg_sw * (1.0 + swz * (1.0 - sg_sw)))).astype(BF16)
        dg_ref[:, z2:z3] = (dm * y_sb * (s1 * (1.0 - s1))).astype(BF16)
        dg_ref[:, z3:] = (dm * y_sw * (s2 * (1.0 - s2))).astype(BF16)
        dh1_ref[...] = dh1
        mb_ref[...] = mb
        usb_ref[...] = usb
        usw_ref[...] = usw
        dh1b_ref[...] = dh1b
        dysb_ref[...] = dysb
        dysw_ref[...] = dysw

    def rows(n):
        return pl.BlockSpec((R, n), lambda i: (i, 0))

    def whole(shape):
        return pl.BlockSpec(shape, lambda i: (0, 0))

    GW = gates.shape[1]
    return pl.pallas_call(
        body,
        name="tail_fwd_bwd",
        grid=(L // R,),
        in_specs=[rows(D), rows(D), rows(SB_WIDTH), rows(SWA_WIDTH), rows(GW),
                  whole(w_bsb.shape), whole(w_bswa.shape), whole(w_out.shape), whole((1, D))],
        out_specs=[rows(SB_WIDTH), rows(SWA_WIDTH), rows(GW), rows(D),
                   rows(D), rows(SB_WIDTH), rows(SWA_WIDTH), rows(D), rows(D), rows(D),
                   whole((1, D)), whole((8, BLOCK))],
        out_shape=[jax.ShapeDtypeStruct((L, SB_WIDTH), BF16), jax.ShapeDtypeStruct((L, SWA_WIDTH), BF16),
                   jax.ShapeDtypeStruct((L, GW), BF16), jax.ShapeDtypeStruct((L, D), F32),
                   jax.ShapeDtypeStruct((L, D), BF16), jax.ShapeDtypeStruct((L, SB_WIDTH), BF16),
                   jax.ShapeDtypeStruct((L, SWA_WIDTH), BF16), jax.ShapeDtypeStruct((L, D), BF16),
                   jax.ShapeDtypeStruct((L, D), BF16), jax.ShapeDtypeStruct((L, D), BF16),
                   jax.ShapeDtypeStruct((1, D), F32), jax.ShapeDtypeStruct((8, BLOCK), F32)],
        compiler_params=_params(("arbitrary",)),
    )(h0, tgt, o_sb, o_sw, gates, w_bsb, w_bswa, w_out, gain_f)


def local_step(x, tgt, meta, gain, w_in, w_bsb, w_bswa, w_out, sinks, gain_f):
    S, D = x.shape
    L = S + BLOCK
    h0 = jnp.concatenate([jnp.zeros((PAD, D), F32), meta, x], axis=0)
    tgt_p = jnp.concatenate([jnp.zeros((BLOCK, D), F32), tgt], axis=0)
    tt = _suffix_matrix()
    cos, sin = rope_tables(L)
    qkv, xn = in_proj(h0, gain, w_in[:, :SB_COLS], "in_proj_sb", BF16)
    proj_sw, _ = in_proj(h0, gain, w_in[:, SB_COLS:GATE_COL0], "in_proj_swa", F32)
    gates, _ = in_proj(h0, gain, w_in[:, GATE_COL0:], "in_proj_gates", F32)
    o_sb = sb_attention_fwd(qkv, tt)
    o_sw = swa_attention_fwd(proj_sw, cos, sin, sinks)
    (do_sb, do_sw, dgates, dh1, mb, usb, usw, dh1b, dysb, dysw, dgf, loss) = tail_fwd_bwd(
        h0, tgt_p, o_sb, o_sw, gates, w_bsb, w_bswa, w_out, gain_f)
    dq_sb, dk_sb, dv_sb = sb_attention_bwd(qkv, o_sb, do_sb, tt)
    dq_sw, dk_sw, dv_sw, dsinks = swa_attention_bwd(proj_sw, cos, sin, sinks, do_sw)
    dproj = jnp.concatenate([dq_sb, dk_sb, dv_sb, dq_sw, dk_sw.astype(BF16), dv_sw.astype(BF16), dgates], axis=1)
    dw_in = matmul_tn(xn, dproj, "dw_in")
    dw_out = matmul_tn(mb, dh1b, "dw_out")
    dw_bsb = matmul_tn(usb, dysb, "dw_bsb")
    dw_bswa = matmul_tn(usw, dysw, "dw_bswa")
    dh0, dgain = in_proj_bwd(dproj, w_in, h0, gain, dh1)
    return (loss[0, 0], dh0[BLOCK:], dh0[PAD:BLOCK], dgain, dw_in, dw_bsb, dw_bswa, dw_out,
            dsinks[:1, :SWA_Q_HEADS], dgf)


MESH_IDS = pl.DeviceIdType.MESH
ANY = pl.BlockSpec(memory_space=pl.ANY)


def _place():
    return lax.axis_index("x"), lax.axis_index("y"), lax.axis_index("c")


def _index(x, y, c):
    return 4 * x + 2 * y + c


def all_gather(block, name):
    def body(x_ref, out_ref, send_sems, recv_sems, local_sem):
        x, y, c = _place()
        me, sibling = (x, y, c), (x, y, 1 - c)
        chips = [(1 - x, y), (x, 1 - y), (1 - x, 1 - y)]

        def copy(k, blk, to, src=None):
            dst = out_ref.at[_index(*blk)]
            return pltpu.make_async_remote_copy(
                src_ref=dst if src is None else src, dst_ref=dst,
                send_sem=send_sems.at[k], recv_sem=recv_sems.at[k], device_id=to, device_id_type=MESH_IDS)

        mine = pltpu.make_async_copy(x_ref, out_ref.at[_index(*me)], local_sem)
        mine.start()
        first = [copy(0, me, sibling, src=x_ref)]
        first += [copy(1 + j, me, (*chip, c), src=x_ref) for j, chip in enumerate(chips)]
        for cp in first:
            cp.start()
        passed = [copy(4 + j, (*chip, c), sibling) for j, chip in enumerate(chips)]
        for j, chip in enumerate(chips):
            copy(1 + j, (*chip, c), me).wait_recv()
            passed[j].start()
        copy(0, sibling, me).wait_recv()
        for j, chip in enumerate(chips):
            copy(4 + j, (*chip, 1 - c), me).wait_recv()
        for cp in first + passed:
            cp.wait_send()
        mine.wait()

    return pl.pallas_call(
        body,
        name=name,
        out_shape=jax.ShapeDtypeStruct((N_DEV,) + block.shape, block.dtype),
        in_specs=[ANY],
        out_specs=ANY,
        scratch_shapes=[pltpu.SemaphoreType.DMA((7,)), pltpu.SemaphoreType.DMA((7,)), pltpu.SemaphoreType.DMA],
    )(block)


def exchange_partials(parts):
    def body(g_ref, out_ref, send_sems, recv_sems, local_sem):
        x, y, c = _place()
        me = _index(x, y, c)
        mine = pltpu.make_async_copy(g_ref.at[me], out_ref.at[me], local_sem)
        mine.start()
        copies = []
        for m in range(1, N_DEV):
            px = 1 - x if m & 4 else x
            py = 1 - y if m & 2 else y
            pc = 1 - c if m & 1 else c
            cp = pltpu.make_async_remote_copy(
                src_ref=g_ref.at[_index(px, py, pc)], dst_ref=out_ref.at[me],
                send_sem=send_sems.at[m - 1], recv_sem=recv_sems.at[m - 1],
                device_id=(px, py, pc), device_id_type=MESH_IDS)
            cp.start()
            copies.append(cp)
        for cp in copies:
            cp.wait()
        mine.wait()

    return pl.pallas_call(
        body,
        name="exchange_partials",
        out_shape=jax.ShapeDtypeStruct(parts.shape, parts.dtype),
        in_specs=[ANY],
        out_specs=ANY,
        scratch_shapes=[pltpu.SemaphoreType.DMA((7,)), pltpu.SemaphoreType.DMA((7,)), pltpu.SemaphoreType.DMA],
    )(parts)


def _adamw(w, g, m, v):
    m = ADAM_B1 * m + (1.0 - ADAM_B1) * g
    v = ADAM_B2 * v + (1.0 - ADAM_B2) * (g * g)
    m_hat = m / (1.0 - ADAM_B1 ** ADAM_STEP)
    v_hat = v / (1.0 - ADAM_B2 ** ADAM_STEP)
    delta = -ADAM_LR * (m_hat / (jnp.sqrt(v_hat) + ADAM_EPS) + ADAM_WD * w)
    return delta, m, v


def sum_and_adamw(parts, w, m, v, name):
    _, R, C = parts.shape
    tr = _pick(R, (528, 512, 256, 128, 24, 8))

    def body(p_ref, w_ref, m_ref, v_ref, g_ref, d_ref, nm_ref, nv_ref):
        g = p_ref[0].astype(F32)
        for s in range(1, N_DEV):
            g = g + p_ref[s].astype(F32)
        d, nm, nv = _adamw(w_ref[...], g, m_ref[...], v_ref[...])
        g_ref[...] = g
        d_ref[...] = d
        nm_ref[...] = nm
        nv_ref[...] = nv

    row = pl.BlockSpec((tr, C), lambda i: (i, 0))
    return pl.pallas_call(
        body,
        name=name,
        grid=(R // tr,),
        in_specs=[pl.BlockSpec((N_DEV, tr, C), lambda i: (0, i, 0)), row, row, row],
        out_specs=[row, row, row, row],
        out_shape=[jax.ShapeDtypeStruct((R, C), F32)] * 4,
        compiler_params=_params(("arbitrary",)),
    )(parts, w, m, v)


W_IN_SHARD = IN_COLS // N_DEV
ROWS_W_IN = D_MODEL * W_IN_SHARD // BLOCK
ROWS_W_BSB = SB_WIDTH
ROWS_W_ROWSHARD = D_MODEL
SMALL_ROWS = 24


def _pack_shards(w_in, w_bsb, w_bswa, w_out, meta):
    return jnp.concatenate([w_in.reshape(ROWS_W_IN, BLOCK), w_bsb.reshape(ROWS_W_BSB, BLOCK),
                            w_bswa.reshape(ROWS_W_ROWSHARD, BLOCK), w_out.reshape(ROWS_W_ROWSHARD, BLOCK),
                            meta.reshape(N_META, BLOCK)], axis=0)


def _unpack_shards(p):
    o = np.cumsum([0, ROWS_W_IN, ROWS_W_BSB, ROWS_W_ROWSHARD, ROWS_W_ROWSHARD, N_META])
    return (p[o[0]:o[1]].reshape(1, D_MODEL, W_IN_SHARD), p[o[1]:o[2]].reshape(1, SB_WIDTH, BLOCK),
            p[o[2]:o[3]].reshape(1, BLOCK, D_MODEL), p[o[3]:o[4]].reshape(1, BLOCK, D_MODEL),
            p[o[4]:o[5]].reshape(N_META, BLOCK))


def _pack_by_owner(dw_in, dw_bsb, dw_bswa, dw_out, dmeta):
    cols = lambda a: a.reshape(a.shape[0], N_DEV, -1).transpose(1, 0, 2)
    return jnp.concatenate([cols(dw_in).reshape(N_DEV, ROWS_W_IN, BLOCK), cols(dw_bsb),
                            dw_bswa.reshape(N_DEV, ROWS_W_ROWSHARD, BLOCK), dw_out.reshape(N_DEV, ROWS_W_ROWSHARD, BLOCK),
                            cols(dmeta)], axis=1)


def _pack_small(gain, gain_f, sinks, loss):
    z = jnp.zeros((SMALL_ROWS - 16, BLOCK), F32)
    z = z.at[0, :SWA_Q_HEADS].set(sinks.reshape(-1)).at[1, 0].set(loss)
    return jnp.concatenate([gain.reshape(8, BLOCK), gain_f.reshape(8, BLOCK), z], axis=0)


def _unpack_small(p):
    return p[0:8].reshape(1, D_MODEL), p[8:16].reshape(D_MODEL), p[16:17, :SWA_Q_HEADS], p[17, 0]


def kernel(x, meta_tokens, norm_gain, w_in, w_branch_sb, w_branch_swa, w_out, attn_sinks, final_norm_gain, loss_target, m_meta_tokens, m_norm_gain, m_w_in, m_w_branch_sb, m_w_branch_swa, m_w_out, m_attn_sinks, m_final_norm_gain, v_meta_tokens, v_norm_gain, v_w_in, v_w_branch_sb, v_w_branch_swa, v_w_out, v_attn_sinks, v_final_norm_gain):
    meta_bits = lax.bitcast_convert_type(meta_tokens, BF16).reshape(2 * N_META, BLOCK)
    mine = jnp.concatenate([_pack_shards(w_in, w_branch_sb, w_branch_swa, w_out, meta_tokens)[:-N_META].astype(BF16),
                            meta_bits], axis=0)
    full = all_gather(mine, "all_gather_weights")
    o = np.cumsum([0, ROWS_W_IN, ROWS_W_BSB, ROWS_W_ROWSHARD, ROWS_W_ROWSHARD, 2 * N_META])
    cols = lambda a: a.transpose(1, 0, 2).reshape(a.shape[1], -1)
    f_w_in = cols(full[:, o[0]:o[1]].reshape(N_DEV, D_MODEL, W_IN_SHARD))
    f_w_bsb = cols(full[:, o[1]:o[2]])
    f_w_bswa = full[:, o[2]:o[3]].reshape(D_MODEL, D_MODEL)
    f_w_out = full[:, o[3]:o[4]].reshape(D_MODEL, D_MODEL)
    f_meta = cols(lax.bitcast_convert_type(full[:, o[4]:o[5]].reshape(N_DEV, N_META, BLOCK, 2), F32))

    (loss, grad_x, dmeta, dgain, dw_in, dw_bsb, dw_bswa, dw_out, dsinks, dgf) = local_step(
        x[0], loss_target[0], f_meta, norm_gain, f_w_in, f_w_bsb, f_w_bswa, f_w_out, attn_sinks,
        final_norm_gain.reshape(1, D_MODEL))

    parts = exchange_partials(_pack_by_owner(dw_in, dw_bsb, dw_bswa, dw_out, dmeta).astype(BF16))
    packs = [_pack_shards(a[0], b[0], c[0], d[0], e) for a, b, c, d, e in (
        (w_in, w_branch_sb, w_branch_swa, w_out, meta_tokens),
        (m_w_in, m_w_branch_sb, m_w_branch_swa, m_w_out, m_meta_tokens),
        (v_w_in, v_w_branch_sb, v_w_branch_swa, v_w_out, v_meta_tokens))]
    big = [_unpack_shards(p) for p in sum_and_adamw(parts, *packs, "sum_adamw_sharded")]

    small = all_gather(_pack_small(dgain, dgf, dsinks, loss), "all_gather_small")
    zero = jnp.zeros((), F32)
    spacks = [_pack_small(a, b, c, zero) for a, b, c in (
        (norm_gain, final_norm_gain, attn_sinks), (m_norm_gain, m_final_norm_gain, m_attn_sinks),
        (v_norm_gain, v_final_norm_gain, v_attn_sinks))]
    sm = [_unpack_small(p) for p in sum_and_adamw(small, *spacks, "sum_adamw_replicated")]

    def leaves(k):
        b, s = big[k], sm[k]
        return (b[4], s[0], b[0], b[1], b[2], b[3], s[2], s[1])

    return (sm[0][3], grad_x[None], *leaves(0), *leaves(1), *leaves(2), *leaves(3))
```

```python
import numpy as np
import jax
import jax.numpy as jnp
from jax import lax
from jax.experimental import pallas as pl
from jax.experimental.pallas import tpu as pltpu

F32 = jnp.float32
BF16 = jnp.bfloat16

D_MODEL = 1024
N_META = 16
BLOCK = 128
PAD = BLOCK - N_META
HEAD_DIM = 64
SB_HEADS = 8
SB_WIDTH = SB_HEADS * HEAD_DIM
SWA_Q_HEADS = 16
SWA_KV_HEADS = 2
SWA_WIDTH = SWA_Q_HEADS * HEAD_DIM
SWA_KV_WIDTH = SWA_KV_HEADS * HEAD_DIM
ROPE_THETA = 10000.0
RMS_EPS = 1e-6
SCALE = HEAD_DIM ** -0.5
SPLITS = (SB_WIDTH, SB_WIDTH, SB_WIDTH, SWA_WIDTH, SWA_KV_WIDTH, SWA_KV_WIDTH,
          SB_WIDTH, SWA_WIDTH, D_MODEL, D_MODEL)
IN_COLS = sum(SPLITS)
SB_COLS = 3 * SB_WIDTH
SWA_COLS = SWA_WIDTH + 2 * SWA_KV_WIDTH
GATE_COL0 = SB_COLS + SWA_COLS

N_DEV = 8
ADAM_LR = 0.001
ADAM_B1 = 0.9
ADAM_B2 = 0.999
ADAM_EPS = 1e-08
ADAM_WD = 0.01
ADAM_STEP = 10

VMEM_LIMIT = 56 * 1024 * 1024


def _params(sem, **kw):
    return pltpu.CompilerParams(dimension_semantics=sem, vmem_limit_bytes=VMEM_LIMIT, **kw)


def _dot(a, b):
    return jnp.dot(a, b, preferred_element_type=F32)


def _dot_nt(a, b):
    return lax.dot_general(a, b, (((1,), (1,)), ((), ())), preferred_element_type=F32)


def _dot_tn(a, b):
    return lax.dot_general(a, b, (((0,), (0,)), ((), ())), preferred_element_type=F32)


def _cat(xs, axis):
    return xs[0] if len(xs) == 1 else jnp.concatenate(xs, axis=axis)


def _split_bf16(x):
    hi = x.astype(BF16)
    lo = (x - hi.astype(F32)).astype(BF16)
    return jnp.concatenate([hi, lo], axis=1)


def _suffix_matrix():
    j = np.arange(BLOCK)[:, None]
    s = np.arange(BLOCK)[None, :]
    t = np.concatenate([(j >= s).astype(np.float32), np.ones((BLOCK, BLOCK), np.float32)], axis=1)
    return jnp.asarray(np.concatenate([t, t], axis=0), dtype=BF16)


def _softplus(z):
    return jnp.maximum(z, 0.0) + jnp.log(1.0 + jnp.exp(-jnp.abs(z)))


SB_SMALL = 4
SB_PAIRS_FWD = 4
SB_PAIRS_BWD = 2


def _sb_masks(i, nblk):
    if nblk == 0:
        r = lax.broadcasted_iota(jnp.int32, (BLOCK, BLOCK), 0)
        c = lax.broadcasted_iota(jnp.int32, (BLOCK, BLOCK), 1)
        return (c < r) & (i * BLOCK + c >= PAD)
    return lax.broadcasted_iota(jnp.int32, (BLOCK, nblk * BLOCK), 1) >= PAD


def _sb_heads(npairs):
    return [(pr, a) for pr in range(npairs) for a in range(2)]


def _lanes(pr):
    return slice(pr * BLOCK, (pr + 1) * BLOCK)


def _masked_heads(x, half0, npairs):
    out = []
    for pr, a in _sb_heads(npairs):
        xp = x[:, _lanes(pr)]
        out.append((jnp.where(half0, xp, 0.0) if a == 0 else jnp.where(half0, 0.0, xp)).astype(BF16))
    return out


def _sb_sweep(i, tile, carry):
    same = lambda cr: cr
    small = SB_SMALL
    live_tile = tile
    tile = lambda j0, nblk, mask, cr: lax.cond(cr[0], lambda c: live_tile(j0, nblk, mask, c), same, cr)
    carry = live_tile(i, 1, _sb_masks(i, 0), carry)
    carry = lax.cond(i == 1, lambda cr: tile(0, 1, _sb_masks(i, 1), cr), same, carry)
    carry = lax.cond(i == 2, lambda cr: tile(0, 2, _sb_masks(i, 2), cr), same, carry)
    carry = lax.cond(i > 2, lambda cr: tile(i - 2, 2, None, cr), same, carry)
    rest = jnp.maximum(i - 2, 0)
    n_grp = rest // small
    n_one = rest - n_grp * small
    low_is_one = (n_grp == 0) & (n_one > 0)
    n_plain = n_one - jnp.where(low_is_one, 1, 0)
    carry = lax.fori_loop(0, n_plain, lambda t, cr: tile(rest - 1 - t, 1, None, cr), carry)
    carry = lax.cond(low_is_one, lambda cr: tile(0, 1, _sb_masks(i, 1), cr), same, carry)
    carry = lax.fori_loop(0, jnp.maximum(n_grp - 1, 0), lambda t, cr: tile((n_grp - 1 - t) * small, small, None, cr),
                          carry)
    return lax.cond(n_grp > 0, lambda cr: tile(0, small, _sb_masks(i, small), cr), same, carry)


def _sb_weights(zs, ss, cs, tt, nblk, mask):
    suf = _dot(_cat([_split_bf16(s[:, b * BLOCK:(b + 1) * BLOCK]) for s in ss for b in range(nblk)], 0), tt)
    ws, out_cs = [], []
    for h in range(len(zs)):
        c = cs[h]
        wb = [None] * nblk
        for b in reversed(range(nblk)):
            sab = suf[(h * nblk + b) * BLOCK:(h * nblk + b + 1) * BLOCK]
            wb[b] = jnp.exp(zs[h][:, b * BLOCK:(b + 1) * BLOCK] + c - sab[:, :BLOCK])
            c = c - sab[:, BLOCK:]
        w = _cat(wb, 1)
        if mask is not None:
            w = jnp.where(mask, w, 0.0)
        ws.append(w)
        out_cs.append(c)
    return ws, out_cs


SB_DEAD = -110.0


def _row_norm_max(x, half0, a):
    xf = x.astype(F32)
    sq = jnp.where(half0, xf * xf, 0.0) if a == 0 else jnp.where(half0, 0.0, xf * xf)
    return jnp.sqrt(jnp.max(jnp.sum(sq, axis=1, keepdims=True)))


def _sb_logit_bounds(i, q_heads, k_ref, kmax_ref, half0, heads):
    @pl.when(i == 0)
    def _():
        for h, (pr, a) in enumerate(heads):
            kmax_ref[h] = _row_norm_max(k_ref[:, _lanes(pr)], half0, a)
    return [_row_norm_max(q_heads[h], half0, a) * kmax_ref[h] * 1.001 + 0.01 for h, (pr, a) in enumerate(heads)]


def _sb_alive(cs, zmax):
    worst = cs[0] + zmax[0]
    for c, zm in zip(cs[1:], zmax[1:]):
        worst = jnp.maximum(worst, c + zm)
    return jnp.max(worst) > SB_DEAD


def sb_attention_fwd(qkv, tt):
    L = qkv.shape[0]
    nb = L // BLOCK
    NP = SB_PAIRS_FWD
    SB_W = NP * BLOCK
    nq = SB_WIDTH // SB_W

    def body(q_ref, k_ref, v_ref, tt_ref, o_ref, kmax_ref):
        i = pl.program_id(1)
        half0 = lax.broadcasted_iota(jnp.int32, (1, BLOCK), 1) < HEAD_DIM
        heads = _sb_heads(NP)
        qh = _masked_heads(q_ref[...].astype(F32) * SCALE, half0, NP)
        zmax = _sb_logit_bounds(i, qh, k_ref, kmax_ref, half0, heads)

        def rows_of(j0, nblk):
            return pl.ds(pl.multiple_of(j0 * BLOCK, BLOCK), nblk * BLOCK)

        def tile(j0, nblk, mask, carry):
            _, accs, cs = carry
            rows = rows_of(j0, nblk)
            zs = [_dot_nt(qh[h], k_ref[rows, _lanes(pr)]) for h, (pr, a) in enumerate(heads)]
            ss = [_softplus(z) for z in zs]
            if mask is not None:
                ss = [jnp.where(mask, s, 0.0) for s in ss]
            ws, cs = _sb_weights(zs, ss, cs, tt_ref[...], nblk, mask)
            accs = [accs[h] + _dot(ws[h].astype(BF16), v_ref[rows, _lanes(pr)]) for h, (pr, a) in enumerate(heads)]
            return _sb_alive(cs, zmax), tuple(accs), tuple(cs)

        zero = (jnp.zeros((BLOCK, BLOCK), F32),) * len(heads)
        accs = _sb_sweep(i, tile, (jnp.bool_(True), zero, zero))[1]
        o_ref[...] = _cat([jnp.where(half0, accs[2 * pr], accs[2 * pr + 1]) for pr in range(NP)], 1)

    panel = lambda c: pl.BlockSpec((L, SB_W), lambda p, i: (0, c * nq + p), pipeline_mode=pl.Buffered(1))
    return pl.pallas_call(
        body,
        name="sb_attention_fwd",
        grid=(nq, nb),
        in_specs=[
            pl.BlockSpec((BLOCK, SB_W), lambda p, i: (i, p)),
            panel(1), panel(2),
            pl.BlockSpec((2 * BLOCK, 2 * BLOCK), lambda p, i: (0, 0)),
        ],
        out_specs=pl.BlockSpec((BLOCK, SB_W), lambda p, i: (i, p)),
        out_shape=jax.ShapeDtypeStruct((L, SB_WIDTH), F32),
        scratch_shapes=[pltpu.SMEM((2 * NP,), F32)],
        compiler_params=_params(("arbitrary", "arbitrary")),
    )(qkv, qkv, qkv, tt)


def sb_attention_bwd(qkv, o_sb, do_sb, tt):
    L = qkv.shape[0]
    nb = L // BLOCK
    NP = SB_PAIRS_BWD
    SB_W = NP * BLOCK
    nq = SB_WIDTH // SB_W

    def body(q_ref, k_ref, v_ref, o_ref, do_ref, tt_ref, dq_ref, dk_ref, dv_ref, dk_acc, dv_acc, kmax_ref):
        i = pl.program_id(1)
        half0 = lax.broadcasted_iota(jnp.int32, (1, BLOCK), 1) < HEAD_DIM
        heads = _sb_heads(NP)

        @pl.when(i == 0)
        def _():
            dk_acc[...] = jnp.zeros_like(dk_acc)
            dv_acc[...] = jnp.zeros_like(dv_acc)

        qh = _masked_heads(q_ref[...].astype(F32) * SCALE, half0, NP)
        zmax = _sb_logit_bounds(i, qh, k_ref, kmax_ref, half0, heads)
        do = do_ref[...]
        doh = _masked_heads(do, half0, NP)
        od = o_ref[...] * do.astype(F32)
        dsum = []
        for pr, a in heads:
            x = od[:, _lanes(pr)]
            x = jnp.where(half0, x, 0.0) if a == 0 else jnp.where(half0, 0.0, x)
            dsum.append(jnp.broadcast_to(jnp.sum(x, axis=1, keepdims=True), (BLOCK, BLOCK)))

        def rows_of(j0, nblk):
            return pl.ds(pl.multiple_of(j0 * BLOCK, BLOCK), nblk * BLOCK)

        def tile(j0, nblk, mask, carry):
            _, accs, cs, ces = carry
            rows = rows_of(j0, nblk)
            zs = [_dot_nt(qh[h], k_ref[rows, _lanes(pr)]) for h, (pr, a) in enumerate(heads)]
            dws = [_dot_nt(doh[h], v_ref[rows, _lanes(pr)]) for h, (pr, a) in enumerate(heads)]
            ss = [_softplus(z) for z in zs]
            sigs = [jnp.exp(z - s) for z, s in zip(zs, ss)]
            if mask is not None:
                ss = [jnp.where(mask, s, 0.0) for s in ss]
            ws, cs = _sb_weights(zs, ss, cs, tt_ref[...], nblk, mask)
            wbs = [w.astype(BF16) for w in ws]
            es = [wb.astype(F32) * dw for wb, dw in zip(wbs, dws)]
            esuf = _dot(_cat([_split_bf16(e[:, b * BLOCK:(b + 1) * BLOCK]) for e in es for b in range(nblk)], 0),
                        tt_ref[...])
            out_ces, dzbs = [], []
            for h, (pr, a) in enumerate(heads):
                ce = ces[h]
                dzs = [None] * nblk
                for b in reversed(range(nblk)):
                    eab = esuf[(h * nblk + b) * BLOCK:(h * nblk + b + 1) * BLOCK]
                    sl = slice(b * BLOCK, (b + 1) * BLOCK)
                    e = es[h][:, sl]
                    dzs[b] = e - sigs[h][:, sl] * (e + (ce - eab[:, :BLOCK]))
                    ce = ce - eab[:, BLOCK:]
                dz = _cat(dzs, 1)
                if mask is not None:
                    dz = jnp.where(mask, dz, 0.0)
                dzbs.append(dz.astype(BF16))
                out_ces.append(ce)
            accs = tuple(accs[h] + _dot(dzbs[h], k_ref[rows, _lanes(pr)]) for h, (pr, a) in enumerate(heads))
            dkv = [_dot_tn(_cat([dzbs[2 * pr], wbs[2 * pr], dzbs[2 * pr + 1], wbs[2 * pr + 1]], 0), qdo[pr])
                   for pr in range(NP)]
            dk_acc[rows, :] += _cat([x[:, :BLOCK] for x in dkv], 1)
            dv_acc[rows, :] += _cat([x[:, BLOCK:] for x in dkv], 1)
            return _sb_alive(cs, zmax), accs, tuple(cs), tuple(out_ces)

        zb = jnp.zeros((BLOCK, BLOCK), BF16)
        qdo = [_cat([_cat([qh[h], zb], 1) if kind == 0 else _cat([zb, doh[h]], 1)
                     for h in (2 * pr, 2 * pr + 1) for kind in (0, 1)], 0) for pr in range(NP)]
        zero = (jnp.zeros((BLOCK, BLOCK), F32),) * len(heads)
        accs = _sb_sweep(i, tile, (jnp.bool_(True), zero, zero, tuple(dsum)))[1]
        dq_ref[...] = (_cat([jnp.where(half0, accs[2 * pr], accs[2 * pr + 1]) for pr in range(NP)], 1)
                       * SCALE).astype(BF16)

        @pl.when(i == nb - 1)
        def _():
            dk_ref[...] = dk_acc[...].astype(BF16)
            dv_ref[...] = dv_acc[...].astype(BF16)

    blk = pl.BlockSpec((BLOCK, SB_W), lambda p, i: (i, p))
    panel = pl.BlockSpec((L, SB_W), lambda p, i: (0, p))
    return pl.pallas_call(
        body,
        name="sb_attention_bwd",
        grid=(nq, nb),
        in_specs=[
            blk,
            pl.BlockSpec((L, SB_W), lambda p, i: (0, nq + p), pipeline_mode=pl.Buffered(1)),
            pl.BlockSpec((L, SB_W), lambda p, i: (0, 2 * nq + p), pipeline_mode=pl.Buffered(1)),
            blk, blk,
            pl.BlockSpec((2 * BLOCK, 2 * BLOCK), lambda p, i: (0, 0)),
        ],
        out_specs=[blk, panel, panel],
        out_shape=[jax.ShapeDtypeStruct((L, SB_WIDTH), BF16)] * 3,
        scratch_shapes=[pltpu.VMEM((L, SB_W), F32), pltpu.VMEM((L, SB_W), F32), pltpu.SMEM((2 * NP,), F32)],
        compiler_params=_params(("arbitrary", "arbitrary")),
    )(qkv, qkv, qkv, o_sb, do_sb, tt)


SWA_PAIRS = SWA_WIDTH // BLOCK
PAIRS_PER_KV = SWA_PAIRS // SWA_KV_HEADS
CB_SWK = SWA_PAIRS
CB_SWV = SWA_PAIRS + 1


def rope_tables(L):
    half = HEAD_DIM // 2
    inv = ROPE_THETA ** (-jnp.arange(half, dtype=F32) / half)
    pos = (jnp.arange(L) - PAD).astype(F32)
    ang = pos[:, None] * inv[None, :]
    reps = BLOCK // half
    return jnp.tile(jnp.cos(ang), (1, reps)), jnp.tile(jnp.sin(ang), (1, reps))


def _rot_half(x):
    lane = lax.broadcasted_iota(jnp.int32, (1, BLOCK), 1)
    first = (lane % HEAD_DIM) < (HEAD_DIM // 2)
    return jnp.where(first, -pltpu.roll(x, BLOCK - HEAD_DIM // 2, axis=1), pltpu.roll(x, HEAD_DIM // 2, axis=1))


def _rope(x, cos, sin):
    return x * cos + _rot_half(x) * sin


def _unrope(x, cos, sin):
    return x * cos - _rot_half(x) * sin


def _swa_specs():
    prev = lambda n: jnp.maximum(n - 1, 0)
    cur = lambda n: n
    blk = lambda f, c: pl.BlockSpec((BLOCK, BLOCK), lambda n: (f(n), c))
    return [
        pl.BlockSpec((BLOCK, SWA_WIDTH), lambda n: (n, 0)),
        blk(prev, CB_SWK), blk(cur, CB_SWK), blk(prev, CB_SWV), blk(cur, CB_SWV),
        blk(prev, 0), blk(cur, 0), blk(prev, 0), blk(cur, 0),
        pl.BlockSpec(memory_space=pltpu.SMEM),
    ]


def _swa_probs(n, q_ref, kp_ref, kc_ref, vp_ref, vc_ref, cp_ref, cc_ref, sp_ref, sc_ref, sink_ref):
    lane = lax.broadcasted_iota(jnp.int32, (1, BLOCK), 1)
    halves = (lane < HEAD_DIM, lane >= HEAD_DIM)
    cosc, sinc = cc_ref[...], sc_ref[...]
    qs = [_rope(q_ref[:, p * BLOCK:(p + 1) * BLOCK], cosc, sinc) * SCALE for p in range(SWA_PAIRS)]
    kb = jnp.concatenate([_rope(kp_ref[...], cp_ref[...], sp_ref[...]), _rope(kc_ref[...], cosc, sinc)], axis=0)
    vb = jnp.concatenate([vp_ref[...], vc_ref[...]], axis=0)
    kv = {True: (kb, vb), False: (pltpu.roll(kb, HEAD_DIM, axis=1), pltpu.roll(vb, HEAD_DIM, axis=1))}
    rows = PAIRS_PER_KV * BLOCK
    r = lax.broadcasted_iota(jnp.int32, (rows, 2 * BLOCK), 0) % BLOCK
    c = lax.broadcasted_iota(jnp.int32, (rows, 2 * BLOCK), 1)
    valid = (c > r) & (c <= r + BLOCK) & ((n - 1) * BLOCK + c >= PAD)
    combos = [(g, a) for g in range(SWA_KV_HEADS) for a in range(2)]
    qst, ksel, vsel, scores = {}, {}, {}, {}
    for g, a in combos:
        qst[g, a] = jnp.concatenate(
            [jnp.where(halves[a], qs[g * PAIRS_PER_KV + j], 0.0) for j in range(PAIRS_PER_KV)], axis=0).astype(BF16)
        ksel[g, a], vsel[g, a] = kv[g == a]
    for g, a in combos:
        scores[g, a] = _dot_nt(qst[g, a], ksel[g, a].astype(BF16))
    out = {}
    for g, a in combos:
        s = jnp.where(valid, scores[g, a], -1e30)
        sink = jnp.concatenate([jnp.full((BLOCK, BLOCK), sink_ref[0, 2 * (g * PAIRS_PER_KV + j) + a], F32)
                                for j in range(PAIRS_PER_KV)], axis=0)
        mx = jnp.maximum(jnp.max(s, axis=1, keepdims=True), sink)
        pe = jnp.exp(s - _twice(mx))
        es = jnp.exp(sink - mx)
        inv = 1.0 / (_row_sums(pe) + es)
        out[g, a] = (qst[g, a], ksel[g, a], vsel[g, a], pe * _twice(inv), es * inv, halves[a])
    return combos, out


def _twice(x):
    return jnp.concatenate([x, x], axis=1)


def _row_sums(x):
    return _dot(_split_bf16(x), jnp.ones((4 * BLOCK, BLOCK), BF16))


def swa_attention_fwd(proj, cos, sin, sinks):
    L = proj.shape[0]

    def body(q_ref, kp_ref, kc_ref, vp_ref, vc_ref, cp_ref, cc_ref, sp_ref, sc_ref, sink_ref, o_ref):
        n = pl.program_id(0)
        combos, parts = _swa_probs(n, q_ref, kp_ref, kc_ref, vp_ref, vc_ref, cp_ref, cc_ref, sp_ref, sc_ref, sink_ref)
        outs = {}
        for g, a in combos:
            qst, ksel, vsel, probs, psink, half = parts[g, a]
            outs[g, a] = _dot(probs.astype(BF16), jnp.where(half, vsel, 0.0).astype(BF16))
        for g in range(SWA_KV_HEADS):
            both = outs[g, 0] + outs[g, 1]
            for j in range(PAIRS_PER_KV):
                p = g * PAIRS_PER_KV + j
                o_ref[:, p * BLOCK:(p + 1) * BLOCK] = both[j * BLOCK:(j + 1) * BLOCK]

    return pl.pallas_call(
        body,
        name="swa_attention_fwd",
        grid=(L // BLOCK,),
        in_specs=_swa_specs(),
        out_specs=pl.BlockSpec((BLOCK, SWA_WIDTH), lambda n: (n, 0)),
        out_shape=jax.ShapeDtypeStruct((L, SWA_WIDTH), F32),
        compiler_params=_params(("arbitrary",)),
    )(proj, proj, proj, proj, proj, cos, cos, sin, sin, sinks)


def swa_attention_bwd(proj, cos, sin, sinks, do_sw):
    L = proj.shape[0]

    def body(q_ref, kp_ref, kc_ref, vp_ref, vc_ref, cp_ref, cc_ref, sp_ref, sc_ref, sink_ref, do_ref,
             dq_ref, dk_ref, dv_ref, ds_ref):
        n = pl.program_id(0)

        @pl.when(n == 0)
        def _():
            dk_ref[...] = jnp.zeros_like(dk_ref)
            dv_ref[...] = jnp.zeros_like(dv_ref)
            ds_ref[...] = jnp.zeros_like(ds_ref)

        combos, parts = _swa_probs(n, q_ref, kp_ref, kc_ref, vp_ref, vc_ref, cp_ref, cc_ref, sp_ref, sc_ref, sink_ref)
        lane8 = lax.broadcasted_iota(jnp.int32, (8, BLOCK), 1)
        dos, dps = {}, {}
        for g, a in combos:
            half = parts[g, a][5]
            dos[g, a] = jnp.concatenate(
                [jnp.where(half, do_ref[:, (g * PAIRS_PER_KV + j) * BLOCK:(g * PAIRS_PER_KV + j + 1) * BLOCK], 0.0)
                 for j in range(PAIRS_PER_KV)], axis=0).astype(BF16)
        for g, a in combos:
            dps[g, a] = _dot_nt(dos[g, a], parts[g, a][2].astype(BF16))
        dqs = {}
        dkb = jnp.zeros((2 * BLOCK, BLOCK), F32)
        dvb = jnp.zeros((2 * BLOCK, BLOCK), F32)
        dsk = jnp.zeros((8, BLOCK), F32)
        for g, a in combos:
            qst, ksel, vsel, probs, psink, half = parts[g, a]
            dp = dps[g, a]
            delta = _row_sums(probs * dp)
            ds = (probs * (dp - _twice(delta))).astype(BF16)
            pd = psink * delta
            for j in range(PAIRS_PER_KV):
                head = 2 * (g * PAIRS_PER_KV + j) + a
                dsk = dsk + jnp.where(lane8 == head, -jnp.sum(pd[j * BLOCK:(j + 1) * BLOCK, :1]), 0.0)
            dqs[g, a] = _dot(ds, jnp.where(half, ksel, 0.0).astype(BF16))
            dk_a = _dot_tn(ds, qst)
            dv_a = _dot_tn(probs.astype(BF16), dos[g, a])
            if g != a:
                dk_a = pltpu.roll(dk_a, HEAD_DIM, axis=1)
                dv_a = pltpu.roll(dv_a, HEAD_DIM, axis=1)
            dkb = dkb + dk_a
            dvb = dvb + dv_a
        cosc, sinc = cc_ref[...], sc_ref[...]
        for g in range(SWA_KV_HEADS):
            both = (dqs[g, 0] + dqs[g, 1]) * SCALE
            for j in range(PAIRS_PER_KV):
                p = g * PAIRS_PER_KV + j
                dq_ref[:, p * BLOCK:(p + 1) * BLOCK] = _unrope(both[j * BLOCK:(j + 1) * BLOCK], cosc, sinc).astype(BF16)
        ds_ref[...] += dsk
        cur = pl.ds(pl.multiple_of(n * BLOCK, BLOCK), BLOCK)
        dk_ref[cur, :] += _unrope(dkb[BLOCK:], cosc, sinc)
        dv_ref[cur, :] += dvb[BLOCK:]

        @pl.when(n > 0)
        def _():
            prv = pl.ds(pl.multiple_of((n - 1) * BLOCK, BLOCK), BLOCK)
            dk_ref[prv, :] += _unrope(dkb[:BLOCK], cp_ref[...], sp_ref[...])
            dv_ref[prv, :] += dvb[:BLOCK]

    whole = lambda n: (0, 0)
    row = pl.BlockSpec((BLOCK, SWA_WIDTH), lambda n: (n, 0))
    return pl.pallas_call(
        body,
        name="swa_attention_bwd",
        grid=(L // BLOCK,),
        in_specs=_swa_specs() + [row],
        out_specs=[row, pl.BlockSpec((L, BLOCK), whole), pl.BlockSpec((L, BLOCK), whole),
                   pl.BlockSpec((8, BLOCK), whole)],
        out_shape=[jax.ShapeDtypeStruct((L, SWA_WIDTH), BF16), jax.ShapeDtypeStruct((L, BLOCK), F32),
                   jax.ShapeDtypeStruct((L, BLOCK), F32), jax.ShapeDtypeStruct((8, BLOCK), F32)],
        compiler_params=_params(("arbitrary",)),
    )(proj, proj, proj, proj, proj, cos, cos, sin, sin, sinks, do_sw)


ROW_TILE = 640
TAIL_ROWS = 208


def _pick(n, cands):
    for c in cands:
        if n % c == 0:
            return c
    raise ValueError(f"no tile for {n}")


def in_proj(h0, gain, w, name, out_dtype):
    L, D = h0.shape
    N = w.shape[1]
    tm = _pick(L, (ROW_TILE, BLOCK))
    tn = _pick(N, (1792, 1536, 1280, 896, 640, 512, 384, 256, 128))

    def body(h_ref, g_ref, w_ref, o_ref, xn_ref):
        @pl.when(pl.program_id(1) == 0)
        def _():
            x = h_ref[...]
            r = lax.rsqrt(jnp.mean(x * x, axis=1, keepdims=True) + RMS_EPS)
            xn_ref[...] = ((x * r) * g_ref[...]).astype(BF16)
        o_ref[...] = _dot(xn_ref[...], w_ref[...]).astype(out_dtype)

    return pl.pallas_call(
        body,
        name=name,
        grid=(L // tm, N // tn),
        in_specs=[pl.BlockSpec((tm, D), lambda i, j: (i, 0)),
                  pl.BlockSpec((1, D), lambda i, j: (0, 0)),
                  pl.BlockSpec((D, tn), lambda i, j: (0, j))],
        out_specs=[pl.BlockSpec((tm, tn), lambda i, j: (i, j)),
                   pl.BlockSpec((tm, D), lambda i, j: (i, 0))],
        out_shape=[jax.ShapeDtypeStruct((L, N), out_dtype), jax.ShapeDtypeStruct((L, D), BF16)],
        compiler_params=_params(("arbitrary", "arbitrary")),
    )(h0, gain, w)


def matmul_tn(a, b, name):
    Kd, M = a.shape
    N = b.shape[1]
    tk = _pick(Kd, (ROW_TILE, BLOCK))
    tn = _pick(N, (1280, 1024, 896, 640, 512, 256, 128))
    nk = Kd // tk

    def body(a_ref, b_ref, o_ref):
        k = pl.program_id(1)

        @pl.when(k == 0)
        def _():
            o_ref[...] = jnp.zeros_like(o_ref)
        o_ref[...] += _dot_tn(a_ref[...], b_ref[...])

    return pl.pallas_call(
        body,
        name=name,
        grid=(N // tn, nk),
        in_specs=[pl.BlockSpec((tk, M), lambda j, k: (k, 0)),
                  pl.BlockSpec((tk, tn), lambda j, k: (k, j))],
        out_specs=pl.BlockSpec((M, tn), lambda j, k: (0, j)),
        out_shape=jax.ShapeDtypeStruct((M, N), F32),
        compiler_params=_params(("arbitrary", "arbitrary")),
    )(a, b)


def in_proj_bwd(dproj, w, h0, gain, dh1):
    L, N = dproj.shape
    D = w.shape[0]
    tm = _pick(L, (ROW_TILE, BLOCK))
    tk = _pick(N, (3200, 1280, 640, 512, 256, 128))
    nk = N // tk

    def body(dp_ref, w_ref, h_ref, g_ref, dh1_ref, dh0_ref, dg_ref, acc_ref):
        i, k = pl.program_id(0), pl.program_id(1)

        @pl.when(k == 0)
        def _():
            acc_ref[...] = jnp.zeros_like(acc_ref)

        @pl.when((i == 0) & (k == 0))
        def _():
            dg_ref[...] = jnp.zeros_like(dg_ref)

        acc_ref[...] += _dot_nt(dp_ref[...], w_ref[...])

        @pl.when(k == nk - 1)
        def _():
            x = h_ref[...]
            r = lax.rsqrt(jnp.mean(x * x, axis=1, keepdims=True) + RMS_EPS)
            xhat = x * r
            dxn = acc_ref[...]
            dg_ref[...] += jnp.sum(dxn * xhat, axis=0, keepdims=True)
            dxh = dxn * g_ref[...]
            dh0_ref[...] = r * (dxh - xhat * jnp.mean(dxh * xhat, axis=1, keepdims=True)) + dh1_ref[...]

    row = pl.BlockSpec((tm, D), lambda i, k: (i, 0))
    vec = pl.BlockSpec((1, D), lambda i, k: (0, 0))
    return pl.pallas_call(
        body,
        name="in_proj_bwd",
        grid=(L // tm, nk),
        in_specs=[pl.BlockSpec((tm, tk), lambda i, k: (i, k)),
                  pl.BlockSpec((D, tk), lambda i, k: (0, k)),
                  row, vec, row],
        out_specs=[row, vec],
        out_shape=[jax.ShapeDtypeStruct((L, D), F32), jax.ShapeDtypeStruct((1, D), F32)],
        scratch_shapes=[pltpu.VMEM((tm, D), F32)],
        compiler_params=_params(("arbitrary", "arbitrary")),
    )(dproj, w, h0, gain, dh1)


def tail_fwd_bwd(h0, tgt, o_sb, o_sw, gates, w_bsb, w_bswa, w_out, gain_f):
    L, D = h0.shape
    R = _pick(L, (TAIL_ROWS, BLOCK))
    z0, z1, z2, z3 = 0, SB_WIDTH, SB_WIDTH + SWA_WIDTH, SB_WIDTH + SWA_WIDTH + D_MODEL

    def body(h_ref, t_ref, osb_ref, osw_ref, g_ref, wsb_ref, wsw_ref, wo_ref, gf_ref,
             dosb_ref, dosw_ref, dg_ref, dh1_ref, mb_ref, usb_ref, usw_ref, dh1b_ref, dysb_ref, dysw_ref,
             dgf_ref, loss_ref):
        i = pl.program_id(0)

        @pl.when(i == 0)
        def _():
            dgf_ref[...] = jnp.zeros_like(dgf_ref)
            loss_ref[...] = jnp.zeros_like(loss_ref)

        sbz = g_ref[:, z0:z1]
        swz = g_ref[:, z1:z2]
        s1 = jax.nn.sigmoid(g_ref[:, z2:z3])
        s2 = jax.nn.sigmoid(g_ref[:, z3:])
        sg_sb = jax.nn.sigmoid(sbz)
        sg_sw = jax.nn.sigmoid(swz)
        silu_sb = sbz * sg_sb
        silu_sw = swz * sg_sw
        osb = osb_ref[...]
        osw = osw_ref[...]
        usb = (osb * silu_sb).astype(BF16)
        usw = (osw * silu_sw).astype(BF16)
        y_sb = _dot(usb, wsb_ref[...])
        y_sw = _dot(usw, wsw_ref[...])
        mb = (s1 * y_sb + s2 * y_sw).astype(BF16)
        h1 = h_ref[...] + _dot(mb, wo_ref[...])
        rf = lax.rsqrt(jnp.mean(h1 * h1, axis=1, keepdims=True) + RMS_EPS)
        hhat = h1 * rf
        gf = gf_ref[...]
        row = i * R + lax.broadcasted_iota(jnp.int32, (R, 1), 0)
        err = jnp.where(row >= BLOCK, hhat * gf - t_ref[...], 0.0)
        lane0 = (lax.broadcasted_iota(jnp.int32, (8, BLOCK), 0) == 0) & (lax.broadcasted_iota(jnp.int32, (8, BLOCK), 1) == 0)
        loss_ref[...] += jnp.where(lane0, (0.5 / D) * jnp.sum(err * err), 0.0)
        dy = err * (1.0 / D)
        dgf_ref[...] += jnp.sum(dy * hhat, axis=0, keepdims=True)
        dhh = dy * gf
        dh1 = rf * (dhh - hhat * jnp.mean(dhh * hhat, axis=1, keepdims=True))
        dh1b = dh1.astype(BF16)
        dm = _dot_nt(dh1b, wo_ref[...])
        dysb = (dm * s1).astype(BF16)
        dysw = (dm * s2).astype(BF16)
        dusb = _dot_nt(dysb, wsb_ref[...])
        dusw = _dot_nt(dysw, wsw_ref[...])
        dosb_ref[...] = (dusb * silu_sb).astype(BF16)
        dosw_ref[...] = (dusw * silu_sw).astype(BF16)
        dg_ref[:, z0:z1] = (dusb * osb * (sg_sb * (1.0 + sbz * (1.0 - sg_sb)))).astype(BF16)
        dg_ref[:, z1:z2] = (dusw * osw * (sg_sw * (1.0 + swz * (1.0 - sg_sw)))).astype(BF16)
        dg_ref[:, z2:z3] = (dm * y_sb * (s1 * (1.0 - s1))).astype(BF16)
        dg_ref[:, z3:] = (dm * y_sw * (s2 * (1.0 - s2))).astype(BF16)
        dh1_ref[...] = dh1
        mb_ref[...] = mb
        usb_ref[...] = usb
        usw_ref[...] = usw
        dh1b_ref[...] = dh1b
        dysb_ref[...] = dysb
        dysw_ref[...] = dysw

    def rows(n):
        return pl.BlockSpec((R, n), lambda i: (i, 0))

    def whole(shape):
        return pl.BlockSpec(shape, lambda i: (0, 0))

    GW = gates.shape[1]
    return pl.pallas_call(
        body,
        name="tail_fwd_bwd",
        grid=(L // R,),
        in_specs=[rows(D), rows(D), rows(SB_WIDTH), rows(SWA_WIDTH), rows(GW),
                  whole(w_bsb.shape), whole(w_bswa.shape), whole(w_out.shape), whole((1, D))],
        out_specs=[rows(SB_WIDTH), rows(SWA_WIDTH), rows(GW), rows(D),
                   rows(D), rows(SB_WIDTH), rows(SWA_WIDTH), rows(D), rows(D), rows(D),
                   whole((1, D)), whole((8, BLOCK))],
        out_shape=[jax.ShapeDtypeStruct((L, SB_WIDTH), BF16), jax.ShapeDtypeStruct((L, SWA_WIDTH), BF16),
                   jax.ShapeDtypeStruct((L, GW), BF16), jax.ShapeDtypeStruct((L, D), F32),
                   jax.ShapeDtypeStruct((L, D), BF16), jax.ShapeDtypeStruct((L, SB_WIDTH), BF16),
                   jax.ShapeDtypeStruct((L, SWA_WIDTH), BF16), jax.ShapeDtypeStruct((L, D), BF16),
                   jax.ShapeDtypeStruct((L, D), BF16), jax.ShapeDtypeStruct((L, D), BF16),
                   jax.ShapeDtypeStruct((1, D), F32), jax.ShapeDtypeStruct((8, BLOCK), F32)],
        compiler_params=_params(("arbitrary",)),
    )(h0, tgt, o_sb, o_sw, gates, w_bsb, w_bswa, w_out, gain_f)


def local_step(x, tgt, meta, gain, w_in, w_bsb, w_bswa, w_out, sinks, gain_f):
    S, D = x.shape
    L = S + BLOCK
    h0 = jnp.concatenate([jnp.zeros((PAD, D), F32), meta, x], axis=0)
    tgt_p = jnp.concatenate([jnp.zeros((BLOCK, D), F32), tgt], axis=0)
    tt = _suffix_matrix()
    cos, sin = rope_tables(L)
    qkv, xn = in_proj(h0, gain, w_in[:, :SB_COLS], "in_proj_sb", BF16)
    proj_sw, _ = in_proj(h0, gain, w_in[:, SB_COLS:GATE_COL0], "in_proj_swa", F32)
    gates, _ = in_proj(h0, gain, w_in[:, GATE_COL0:], "in_proj_gates", F32)
    o_sb = sb_attention_fwd(qkv, tt)
    o_sw = swa_attention_fwd(proj_sw, cos, sin, sinks)
    (do_sb, do_sw, dgates, dh1, mb, usb, usw, dh1b, dysb, dysw, dgf, loss) = tail_fwd_bwd(
        h0, tgt_p, o_sb, o_sw, gates, w_bsb, w_bswa, w_out, gain_f)
    dq_sb, dk_sb, dv_sb = sb_attention_bwd(qkv, o_sb, do_sb, tt)
    dq_sw, dk_sw, dv_sw, dsinks = swa_attention_bwd(proj_sw, cos, sin, sinks, do_sw)
    dproj = jnp.concatenate([dq_sb, dk_sb, dv_sb, dq_sw, dk_sw.astype(BF16), dv_sw.astype(BF16), dgates], axis=1)
    dw_in = matmul_tn(xn, dproj, "dw_in")
    dw_out = matmul_tn(mb, dh1b, "dw_out")
    dw_bsb = matmul_tn(usb, dysb, "dw_bsb")
    dw_bswa = matmul_tn(usw, dysw, "dw_bswa")
    dh0, dgain = in_proj_bwd(dproj, w_in, h0, gain, dh1)
    return (loss[0, 0], dh0[BLOCK:], dh0[PAD:BLOCK], dgain, dw_in, dw_bsb, dw_bswa, dw_out,
            dsinks[:1, :SWA_Q_HEADS], dgf)


MESH_IDS = pl.DeviceIdType.MESH
ANY = pl.BlockSpec(memory_space=pl.ANY)


def _place():
    return lax.axis_index("x"), lax.axis_index("y"), lax.axis_index("c")


def _index(x, y, c):
    return 4 * x + 2 * y + c


def all_gather(block, name):
    def body(x_ref, out_ref, send_sems, recv_sems, local_sem):
        x, y, c = _place()
        me, sibling = (x, y, c), (x, y, 1 - c)
        chips = [(1 - x, y), (x, 1 - y), (1 - x, 1 - y)]

        def copy(k, blk, to, src=None):
            dst = out_ref.at[_index(*blk)]
            return pltpu.make_async_remote_copy(
                src_ref=dst if src is None else src, dst_ref=dst,
                send_sem=send_sems.at[k], recv_sem=recv_sems.at[k], device_id=to, device_id_type=MESH_IDS)

        mine = pltpu.make_async_copy(x_ref, out_ref.at[_index(*me)], local_sem)
        mine.start()
        first = [copy(0, me, sibling, src=x_ref)]
        first += [copy(1 + j, me, (*chip, c), src=x_ref) for j, chip in enumerate(chips)]
        for cp in first:
            cp.start()
        passed = [copy(4 + j, (*chip, c), sibling) for j, chip in enumerate(chips)]
        for j, chip in enumerate(chips):
            copy(1 + j, (*chip, c), me).wait_recv()
            passed[j].start()
        copy(0, sibling, me).wait_recv()
        for j, chip in enumerate(chips):
            copy(4 + j, (*chip, 1 - c), me).wait_recv()
        for cp in first + passed:
            cp.wait_send()
        mine.wait()

    return pl.pallas_call(
        body,
        name=name,
        out_shape=jax.ShapeDtypeStruct((N_DEV,) + block.shape, block.dtype),
        in_specs=[ANY],
        out_specs=ANY,
        scratch_shapes=[pltpu.SemaphoreType.DMA((7,)), pltpu.SemaphoreType.DMA((7,)), pltpu.SemaphoreType.DMA],
    )(block)


def exchange_partials(parts):
    def body(g_ref, out_ref, send_sems, recv_sems, local_sem):
        x, y, c = _place()
        me = _index(x, y, c)
        mine = pltpu.make_async_copy(g_ref.at[me], out_ref.at[me], local_sem)
        mine.start()
        copies = []
        for m in range(1, N_DEV):
            px = 1 - x if m & 4 else x
            py = 1 - y if m & 2 else y
            pc = 1 - c if m & 1 else c
            cp = pltpu.make_async_remote_copy(
                src_ref=g_ref.at[_index(px, py, pc)], dst_ref=out_ref.at[me],
                send_sem=send_sems.at[m - 1], recv_sem=recv_sems.at[m - 1],
                device_id=(px, py, pc), device_id_type=MESH_IDS)
            cp.start()
            copies.append(cp)
        for cp in copies:
            cp.wait()
        mine.wait()

    return pl.pallas_call(
        body,
        name="exchange_partials",
        out_shape=jax.ShapeDtypeStruct(parts.shape, parts.dtype),
        in_specs=[ANY],
        out_specs=ANY,
        scratch_shapes=[pltpu.SemaphoreType.DMA((7,)), pltpu.SemaphoreType.DMA((7,)), pltpu.SemaphoreType.DMA],
    )(parts)


def _adamw(w, g, m, v):
    m = ADAM_B1 * m + (1.0 - ADAM_B1) * g
    v = ADAM_B2 * v + (1.0 - ADAM_B2) * (g * g)
    m_hat = m / (1.0 - ADAM_B1 ** ADAM_STEP)
    v_hat = v / (1.0 - ADAM_B2 ** ADAM_STEP)
    delta = -ADAM_LR * (m_hat / (jnp.sqrt(v_hat) + ADAM_EPS) + ADAM_WD * w)
    return delta, m, v


def sum_and_adamw(parts, w, m, v, name):
    _, R, C = parts.shape
    tr = _pick(R, (528, 512, 256, 128, 24, 8))

    def body(p_ref, w_ref, m_ref, v_ref, g_ref, d_ref, nm_ref, nv_ref):
        g = p_ref[0].astype(F32)
        for s in range(1, N_DEV):
            g = g + p_ref[s].astype(F32)
        d, nm, nv = _adamw(w_ref[...], g, m_ref[...], v_ref[...])
        g_ref[...] = g
        d_ref[...] = d
        nm_ref[...] = nm
        nv_ref[...] = nv

    row = pl.BlockSpec((tr, C), lambda i: (i, 0))
    return pl.pallas_call(
        body,
        name=name,
        grid=(R // tr,),
        in_specs=[pl.BlockSpec((N_DEV, tr, C), lambda i: (0, i, 0)), row, row, row],
        out_specs=[row, row, row, row],
        out_shape=[jax.ShapeDtypeStruct((R, C), F32)] * 4,
        compiler_params=_params(("arbitrary",)),
    )(parts, w, m, v)


W_IN_SHARD = IN_COLS // N_DEV
ROWS_W_IN = D_MODEL * W_IN_SHARD // BLOCK
ROWS_W_BSB = SB_WIDTH
ROWS_W_ROWSHARD = D_MODEL
SMALL_ROWS = 24


def _pack_shards(w_in, w_bsb, w_bswa, w_out, meta):
    return jnp.concatenate([w_in.reshape(ROWS_W_IN, BLOCK), w_bsb.reshape(ROWS_W_BSB, BLOCK),
                            w_bswa.reshape(ROWS_W_ROWSHARD, BLOCK), w_out.reshape(ROWS_W_ROWSHARD, BLOCK),
                            meta.reshape(N_META, BLOCK)], axis=0)


def _unpack_shards(p):
    o = np.cumsum([0, ROWS_W_IN, ROWS_W_BSB, ROWS_W_ROWSHARD, ROWS_W_ROWSHARD, N_META])
    return (p[o[0]:o[1]].reshape(1, D_MODEL, W_IN_SHARD), p[o[1]:o[2]].reshape(1, SB_WIDTH, BLOCK),
            p[o[2]:o[3]].reshape(1, BLOCK, D_MODEL), p[o[3]:o[4]].reshape(1, BLOCK, D_MODEL),
            p[o[4]:o[5]].reshape(N_META, BLOCK))


def _pack_by_owner(dw_in, dw_bsb, dw_bswa, dw_out, dmeta):
    cols = lambda a: a.reshape(a.shape[0], N_DEV, -1).transpose(1, 0, 2)
    return jnp.concatenate([cols(dw_in).reshape(N_DEV, ROWS_W_IN, BLOCK), cols(dw_bsb),
                            dw_bswa.reshape(N_DEV, ROWS_W_ROWSHARD, BLOCK), dw_out.reshape(N_DEV, ROWS_W_ROWSHARD, BLOCK),
                            cols(dmeta)], axis=1)


def _pack_small(gain, gain_f, sinks, loss):
    z = jnp.zeros((SMALL_ROWS - 16, BLOCK), F32)
    z = z.at[0, :SWA_Q_HEADS].set(sinks.reshape(-1)).at[1, 0].set(loss)
    return jnp.concatenate([gain.reshape(8, BLOCK), gain_f.reshape(8, BLOCK), z], axis=0)


def _unpack_small(p):
    return p[0:8].reshape(1, D_MODEL), p[8:16].reshape(D_MODEL), p[16:17, :SWA_Q_HEADS], p[17, 0]


def kernel(x, meta_tokens, norm_gain, w_in, w_branch_sb, w_branch_swa, w_out, attn_sinks, final_norm_gain, loss_target, m_meta_tokens, m_norm_gain, m_w_in, m_w_branch_sb, m_w_branch_swa, m_w_out, m_attn_sinks, m_final_norm_gain, v_meta_tokens, v_norm_gain, v_w_in, v_w_branch_sb, v_w_branch_swa, v_w_out, v_attn_sinks, v_final_norm_gain):
    meta_bits = lax.bitcast_convert_type(meta_tokens, BF16).reshape(2 * N_META, BLOCK)
    mine = jnp.concatenate([_pack_shards(w_in, w_branch_sb, w_branch_swa, w_out, meta_tokens)[:-N_META].astype(BF16),
                            meta_bits], axis=0)
    full = all_gather(mine, "all_gather_weights")
    o = np.cumsum([0, ROWS_W_IN, ROWS_W_BSB, ROWS_W_ROWSHARD, ROWS_W_ROWSHARD, 2 * N_META])
    cols = lambda a: a.transpose(1, 0, 2).reshape(a.shape[1], -1)
    f_w_in = cols(full[:, o[0]:o[1]].reshape(N_DEV, D_MODEL, W_IN_SHARD))
    f_w_bsb = cols(full[:, o[1]:o[2]])
    f_w_bswa = full[:, o[2]:o[3]].reshape(D_MODEL, D_MODEL)
    f_w_out = full[:, o[3]:o[4]].reshape(D_MODEL, D_MODEL)
    f_meta = cols(lax.bitcast_convert_type(full[:, o[4]:o[5]].reshape(N_DEV, N_META, BLOCK, 2), F32))

    (loss, grad_x, dmeta, dgain, dw_in, dw_bsb, dw_bswa, dw_out, dsinks, dgf) = local_step(
        x[0], loss_target[0], f_meta, norm_gain, f_w_in, f_w_bsb, f_w_bswa, f_w_out, attn_sinks,
        final_norm_gain.reshape(1, D_MODEL))

    parts = exchange_partials(_pack_by_owner(dw_in, dw_bsb, dw_bswa, dw_out, dmeta).astype(BF16))
    packs = [_pack_shards(a[0], b[0], c[0], d[0], e) for a, b, c, d, e in (
        (w_in, w_branch_sb, w_branch_swa, w_out, meta_tokens),
        (m_w_in, m_w_branch_sb, m_w_branch_swa, m_w_out, m_meta_tokens),
        (v_w_in, v_w_branch_sb, v_w_branch_swa, v_w_out, v_meta_tokens))]
    big = [_unpack_shards(p) for p in sum_and_adamw(parts, *packs, "sum_adamw_sharded")]

    small = all_gather(_pack_small(dgain, dgf, dsinks, loss), "all_gather_small")
    zero = jnp.zeros((), F32)
    spacks = [_pack_small(a, b, c, zero) for a, b, c in (
        (norm_gain, final_norm_gain, attn_sinks), (m_norm_gain, m_final_norm_gain, m_attn_sinks),
        (v_norm_gain, v_final_norm_gain, v_attn_sinks))]
    sm = [_unpack_small(p) for p in sum_and_adamw(small, *spacks, "sum_adamw_replicated")]

    def leaves(k):
        b, s = big[k], sm[k]
        return (b[4], s[0], b[0], b[1], b[2], b[3], s[2], s[1])

    return (sm[0][3], grad_x[None], *leaves(0), *leaves(1), *leaves(2), *leaves(3))
```

```python
import numpy as np
import jax
import jax.numpy as jnp
from jax import lax
from jax.experimental import pallas as pl
from jax.experimental.pallas import tpu as pltpu

F32 = jnp.float32
BF16 = jnp.bfloat16

D_MODEL = 1024
N_META = 16
BLOCK = 128
PAD = BLOCK - N_META
HEAD_DIM = 64
SB_HEADS = 8
SB_WIDTH = SB_HEADS * HEAD_DIM
SWA_Q_HEADS = 16
SWA_KV_HEADS = 2
SWA_WIDTH = SWA_Q_HEADS * HEAD_DIM
SWA_KV_WIDTH = SWA_KV_HEADS * HEAD_DIM
ROPE_THETA = 10000.0
RMS_EPS = 1e-6
SCALE = HEAD_DIM ** -0.5
SPLITS = (SB_WIDTH, SB_WIDTH, SB_WIDTH, SWA_WIDTH, SWA_KV_WIDTH, SWA_KV_WIDTH,
          SB_WIDTH, SWA_WIDTH, D_MODEL, D_MODEL)
IN_COLS = sum(SPLITS)
SB_COLS = 3 * SB_WIDTH
SWA_COLS = SWA_WIDTH + 2 * SWA_KV_WIDTH
GATE_COL0 = SB_COLS + SWA_COLS

N_DEV = 8
ADAM_LR = 0.001
ADAM_B1 = 0.9
ADAM_B2 = 0.999
ADAM_EPS = 1e-08
ADAM_WD = 0.01
ADAM_STEP = 10

VMEM_LIMIT = 56 * 1024 * 1024


def _params(sem, **kw):
    return pltpu.CompilerParams(dimension_semantics=sem, vmem_limit_bytes=VMEM_LIMIT, **kw)


def _dot(a, b):
    return jnp.dot(a, b, preferred_element_type=F32)


def _dot_nt(a, b):
    return lax.dot_general(a, b, (((1,), (1,)), ((), ())), preferred_element_type=F32)


def _dot_tn(a, b):
    return lax.dot_general(a, b, (((0,), (0,)), ((), ())), preferred_element_type=F32)


def _cat(xs, axis):
    return xs[0] if len(xs) == 1 else jnp.concatenate(xs, axis=axis)


def _split_bf16(x):
    hi = x.astype(BF16)
    lo = (x - hi.astype(F32)).astype(BF16)
    return jnp.concatenate([hi, lo], axis=1)


def _suffix_matrix():
    j = np.arange(BLOCK)[:, None]
    s = np.arange(BLOCK)[None, :]
    t = np.concatenate([(j >= s).astype(np.float32), np.ones((BLOCK, BLOCK), np.float32)], axis=1)
    return jnp.asarray(np.concatenate([t, t], axis=0), dtype=BF16)


def _softplus(z):
    return jnp.maximum(z, 0.0) + jnp.log(1.0 + jnp.exp(-jnp.abs(z)))


SB_SMALL = 4
SB_PAIRS_FWD = 4
SB_PAIRS_BWD = 2


def _sb_masks(i, nblk):
    if nblk == 0:
        r = lax.broadcasted_iota(jnp.int32, (BLOCK, BLOCK), 0)
        c = lax.broadcasted_iota(jnp.int32, (BLOCK, BLOCK), 1)
        return (c < r) & (i * BLOCK + c >= PAD)
    return lax.broadcasted_iota(jnp.int32, (BLOCK, nblk * BLOCK), 1) >= PAD


def _sb_heads(npairs):
    return [(pr, a) for pr in range(npairs) for a in range(2)]


def _lanes(pr):
    return slice(pr * BLOCK, (pr + 1) * BLOCK)


def _masked_heads(x, half0, npairs):
    out = []
    for pr, a in _sb_heads(npairs):
        xp = x[:, _lanes(pr)]
        out.append((jnp.where(half0, xp, 0.0) if a == 0 else jnp.where(half0, 0.0, xp)).astype(BF16))
    return out


def _sb_sweep(i, tile):
    same = lambda alive: alive
    small = SB_SMALL
    live_tile = tile
    tile = lambda j0, nblk, mask, alive: lax.cond(alive, lambda _: live_tile(j0, nblk, mask), same, alive)
    alive = live_tile(i, 1, _sb_masks(i, 0))
    alive = lax.switch(jnp.minimum(i, 3), [
        same,
        lambda al: tile(0, 1, _sb_masks(i, 1), al),
        lambda al: tile(0, 2, _sb_masks(i, 2), al),
        lambda al: tile(i - 2, 2, None, al)], alive)

    def below(alive):
        rest = i - 2
        n_grp = rest // small
        n_one = rest - n_grp * small
        low_is_one = (n_grp == 0) & (n_one > 0)
        n_plain = n_one - jnp.where(low_is_one, 1, 0)
        alive = lax.fori_loop(0, n_plain, lambda t, al: tile(rest - 1 - t, 1, None, al), alive)
        alive = lax.cond(low_is_one, lambda al: tile(0, 1, _sb_masks(i, 1), al), same, alive)
        alive = lax.fori_loop(0, jnp.maximum(n_grp - 1, 0),
                              lambda t, al: tile((n_grp - 1 - t) * small, small, None, al), alive)
        return lax.cond(n_grp > 0, lambda al: tile(0, small, _sb_masks(i, small), al), same, alive)

    lax.cond(alive & (i > 2), below, same, alive)


def _sb_weights(zs, ss, cs, tt, nblk, mask):
    suf = _dot(_cat([_split_bf16(s[:, b * BLOCK:(b + 1) * BLOCK]) for s in ss for b in range(nblk)], 0), tt)
    ws, out_cs = [], []
    for h in range(len(zs)):
        c = cs[h]
        wb = [None] * nblk
        for b in reversed(range(nblk)):
            sab = suf[(h * nblk + b) * BLOCK:(h * nblk + b + 1) * BLOCK]
            wb[b] = jnp.exp(zs[h][:, b * BLOCK:(b + 1) * BLOCK] + c - sab[:, :BLOCK])
            c = c - sab[:, BLOCK:]
        w = _cat(wb, 1)
        if mask is not None:
            w = jnp.where(mask, w, 0.0)
        ws.append(w)
        out_cs.append(c)
    return ws, out_cs


SB_DEAD = -110.0


def _row_norm_max(x, half0, a):
    xf = x.astype(F32)
    sq = jnp.where(half0, xf * xf, 0.0) if a == 0 else jnp.where(half0, 0.0, xf * xf)
    return jnp.sqrt(jnp.max(jnp.sum(sq, axis=1, keepdims=True)))


def _sb_logit_bounds(i, q_heads, k_ref, kmax_ref, half0, heads):
    @pl.when(i == 0)
    def _():
        for h, (pr, a) in enumerate(heads):
            kmax_ref[h] = _row_norm_max(k_ref[:, _lanes(pr)], half0, a)
    return [_row_norm_max(q_heads[h], half0, a) * kmax_ref[h] * 1.001 + 0.01 for h, (pr, a) in enumerate(heads)]


def _sb_alive(cs, zmax):
    worst = cs[0] + zmax[0]
    for c, zm in zip(cs[1:], zmax[1:]):
        worst = jnp.maximum(worst, c + zm)
    return jnp.max(worst) > SB_DEAD


def sb_attention_fwd(qkv, tt):
    L = qkv.shape[0]
    nb = L // BLOCK
    NP = SB_PAIRS_FWD
    SB_W = NP * BLOCK
    nq = SB_WIDTH // SB_W

    def body(q_ref, k_ref, v_ref, tt_ref, o_ref, kmax_ref, acc_ref, c_ref):
        i = pl.program_id(1)
        half0 = lax.broadcasted_iota(jnp.int32, (1, BLOCK), 1) < HEAD_DIM
        heads = _sb_heads(NP)
        qh = _masked_heads(q_ref[...].astype(F32) * SCALE, half0, NP)
        zmax = _sb_logit_bounds(i, qh, k_ref, kmax_ref, half0, heads)
        acc_ref[...] = jnp.zeros_like(acc_ref)
        c_ref[...] = jnp.zeros_like(c_ref)

        def rows_of(j0, nblk):
            return pl.ds(pl.multiple_of(j0 * BLOCK, BLOCK), nblk * BLOCK)

        def tile(j0, nblk, mask):
            rows = rows_of(j0, nblk)
            zs = [_dot_nt(qh[h], k_ref[rows, _lanes(pr)]) for h, (pr, a) in enumerate(heads)]
            ss = [_softplus(z) for z in zs]
            if mask is not None:
                ss = [jnp.where(mask, s, 0.0) for s in ss]
            ws, cs = _sb_weights(zs, ss, [c_ref[h] for h in range(len(heads))], tt_ref[...], nblk, mask)
            for h, (pr, a) in enumerate(heads):
                acc_ref[h] += _dot(ws[h].astype(BF16), v_ref[rows, _lanes(pr)])
                c_ref[h] = cs[h]
            return _sb_alive(cs, zmax)

        _sb_sweep(i, tile)
        o_ref[...] = _cat([jnp.where(half0, acc_ref[2 * pr], acc_ref[2 * pr + 1]) for pr in range(NP)], 1)

    panel = lambda c: pl.BlockSpec((L, SB_W), lambda p, i: (0, c * nq + p), pipeline_mode=pl.Buffered(1))
    return pl.pallas_call(
        body,
        name="sb_attention_fwd",
        grid=(nq, nb),
        in_specs=[
            pl.BlockSpec((BLOCK, SB_W), lambda p, i: (i, p)),
            panel(1), panel(2),
            pl.BlockSpec((2 * BLOCK, 2 * BLOCK), lambda p, i: (0, 0)),
        ],
        out_specs=pl.BlockSpec((BLOCK, SB_W), lambda p, i: (i, p)),
        out_shape=jax.ShapeDtypeStruct((L, SB_WIDTH), F32),
        scratch_shapes=[pltpu.SMEM((2 * NP,), F32), pltpu.VMEM((2 * NP, BLOCK, BLOCK), F32),
                        pltpu.VMEM((2 * NP, BLOCK, BLOCK), F32)],
        compiler_params=_params(("arbitrary", "arbitrary")),
    )(qkv, qkv, qkv, tt)


def sb_attention_bwd(qkv, o_sb, do_sb, tt):
    L = qkv.shape[0]
    nb = L // BLOCK
    NP = SB_PAIRS_BWD
    SB_W = NP * BLOCK
    nq = SB_WIDTH // SB_W

    def body(q_ref, k_ref, v_ref, o_ref, do_ref, tt_ref, dq_ref, dk_ref, dv_ref, dk_acc, dv_acc, kmax_ref,
             acc_ref, c_ref, ce_ref):
        i = pl.program_id(1)
        half0 = lax.broadcasted_iota(jnp.int32, (1, BLOCK), 1) < HEAD_DIM
        heads = _sb_heads(NP)

        @pl.when(i == 0)
        def _():
            dk_acc[...] = jnp.zeros_like(dk_acc)
            dv_acc[...] = jnp.zeros_like(dv_acc)

        qh = _masked_heads(q_ref[...].astype(F32) * SCALE, half0, NP)
        zmax = _sb_logit_bounds(i, qh, k_ref, kmax_ref, half0, heads)
        do = do_ref[...]
        doh = _masked_heads(do, half0, NP)
        od = o_ref[...] * do.astype(F32)
        acc_ref[...] = jnp.zeros_like(acc_ref)
        c_ref[...] = jnp.zeros_like(c_ref)
        for h, (pr, a) in enumerate(heads):
            x = od[:, _lanes(pr)]
            x = jnp.where(half0, x, 0.0) if a == 0 else jnp.where(half0, 0.0, x)
            ce_ref[h] = jnp.broadcast_to(jnp.sum(x, axis=1, keepdims=True), (BLOCK, BLOCK))

        def rows_of(j0, nblk):
            return pl.ds(pl.multiple_of(j0 * BLOCK, BLOCK), nblk * BLOCK)

        def tile(j0, nblk, mask):
            cs = [c_ref[h] for h in range(len(heads))]
            ces = [ce_ref[h] for h in range(len(heads))]
            rows = rows_of(j0, nblk)
            zs = [_dot_nt(qh[h], k_ref[rows, _lanes(pr)]) for h, (pr, a) in enumerate(heads)]
            dws = [_dot_nt(doh[h], v_ref[rows, _lanes(pr)]) for h, (pr, a) in enumerate(heads)]
            ss = [_softplus(z) for z in zs]
            sigs = [jnp.exp(z - s) for z, s in zip(zs, ss)]
            if mask is not None:
                ss = [jnp.where(mask, s, 0.0) for s in ss]
            ws, cs = _sb_weights(zs, ss, cs, tt_ref[...], nblk, mask)
            wbs = [w.astype(BF16) for w in ws]
            es = [wb.astype(F32) * dw for wb, dw in zip(wbs, dws)]
            esuf = _dot(_cat([_split_bf16(e[:, b * BLOCK:(b + 1) * BLOCK]) for e in es for b in range(nblk)], 0),
                        tt_ref[...])
            out_ces, dzbs = [], []
            for h, (pr, a) in enumerate(heads):
                ce = ces[h]
                dzs = [None] * nblk
                for b in reversed(range(nblk)):
                    eab = esuf[(h * nblk + b) * BLOCK:(h * nblk + b + 1) * BLOCK]
                    sl = slice(b * BLOCK, (b + 1) * BLOCK)
                    e = es[h][:, sl]
                    dzs[b] = e - sigs[h][:, sl] * (e + (ce - eab[:, :BLOCK]))
                    ce = ce - eab[:, BLOCK:]
                dz = _cat(dzs, 1)
                if mask is not None:
                    dz = jnp.where(mask, dz, 0.0)
                dzbs.append(dz.astype(BF16))
                out_ces.append(ce)
            for h, (pr, a) in enumerate(heads):
                acc_ref[h] += _dot(dzbs[h], k_ref[rows, _lanes(pr)])
                c_ref[h] = cs[h]
                ce_ref[h] = out_ces[h]
            dkv = [_dot_tn(_cat([dzbs[2 * pr], wbs[2 * pr], dzbs[2 * pr + 1], wbs[2 * pr + 1]], 0), qdo[pr])
                   for pr in range(NP)]
            dk_acc[rows, :] += _cat([x[:, :BLOCK] for x in dkv], 1)
            dv_acc[rows, :] += _cat([x[:, BLOCK:] for x in dkv], 1)
            return _sb_alive(cs, zmax)

        zb = jnp.zeros((BLOCK, BLOCK), BF16)
        qdo = [_cat([_cat([qh[h], zb], 1) if kind == 0 else _cat([zb, doh[h]], 1)
                     for h in (2 * pr, 2 * pr + 1) for kind in (0, 1)], 0) for pr in range(NP)]
        _sb_sweep(i, tile)
        dq_ref[...] = (_cat([jnp.where(half0, acc_ref[2 * pr], acc_ref[2 * pr + 1]) for pr in range(NP)], 1)
                       * SCALE).astype(BF16)

        @pl.when(i == nb - 1)
        def _():
            dk_ref[...] = dk_acc[...].astype(BF16)
            dv_ref[...] = dv_acc[...].astype(BF16)

    blk = pl.BlockSpec((BLOCK, SB_W), lambda p, i: (i, p))
    panel = pl.BlockSpec((L, SB_W), lambda p, i: (0, p))
    return pl.pallas_call(
        body,
        name="sb_attention_bwd",
        grid=(nq, nb),
        in_specs=[
            blk,
            pl.BlockSpec((L, SB_W), lambda p, i: (0, nq + p), pipeline_mode=pl.Buffered(1)),
            pl.BlockSpec((L, SB_W), lambda p, i: (0, 2 * nq + p), pipeline_mode=pl.Buffered(1)),
            blk, blk,
            pl.BlockSpec((2 * BLOCK, 2 * BLOCK), lambda p, i: (0, 0)),
        ],
        out_specs=[blk, panel, panel],
        out_shape=[jax.ShapeDtypeStruct((L, SB_WIDTH), BF16)] * 3,
        scratch_shapes=[pltpu.VMEM((L, SB_W), F32), pltpu.VMEM((L, SB_W), F32), pltpu.SMEM((2 * NP,), F32)]
        + [pltpu.VMEM((2 * NP, BLOCK, BLOCK), F32)] * 3,
        compiler_params=_params(("arbitrary", "arbitrary")),
    )(qkv, qkv, qkv, o_sb, do_sb, tt)


SWA_PAIRS = SWA_WIDTH // BLOCK
PAIRS_PER_KV = SWA_PAIRS // SWA_KV_HEADS
CB_SWK = SWA_PAIRS
CB_SWV = SWA_PAIRS + 1


def rope_tables(L):
    half = HEAD_DIM // 2
    inv = ROPE_THETA ** (-jnp.arange(half, dtype=F32) / half)
    pos = (jnp.arange(L) - PAD).astype(F32)
    ang = pos[:, None] * inv[None, :]
    reps = BLOCK // half
    return jnp.tile(jnp.cos(ang), (1, reps)), jnp.tile(jnp.sin(ang), (1, reps))


def _rot_half(x):
    lane = lax.broadcasted_iota(jnp.int32, (1, BLOCK), 1)
    first = (lane % HEAD_DIM) < (HEAD_DIM // 2)
    return jnp.where(first, -pltpu.roll(x, BLOCK - HEAD_DIM // 2, axis=1), pltpu.roll(x, HEAD_DIM // 2, axis=1))


def _rope(x, cos, sin):
    return x * cos + _rot_half(x) * sin


def _unrope(x, cos, sin):
    return x * cos - _rot_half(x) * sin


def _swa_specs():
    prev = lambda n: jnp.maximum(n - 1, 0)
    cur = lambda n: n
    blk = lambda f, c: pl.BlockSpec((BLOCK, BLOCK), lambda n: (f(n), c))
    return [
        pl.BlockSpec((BLOCK, SWA_WIDTH), lambda n: (n, 0)),
        blk(prev, CB_SWK), blk(cur, CB_SWK), blk(prev, CB_SWV), blk(cur, CB_SWV),
        blk(prev, 0), blk(cur, 0), blk(prev, 0), blk(cur, 0),
        pl.BlockSpec(memory_space=pltpu.SMEM),
    ]


def _swa_probs(n, q_ref, kp_ref, kc_ref, vp_ref, vc_ref, cp_ref, cc_ref, sp_ref, sc_ref, sink_ref):
    lane = lax.broadcasted_iota(jnp.int32, (1, BLOCK), 1)
    halves = (lane < HEAD_DIM, lane >= HEAD_DIM)
    cosc, sinc = cc_ref[...], sc_ref[...]
    qs = [_rope(q_ref[:, p * BLOCK:(p + 1) * BLOCK], cosc, sinc) * SCALE for p in range(SWA_PAIRS)]
    kb = jnp.concatenate([_rope(kp_ref[...], cp_ref[...], sp_ref[...]), _rope(kc_ref[...], cosc, sinc)], axis=0)
    vb = jnp.concatenate([vp_ref[...], vc_ref[...]], axis=0)
    kv = {True: (kb, vb), False: (pltpu.roll(kb, HEAD_DIM, axis=1), pltpu.roll(vb, HEAD_DIM, axis=1))}
    rows = PAIRS_PER_KV * BLOCK
    r = lax.broadcasted_iota(jnp.int32, (rows, 2 * BLOCK), 0) % BLOCK
    c = lax.broadcasted_iota(jnp.int32, (rows, 2 * BLOCK), 1)
    valid = (c > r) & (c <= r + BLOCK) & ((n - 1) * BLOCK + c >= PAD)
    combos = [(g, a) for g in range(SWA_KV_HEADS) for a in range(2)]
    qst, ksel, vsel, scores = {}, {}, {}, {}
    for g, a in combos:
        qst[g, a] = jnp.concatenate(
            [jnp.where(halves[a], qs[g * PAIRS_PER_KV + j], 0.0) for j in range(PAIRS_PER_KV)], axis=0).astype(BF16)
        ksel[g, a], vsel[g, a] = kv[g == a]
    for g, a in combos:
        scores[g, a] = _dot_nt(qst[g, a], ksel[g, a].astype(BF16))
    out = {}
    for g, a in combos:
        s = jnp.where(valid, scores[g, a], -1e30)
        sink = jnp.concatenate([jnp.full((BLOCK, BLOCK), sink_ref[0, 2 * (g * PAIRS_PER_KV + j) + a], F32)
                                for j in range(PAIRS_PER_KV)], axis=0)
        mx = jnp.maximum(jnp.max(s, axis=1, keepdims=True), sink)
        pe = jnp.exp(s - _twice(mx))
        es = jnp.exp(sink - mx)
        inv = 1.0 / (_row_sums(pe) + es)
        out[g, a] = (qst[g, a], ksel[g, a], vsel[g, a], pe * _twice(inv), es * inv, halves[a])
    return combos, out


def _twice(x):
    return jnp.concatenate([x, x], axis=1)


def _row_sums(x):
    return _dot(_split_bf16(x), jnp.ones((4 * BLOCK, BLOCK), BF16))


def swa_attention_fwd(proj, cos, sin, sinks):
    L = proj.shape[0]

    def body(q_ref, kp_ref, kc_ref, vp_ref, vc_ref, cp_ref, cc_ref, sp_ref, sc_ref, sink_ref, o_ref):
        n = pl.program_id(0)
        combos, parts = _swa_probs(n, q_ref, kp_ref, kc_ref, vp_ref, vc_ref, cp_ref, cc_ref, sp_ref, sc_ref, sink_ref)
        outs = {}
        for g, a in combos:
            qst, ksel, vsel, probs, psink, half = parts[g, a]
            outs[g, a] = _dot(probs.astype(BF16), jnp.where(half, vsel, 0.0).astype(BF16))
        for g in range(SWA_KV_HEADS):
            both = outs[g, 0] + outs[g, 1]
            for j in range(PAIRS_PER_KV):
                p = g * PAIRS_PER_KV + j
                o_ref[:, p * BLOCK:(p + 1) * BLOCK] = both[j * BLOCK:(j + 1) * BLOCK]

    return pl.pallas_call(
        body,
        name="swa_attention_fwd",
        grid=(L // BLOCK,),
        in_specs=_swa_specs(),
        out_specs=pl.BlockSpec((BLOCK, SWA_WIDTH), lambda n: (n, 0)),
        out_shape=jax.ShapeDtypeStruct((L, SWA_WIDTH), F32),
        compiler_params=_params(("arbitrary",)),
    )(proj, proj, proj, proj, proj, cos, cos, sin, sin, sinks)


def swa_attention_bwd(proj, cos, sin, sinks, do_sw):
    L = proj.shape[0]

    def body(q_ref, kp_ref, kc_ref, vp_ref, vc_ref, cp_ref, cc_ref, sp_ref, sc_ref, sink_ref, do_ref,
             dq_ref, dk_ref, dv_ref, ds_ref):
        n = pl.program_id(0)

        @pl.when(n == 0)
        def _():
            dk_ref[...] = jnp.zeros_like(dk_ref)
            dv_ref[...] = jnp.zeros_like(dv_ref)
            ds_ref[...] = jnp.zeros_like(ds_ref)

        combos, parts = _swa_probs(n, q_ref, kp_ref, kc_ref, vp_ref, vc_ref, cp_ref, cc_ref, sp_ref, sc_ref, sink_ref)
        lane8 = lax.broadcasted_iota(jnp.int32, (8, BLOCK), 1)
        dos, dps = {}, {}
        for g, a in combos:
            half = parts[g, a][5]
            dos[g, a] = jnp.concatenate(
                [jnp.where(half, do_ref[:, (g * PAIRS_PER_KV + j) * BLOCK:(g * PAIRS_PER_KV + j + 1) * BLOCK], 0.0)
                 for j in range(PAIRS_PER_KV)], axis=0).astype(BF16)
        for g, a in combos:
            dps[g, a] = _dot_nt(dos[g, a], parts[g, a][2].astype(BF16))
        dqs = {}
        dkb = jnp.zeros((2 * BLOCK, BLOCK), F32)
        dvb = jnp.zeros((2 * BLOCK, BLOCK), F32)
        dsk = jnp.zeros((8, BLOCK), F32)
        for g, a in combos:
            qst, ksel, vsel, probs, psink, half = parts[g, a]
            dp = dps[g, a]
            delta = _row_sums(probs * dp)
            ds = (probs * (dp - _twice(delta))).astype(BF16)
            pd = psink * delta
            for j in range(PAIRS_PER_KV):
                head = 2 * (g * PAIRS_PER_KV + j) + a
                dsk = dsk + jnp.where(lane8 == head, -jnp.sum(pd[j * BLOCK:(j + 1) * BLOCK, :1]), 0.0)
            dqs[g, a] = _dot(ds, jnp.where(half, ksel, 0.0).astype(BF16))
            dk_a = _dot_tn(ds, qst)
            dv_a = _dot_tn(probs.astype(BF16), dos[g, a])
            if g != a:
                dk_a = pltpu.roll(dk_a, HEAD_DIM, axis=1)
                dv_a = pltpu.roll(dv_a, HEAD_DIM, axis=1)
            dkb = dkb + dk_a
            dvb = dvb + dv_a
        cosc, sinc = cc_ref[...], sc_ref[...]
        for g in range(SWA_KV_HEADS):
            both = (dqs[g, 0] + dqs[g, 1]) * SCALE
            for j in range(PAIRS_PER_KV):
                p = g * PAIRS_PER_KV + j
                dq_ref[:, p * BLOCK:(p + 1) * BLOCK] = _unrope(both[j * BLOCK:(j + 1) * BLOCK], cosc, sinc).astype(BF16)
        ds_ref[...] += dsk
        cur = pl.ds(pl.multiple_of(n * BLOCK, BLOCK), BLOCK)
        dk_ref[cur, :] += _unrope(dkb[BLOCK:], cosc, sinc)
        dv_ref[cur, :] += dvb[BLOCK:]

        @pl.when(n > 0)
        def _():
            prv = pl.ds(pl.multiple_of((n - 1) * BLOCK, BLOCK), BLOCK)
            dk_ref[prv, :] += _unrope(dkb[:BLOCK], cp_ref[...], sp_ref[...])
            dv_ref[prv, :] += dvb[:BLOCK]

    whole = lambda n: (0, 0)
    row = pl.BlockSpec((BLOCK, SWA_WIDTH), lambda n: (n, 0))
    return pl.pallas_call(
        body,
        name="swa_attention_bwd",
        grid=(L // BLOCK,),
        in_specs=_swa_specs() + [row],
        out_specs=[row, pl.BlockSpec((L, BLOCK), whole), pl.BlockSpec((L, BLOCK), whole),
                   pl.BlockSpec((8, BLOCK), whole)],
        out_shape=[jax.ShapeDtypeStruct((L, SWA_WIDTH), BF16), jax.ShapeDtypeStruct((L, BLOCK), F32),
                   jax.ShapeDtypeStruct((L, BLOCK), F32), jax.ShapeDtypeStruct((8, BLOCK), F32)],
        compiler_params=_params(("arbitrary",)),
    )(proj, proj, proj, proj, proj, cos, cos, sin, sin, sinks, do_sw)


ROW_TILE = 640
TAIL_ROWS = 208


def _pick(n, cands):
    for c in cands:
        if n % c == 0:
            return c
    raise ValueError(f"no tile for {n}")


def in_proj(h0, gain, w, name, out_dtype):
    L, D = h0.shape
    N = w.shape[1]
    tm = _pick(L, (ROW_TILE, BLOCK))
    tn = _pick(N, (1792, 1536, 1280, 896, 640, 512, 384, 256, 128))

    def body(h_ref, g_ref, w_ref, o_ref, xn_ref):
        @pl.when(pl.program_id(1) == 0)
        def _():
            x = h_ref[...]
            r = lax.rsqrt(jnp.mean(x * x, axis=1, keepdims=True) + RMS_EPS)
            xn_ref[...] = ((x * r) * g_ref[...]).astype(BF16)
        o_ref[...] = _dot(xn_ref[...], w_ref[...]).astype(out_dtype)

    return pl.pallas_call(
        body,
        name=name,
        grid=(L // tm, N // tn),
        in_specs=[pl.BlockSpec((tm, D), lambda i, j: (i, 0)),
                  pl.BlockSpec((1, D), lambda i, j: (0, 0)),
                  pl.BlockSpec((D, tn), lambda i, j: (0, j))],
        out_specs=[pl.BlockSpec((tm, tn), lambda i, j: (i, j)),
                   pl.BlockSpec((tm, D), lambda i, j: (i, 0))],
        out_shape=[jax.ShapeDtypeStruct((L, N), out_dtype), jax.ShapeDtypeStruct((L, D), BF16)],
        compiler_params=_params(("arbitrary", "arbitrary")),
    )(h0, gain, w)


def matmul_tn(a, b, name):
    Kd, M = a.shape
    N = b.shape[1]
    tk = _pick(Kd, (ROW_TILE, BLOCK))
    tn = _pick(N, (1280, 1024, 896, 640, 512, 256, 128))
    nk = Kd // tk

    def body(a_ref, b_ref, o_ref):
        k = pl.program_id(1)

        @pl.when(k == 0)
        def _():
            o_ref[...] = jnp.zeros_like(o_ref)
        o_ref[...] += _dot_tn(a_ref[...], b_ref[...])

    return pl.pallas_call(
        body,
        name=name,
        grid=(N // tn, nk),
        in_specs=[pl.BlockSpec((tk, M), lambda j, k: (k, 0)),
                  pl.BlockSpec((tk, tn), lambda j, k: (k, j))],
        out_specs=pl.BlockSpec((M, tn), lambda j, k: (0, j)),
        out_shape=jax.ShapeDtypeStruct((M, N), F32),
        compiler_params=_params(("arbitrary", "arbitrary")),
    )(a, b)


def in_proj_bwd(dproj, w, h0, gain, dh1):
    L, N = dproj.shape
    D = w.shape[0]
    tm = _pick(L, (ROW_TILE, BLOCK))
    tk = _pick(N, (3200, 1280, 640, 512, 256, 128))
    nk = N // tk

    def body(dp_ref, w_ref, h_ref, g_ref, dh1_ref, dh0_ref, dg_ref, acc_ref):
        i, k = pl.program_id(0), pl.program_id(1)

        @pl.when(k == 0)
        def _():
            acc_ref[...] = jnp.zeros_like(acc_ref)

        @pl.when((i == 0) & (k == 0))
        def _():
            dg_ref[...] = jnp.zeros_like(dg_ref)

        acc_ref[...] += _dot_nt(dp_ref[...], w_ref[...])

        @pl.when(k == nk - 1)
        def _():
            x = h_ref[...]
            r = lax.rsqrt(jnp.mean(x * x, axis=1, keepdims=True) + RMS_EPS)
            xhat = x * r
            dxn = acc_ref[...]
            dg_ref[...] += jnp.sum(dxn * xhat, axis=0, keepdims=True)
            dxh = dxn * g_ref[...]
            dh0_ref[...] = r * (dxh - xhat * jnp.mean(dxh * xhat, axis=1, keepdims=True)) + dh1_ref[...]

    row = pl.BlockSpec((tm, D), lambda i, k: (i, 0))
    vec = pl.BlockSpec((1, D), lambda i, k: (0, 0))
    return pl.pallas_call(
        body,
        name="in_proj_bwd",
        grid=(L // tm, nk),
        in_specs=[pl.BlockSpec((tm, tk), lambda i, k: (i, k)),
                  pl.BlockSpec((D, tk), lambda i, k: (0, k)),
                  row, vec, row],
        out_specs=[row, vec],
        out_shape=[jax.ShapeDtypeStruct((L, D), F32), jax.ShapeDtypeStruct((1, D), F32)],
        scratch_shapes=[pltpu.VMEM((tm, D), F32)],
        compiler_params=_params(("arbitrary", "arbitrary")),
    )(dproj, w, h0, gain, dh1)


def tail_fwd_bwd(h0, tgt, o_sb, o_sw, gates, w_bsb, w_bswa, w_out, gain_f):
    L, D = h0.shape
    R = _pick(L, (TAIL_ROWS, BLOCK))
    z0, z1, z2, z3 = 0, SB_WIDTH, SB_WIDTH + SWA_WIDTH, SB_WIDTH + SWA_WIDTH + D_MODEL

    def body(h_ref, t_ref, osb_ref, osw_ref, g_ref, wsb_ref, wsw_ref, wo_ref, gf_ref,
             dosb_ref, dosw_ref, dg_ref, dh1_ref, mb_ref, usb_ref, usw_ref, dh1b_ref, dysb_ref, dysw_ref,
             dgf_ref, loss_ref):
        i = pl.program_id(0)

        @pl.when(i == 0)
        def _():
            dgf_ref[...] = jnp.zeros_like(dgf_ref)
            loss_ref[...] = jnp.zeros_like(loss_ref)

        sbz = g_ref[:, z0:z1]
        swz = g_ref[:, z1:z2]
        s1 = jax.nn.sigmoid(g_ref[:, z2:z3])
        s2 = jax.nn.sigmoid(g_ref[:, z3:])
        sg_sb = jax.nn.sigmoid(sbz)
        sg_sw = jax.nn.sigmoid(swz)
        silu_sb = sbz * sg_sb
        silu_sw = swz * sg_sw
        osb = osb_ref[...]
        osw = osw_ref[...]
        usb = (osb * silu_sb).astype(BF16)
        usw = (osw * silu_sw).astype(BF16)
        y_sb = _dot(usb, wsb_ref[...])
        y_sw = _dot(usw, wsw_ref[...])
        mb = (s1 * y_sb + s2 * y_sw).astype(BF16)
        h1 = h_ref[...] + _dot(mb, wo_ref[...])
        rf = lax.rsqrt(jnp.mean(h1 * h1, axis=1, keepdims=True) + RMS_EPS)
        hhat = h1 * rf
        gf = gf_ref[...]
        row = i * R + lax.broadcasted_iota(jnp.int32, (R, 1), 0)
        err = jnp.where(row >= BLOCK, hhat * gf - t_ref[...], 0.0)
        lane0 = (lax.broadcasted_iota(jnp.int32, (8, BLOCK), 0) == 0) & (lax.broadcasted_iota(jnp.int32, (8, BLOCK), 1) == 0)
        loss_ref[...] += jnp.where(lane0, (0.5 / D) * jnp.sum(err * err), 0.0)
        dy = err * (1.0 / D)
        dgf_ref[...] += jnp.sum(dy * hhat, axis=0, keepdims=True)
        dhh = dy * gf
        dh1 = rf * (dhh - hhat * jnp.mean(dhh * hhat, axis=1, keepdims=True))
        dh1b = dh1.astype(BF16)
        dm = _dot_nt(dh1b, wo_ref[...])
        dysb = (dm * s1).astype(BF16)
        dysw = (dm * s2).astype(BF16)
        dusb = _dot_nt(dysb, wsb_ref[...])
        dusw = _dot_nt(dysw, wsw_ref[...])
        dosb_ref[...] = (dusb * silu_sb).astype(BF16)
        dosw_ref[...] = (dusw * silu_sw).astype(BF16)
        dg_ref[:, z0:z1] = (dusb * osb * (sg_sb * (1.0 + sbz * (1.0 - sg_sb)))).astype(BF16)
        dg_ref[:, z1:z2] = (dusw * osw * (sg_sw * (1.0 + swz * (1.0 - sg_sw)))).astype(BF16)
        dg_ref[:, z2:z3] = (dm * y_sb * (s1 * (1.0 - s1))).astype(BF16)
        dg_ref[:, z3:] = (dm * y_sw * (s2 * (1.0 - s2))).astype(BF16)
        dh1_ref[...] = dh1
        mb_ref[...] = mb
        usb_ref[...] = usb
        usw_ref[...] = usw
        dh1b_ref[...] = dh1b
        dysb_ref[...] = dysb
        dysw_ref[...] = dysw

    def rows(n):
        return pl.BlockSpec((R, n), lambda i: (i, 0))

    def whole(shape):
        return pl.BlockSpec(shape, lambda i: (0, 0))

    GW = gates.shape[1]
    return pl.pallas_call(
        body,
        name="tail_fwd_bwd",
        grid=(L // R,),
        in_specs=[rows(D), rows(D), rows(SB_WIDTH), rows(SWA_WIDTH), rows(GW),
                  whole(w_bsb.shape), whole(w_bswa.shape), whole(w_out.shape), whole((1, D))],
        out_specs=[rows(SB_WIDTH), rows(SWA_WIDTH), rows(GW), rows(D),
                   rows(D), rows(SB_WIDTH), rows(SWA_WIDTH), rows(D), rows(D), rows(D),
                   whole((1, D)), whole((8, BLOCK))],
        out_shape=[jax.ShapeDtypeStruct((L, SB_WIDTH), BF16), jax.ShapeDtypeStruct((L, SWA_WIDTH), BF16),
                   jax.ShapeDtypeStruct((L, GW), BF16), jax.ShapeDtypeStruct((L, D), F32),
                   jax.ShapeDtypeStruct((L, D), BF16), jax.ShapeDtypeStruct((L, SB_WIDTH), BF16),
                   jax.ShapeDtypeStruct((L, SWA_WIDTH), BF16), jax.ShapeDtypeStruct((L, D), BF16),
                   jax.ShapeDtypeStruct((L, D), BF16), jax.ShapeDtypeStruct((L, D), BF16),
                   jax.ShapeDtypeStruct((1, D), F32), jax.ShapeDtypeStruct((8, BLOCK), F32)],
        compiler_params=_params(("arbitrary",)),
    )(h0, tgt, o_sb, o_sw, gates, w_bsb, w_bswa, w_out, gain_f)


def local_step(x, tgt, meta, gain, w_in, w_bsb, w_bswa, w_out, sinks, gain_f):
    S, D = x.shape
    L = S + BLOCK
    h0 = jnp.concatenate([jnp.zeros((PAD, D), F32), meta, x], axis=0)
    tgt_p = jnp.concatenate([jnp.zeros((BLOCK, D), F32), tgt], axis=0)
    tt = _suffix_matrix()
    cos, sin = rope_tables(L)
    qkv, xn = in_proj(h0, gain, w_in[:, :SB_COLS], "in_proj_sb", BF16)
    proj_sw, _ = in_proj(h0, gain, w_in[:, SB_COLS:GATE_COL0], "in_proj_swa", F32)
    gates, _ = in_proj(h0, gain, w_in[:, GATE_COL0:], "in_proj_gates", F32)
    o_sb = sb_attention_fwd(qkv, tt)
    o_sw = swa_attention_fwd(proj_sw, cos, sin, sinks)
    (do_sb, do_sw, dgates, dh1, mb, usb, usw, dh1b, dysb, dysw, dgf, loss) = tail_fwd_bwd(
        h0, tgt_p, o_sb, o_sw, gates, w_bsb, w_bswa, w_out, gain_f)
    dq_sb, dk_sb, dv_sb = sb_attention_bwd(qkv, o_sb, do_sb, tt)
    dq_sw, dk_sw, dv_sw, dsinks = swa_attention_bwd(proj_sw, cos, sin, sinks, do_sw)
    dproj = jnp.concatenate([dq_sb, dk_sb, dv_sb, dq_sw, dk_sw.astype(BF16), dv_sw.astype(BF16), dgates], axis=1)
    dw_in = matmul_tn(xn, dproj, "dw_in")
    dw_out = matmul_tn(mb, dh1b, "dw_out")
    dw_bsb = matmul_tn(usb, dysb, "dw_bsb")
    dw_bswa = matmul_tn(usw, dysw, "dw_bswa")
    dh0, dgain = in_proj_bwd(dproj, w_in, h0, gain, dh1)
    return (loss[0, 0], dh0[BLOCK:], dh0[PAD:BLOCK], dgain, dw_in, dw_bsb, dw_bswa, dw_out,
            dsinks[:1, :SWA_Q_HEADS], dgf)


MESH_IDS = pl.DeviceIdType.MESH
ANY = pl.BlockSpec(memory_space=pl.ANY)


def _place():
    return lax.axis_index("x"), lax.axis_index("y"), lax.axis_index("c")


def _index(x, y, c):
    return 4 * x + 2 * y + c


def all_gather(block, name):
    def body(x_ref, out_ref, send_sems, recv_sems, local_sem):
        x, y, c = _place()
        me, sibling = (x, y, c), (x, y, 1 - c)
        chips = [(1 - x, y), (x, 1 - y), (1 - x, 1 - y)]

        def copy(k, blk, to, src=None):
            dst = out_ref.at[_index(*blk)]
            return pltpu.make_async_remote_copy(
                src_ref=dst if src is None else src, dst_ref=dst,
                send_sem=send_sems.at[k], recv_sem=recv_sems.at[k], device_id=to, device_id_type=MESH_IDS)

        mine = pltpu.make_async_copy(x_ref, out_ref.at[_index(*me)], local_sem)
        mine.start()
        first = [copy(0, me, sibling, src=x_ref)]
        first += [copy(1 + j, me, (*chip, c), src=x_ref) for j, chip in enumerate(chips)]
        for cp in first:
            cp.start()
        passed = [copy(4 + j, (*chip, c), sibling) for j, chip in enumerate(chips)]
        for j, chip in enumerate(chips):
            copy(1 + j, (*chip, c), me).wait_recv()
            passed[j].start()
        copy(0, sibling, me).wait_recv()
        for j, chip in enumerate(chips):
            copy(4 + j, (*chip, 1 - c), me).wait_recv()
        for cp in first + passed:
            cp.wait_send()
        mine.wait()

    return pl.pallas_call(
        body,
        name=name,
        out_shape=jax.ShapeDtypeStruct((N_DEV,) + block.shape, block.dtype),
        in_specs=[ANY],
        out_specs=ANY,
        scratch_shapes=[pltpu.SemaphoreType.DMA((7,)), pltpu.SemaphoreType.DMA((7,)), pltpu.SemaphoreType.DMA],
    )(block)


def exchange_partials(parts):
    def body(g_ref, out_ref, send_sems, recv_sems, local_sem):
        x, y, c = _place()
        me = _index(x, y, c)
        mine = pltpu.make_async_copy(g_ref.at[me], out_ref.at[me], local_sem)
        mine.start()
        copies = []
        for m in range(1, N_DEV):
            px = 1 - x if m & 4 else x
            py = 1 - y if m & 2 else y
            pc = 1 - c if m & 1 else c
            cp = pltpu.make_async_remote_copy(
                src_ref=g_ref.at[_index(px, py, pc)], dst_ref=out_ref.at[me],
                send_sem=send_sems.at[m - 1], recv_sem=recv_sems.at[m - 1],
                device_id=(px, py, pc), device_id_type=MESH_IDS)
            cp.start()
            copies.append(cp)
        for cp in copies:
            cp.wait()
        mine.wait()

    return pl.pallas_call(
        body,
        name="exchange_partials",
        out_shape=jax.ShapeDtypeStruct(parts.shape, parts.dtype),
        in_specs=[ANY],
        out_specs=ANY,
        scratch_shapes=[pltpu.SemaphoreType.DMA((7,)), pltpu.SemaphoreType.DMA((7,)), pltpu.SemaphoreType.DMA],
    )(parts)


def _adamw(w, g, m, v):
    m = ADAM_B1 * m + (1.0 - ADAM_B1) * g
    v = ADAM_B2 * v + (1.0 - ADAM_B2) * (g * g)
    m_hat = m / (1.0 - ADAM_B1 ** ADAM_STEP)
    v_hat = v / (1.0 - ADAM_B2 ** ADAM_STEP)
    delta = -ADAM_LR * (m_hat / (jnp.sqrt(v_hat) + ADAM_EPS) + ADAM_WD * w)
    return delta, m, v


def sum_and_adamw(parts, w, m, v, name):
    _, R, C = parts.shape
    tr = _pick(R, (528, 512, 256, 128, 24, 8))

    def body(p_ref, w_ref, m_ref, v_ref, g_ref, d_ref, nm_ref, nv_ref):
        g = p_ref[0].astype(F32)
        for s in range(1, N_DEV):
            g = g + p_ref[s].astype(F32)
        d, nm, nv = _adamw(w_ref[...], g, m_ref[...], v_ref[...])
        g_ref[...] = g
        d_ref[...] = d
        nm_ref[...] = nm
        nv_ref[...] = nv

    row = pl.BlockSpec((tr, C), lambda i: (i, 0))
    return pl.pallas_call(
        body,
        name=name,
        grid=(R // tr,),
        in_specs=[pl.BlockSpec((N_DEV, tr, C), lambda i: (0, i, 0)), row, row, row],
        out_specs=[row, row, row, row],
        out_shape=[jax.ShapeDtypeStruct((R, C), F32)] * 4,
        compiler_params=_params(("arbitrary",)),
    )(parts, w, m, v)


W_IN_SHARD = IN_COLS // N_DEV
ROWS_W_IN = D_MODEL * W_IN_SHARD // BLOCK
ROWS_W_BSB = SB_WIDTH
ROWS_W_ROWSHARD = D_MODEL
SMALL_ROWS = 24


def _pack_shards(w_in, w_bsb, w_bswa, w_out, meta):
    return jnp.concatenate([w_in.reshape(ROWS_W_IN, BLOCK), w_bsb.reshape(ROWS_W_BSB, BLOCK),
                            w_bswa.reshape(ROWS_W_ROWSHARD, BLOCK), w_out.reshape(ROWS_W_ROWSHARD, BLOCK),
                            meta.reshape(N_META, BLOCK)], axis=0)


def _unpack_shards(p):
    o = np.cumsum([0, ROWS_W_IN, ROWS_W_BSB, ROWS_W_ROWSHARD, ROWS_W_ROWSHARD, N_META])
    return (p[o[0]:o[1]].reshape(1, D_MODEL, W_IN_SHARD), p[o[1]:o[2]].reshape(1, SB_WIDTH, BLOCK),
            p[o[2]:o[3]].reshape(1, BLOCK, D_MODEL), p[o[3]:o[4]].reshape(1, BLOCK, D_MODEL),
            p[o[4]:o[5]].reshape(N_META, BLOCK))


def _pack_by_owner(dw_in, dw_bsb, dw_bswa, dw_out, dmeta):
    cols = lambda a: a.reshape(a.shape[0], N_DEV, -1).transpose(1, 0, 2)
    return jnp.concatenate([cols(dw_in).reshape(N_DEV, ROWS_W_IN, BLOCK), cols(dw_bsb),
                            dw_bswa.reshape(N_DEV, ROWS_W_ROWSHARD, BLOCK), dw_out.reshape(N_DEV, ROWS_W_ROWSHARD, BLOCK),
                            cols(dmeta)], axis=1)


def _pack_small(gain, gain_f, sinks, loss):
    z = jnp.zeros((SMALL_ROWS - 16, BLOCK), F32)
    z = z.at[0, :SWA_Q_HEADS].set(sinks.reshape(-1)).at[1, 0].set(loss)
    return jnp.concatenate([gain.reshape(8, BLOCK), gain_f.reshape(8, BLOCK), z], axis=0)


def _unpack_small(p):
    return p[0:8].reshape(1, D_MODEL), p[8:16].reshape(D_MODEL), p[16:17, :SWA_Q_HEADS], p[17, 0]


def kernel(x, meta_tokens, norm_gain, w_in, w_branch_sb, w_branch_swa, w_out, attn_sinks, final_norm_gain, loss_target, m_meta_tokens, m_norm_gain, m_w_in, m_w_branch_sb, m_w_branch_swa, m_w_out, m_attn_sinks, m_final_norm_gain, v_meta_tokens, v_norm_gain, v_w_in, v_w_branch_sb, v_w_branch_swa, v_w_out, v_attn_sinks, v_final_norm_gain):
    meta_bits = lax.bitcast_convert_type(meta_tokens, BF16).reshape(2 * N_META, BLOCK)
    mine = jnp.concatenate([_pack_shards(w_in, w_branch_sb, w_branch_swa, w_out, meta_tokens)[:-N_META].astype(BF16),
                            meta_bits], axis=0)
    full = all_gather(mine, "all_gather_weights")
    o = np.cumsum([0, ROWS_W_IN, ROWS_W_BSB, ROWS_W_ROWSHARD, ROWS_W_ROWSHARD, 2 * N_META])
    cols = lambda a: a.transpose(1, 0, 2).reshape(a.shape[1], -1)
    f_w_in = cols(full[:, o[0]:o[1]].reshape(N_DEV, D_MODEL, W_IN_SHARD))
    f_w_bsb = cols(full[:, o[1]:o[2]])
    f_w_bswa = full[:, o[2]:o[3]].reshape(D_MODEL, D_MODEL)
    f_w_out = full[:, o[3]:o[4]].reshape(D_MODEL, D_MODEL)
    f_meta = cols(lax.bitcast_convert_type(full[:, o[4]:o[5]].reshape(N_DEV, N_META, BLOCK, 2), F32))

    (loss, grad_x, dmeta, dgain, dw_in, dw_bsb, dw_bswa, dw_out, dsinks, dgf) = local_step(
        x[0], loss_target[0], f_meta, norm_gain, f_w_in, f_w_bsb, f_w_bswa, f_w_out, attn_sinks,
        final_norm_gain.reshape(1, D_MODEL))

    parts = exchange_partials(_pack_by_owner(dw_in, dw_bsb, dw_bswa, dw_out, dmeta).astype(BF16))
    packs = [_pack_shards(a[0], b[0], c[0], d[0], e) for a, b, c, d, e in (
        (w_in, w_branch_sb, w_branch_swa, w_out, meta_tokens),
        (m_w_in, m_w_branch_sb, m_w_branch_swa, m_w_out, m_meta_tokens),
        (v_w_in, v_w_branch_sb, v_w_branch_swa, v_w_out, v_meta_tokens))]
    big = [_unpack_shards(p) for p in sum_and_adamw(parts, *packs, "sum_adamw_sharded")]

    small = all_gather(_pack_small(dgain, dgf, dsinks, loss), "all_gather_small")
    zero = jnp.zeros((), F32)
    spacks = [_pack_small(a, b, c, zero) for a, b, c in (
        (norm_gain, final_norm_gain, attn_sinks), (m_norm_gain, m_final_norm_gain, m_attn_sinks),
        (v_norm_gain, v_final_norm_gain, v_attn_sinks))]
    sm = [_unpack_small(p) for p in sum_and_adamw(small, *spacks, "sum_adamw_replicated")]

    def leaves(k):
        b, s = big[k], sm[k]
        return (b[4], s[0], b[0], b[1], b[2], b[3], s[2], s[1])

    return (sm[0][3], grad_x[None], *leaves(0), *leaves(1), *leaves(2), *leaves(3))
```

```python
import numpy as np
import jax
import jax.numpy as jnp
from jax import lax
from jax.experimental import pallas as pl
from jax.experimental.pallas import tpu as pltpu

F32 = jnp.float32
BF16 = jnp.bfloat16

D_MODEL = 1024
N_META = 16
BLOCK = 128
PAD = BLOCK - N_META
HEAD_DIM = 64
SB_HEADS = 8
SB_WIDTH = SB_HEADS * HEAD_DIM
SWA_Q_HEADS = 16
SWA_KV_HEADS = 2
SWA_WIDTH = SWA_Q_HEADS * HEAD_DIM
SWA_KV_WIDTH = SWA_KV_HEADS * HEAD_DIM
ROPE_THETA = 10000.0
RMS_EPS = 1e-6
SCALE = HEAD_DIM ** -0.5
SPLITS = (SB_WIDTH, SB_WIDTH, SB_WIDTH, SWA_WIDTH, SWA_KV_WIDTH, SWA_KV_WIDTH,
          SB_WIDTH, SWA_WIDTH, D_MODEL, D_MODEL)
IN_COLS = sum(SPLITS)
SB_COLS = 3 * SB_WIDTH
SWA_COLS = SWA_WIDTH + 2 * SWA_KV_WIDTH
GATE_COL0 = SB_COLS + SWA_COLS

N_DEV = 8
ADAM_LR = 0.001
ADAM_B1 = 0.9
ADAM_B2 = 0.999
ADAM_EPS = 1e-08
ADAM_WD = 0.01
ADAM_STEP = 10

VMEM_LIMIT = 56 * 1024 * 1024


def _params(sem, **kw):
    return pltpu.CompilerParams(dimension_semantics=sem, vmem_limit_bytes=VMEM_LIMIT, **kw)


def _dot(a, b):
    return jnp.dot(a, b, preferred_element_type=F32)


def _dot_nt(a, b):
    return lax.dot_general(a, b, (((1,), (1,)), ((), ())), preferred_element_type=F32)


def _dot_tn(a, b):
    return lax.dot_general(a, b, (((0,), (0,)), ((), ())), preferred_element_type=F32)


def _cat(xs, axis):
    return xs[0] if len(xs) == 1 else jnp.concatenate(xs, axis=axis)


def _split_bf16(x):
    hi = x.astype(BF16)
    lo = (x - hi.astype(F32)).astype(BF16)
    return jnp.concatenate([hi, lo], axis=1)


def _suffix_matrix():
    j = np.arange(BLOCK)[:, None]
    s = np.arange(BLOCK)[None, :]
    t = np.concatenate([(j >= s).astype(np.float32), np.ones((BLOCK, BLOCK), np.float32)], axis=1)
    return jnp.asarray(np.concatenate([t, t], axis=0), dtype=BF16)


def _softplus(z):
    return jnp.maximum(z, 0.0) + jnp.log(1.0 + jnp.exp(-jnp.abs(z)))


SB_SMALL = 4
SB_PAIRS_FWD = 4
SB_PAIRS_BWD = 2


def _sb_masks(i, nblk):
    if nblk == 0:
        r = lax.broadcasted_iota(jnp.int32, (BLOCK, BLOCK), 0)
        c = lax.broadcasted_iota(jnp.int32, (BLOCK, BLOCK), 1)
        return (c < r) & (i * BLOCK + c >= PAD)
    return lax.broadcasted_iota(jnp.int32, (BLOCK, nblk * BLOCK), 1) >= PAD


def _sb_heads(npairs):
    return [(pr, a) for pr in range(npairs) for a in range(2)]


def _lanes(pr):
    return slice(pr * BLOCK, (pr + 1) * BLOCK)


def _masked_heads(x, half0, npairs):
    out = []
    for pr, a in _sb_heads(npairs):
        xp = x[:, _lanes(pr)]
        out.append((jnp.where(half0, xp, 0.0) if a == 0 else jnp.where(half0, 0.0, xp)).astype(BF16))
    return out


def _sb_sweep(i, tile):
    same = lambda alive: alive
    small = SB_SMALL
    live_tile = tile
    tile = lambda j0, nblk, mask, alive: lax.cond(alive, lambda _: live_tile(j0, nblk, mask), same, alive)
    alive = live_tile(i, 1, _sb_masks(i, 0))
    alive = lax.switch(jnp.minimum(i, 3), [
        same,
        lambda al: tile(0, 1, _sb_masks(i, 1), al),
        lambda al: tile(0, 2, _sb_masks(i, 2), al),
        lambda al: tile(i - 2, 2, None, al)], alive)

    def below(alive):
        rest = i - 2
        n_grp = rest // small
        n_one = rest - n_grp * small
        low_is_one = (n_grp == 0) & (n_one > 0)
        n_plain = n_one - jnp.where(low_is_one, 1, 0)
        alive = lax.fori_loop(0, n_plain, lambda t, al: tile(rest - 1 - t, 1, None, al), alive)
        alive = lax.cond(low_is_one, lambda al: tile(0, 1, _sb_masks(i, 1), al), same, alive)
        alive = lax.fori_loop(0, jnp.maximum(n_grp - 1, 0),
                              lambda t, al: tile((n_grp - 1 - t) * small, small, None, al), alive)
        return lax.cond(n_grp > 0, lambda al: tile(0, small, _sb_masks(i, small), al), same, alive)

    lax.cond(alive & (i > 2), below, same, alive)


def _sb_weights(zs, ss, cs, tt, nblk, mask):
    suf = _dot(_cat([_split_bf16(s[:, b * BLOCK:(b + 1) * BLOCK]) for s in ss for b in range(nblk)], 0), tt)
    ws, out_cs = [], []
    for h in range(len(zs)):
        c = cs[h]
        wb = [None] * nblk
        for b in reversed(range(nblk)):
            sab = suf[(h * nblk + b) * BLOCK:(h * nblk + b + 1) * BLOCK]
            wb[b] = jnp.exp(zs[h][:, b * BLOCK:(b + 1) * BLOCK] + c - sab[:, :BLOCK])
            c = c - sab[:, BLOCK:]
        w = _cat(wb, 1)
        if mask is not None:
            w = jnp.where(mask, w, 0.0)
        ws.append(w)
        out_cs.append(c)
    return ws, out_cs


SB_DEAD = -110.0


def _row_norm_max(x, half0, a):
    xf = x.astype(F32)
    sq = jnp.where(half0, xf * xf, 0.0) if a == 0 else jnp.where(half0, 0.0, xf * xf)
    return jnp.sqrt(jnp.max(jnp.sum(sq, axis=1, keepdims=True)))


def _sb_logit_bounds(i, q_heads, k_ref, kmax_ref, half0, heads):
    @pl.when(i == 0)
    def _():
        for h, (pr, a) in enumerate(heads):
            kmax_ref[h] = _row_norm_max(k_ref[:, _lanes(pr)], half0, a)
    return [_row_norm_max(q_heads[h], half0, a) * kmax_ref[h] * 1.001 + 0.01 for h, (pr, a) in enumerate(heads)]


def _sb_alive(cs, zmax):
    worst = cs[0] + zmax[0]
    for c, zm in zip(cs[1:], zmax[1:]):
        worst = jnp.maximum(worst, c + zm)
    return jnp.max(worst) > SB_DEAD


def sb_attention_fwd(qkv, tt):
    L = qkv.shape[0]
    nb = L // BLOCK
    NP = SB_PAIRS_FWD
    SB_W = NP * BLOCK
    nq = SB_WIDTH // SB_W

    def body(q_ref, k_ref, v_ref, tt_ref, o_ref, kmax_ref, acc_ref, c_ref):
        i = pl.program_id(1)
        half0 = lax.broadcasted_iota(jnp.int32, (1, BLOCK), 1) < HEAD_DIM
        heads = _sb_heads(NP)
        qh = _masked_heads(q_ref[...].astype(F32) * SCALE, half0, NP)
        zmax = _sb_logit_bounds(i, qh, k_ref, kmax_ref, half0, heads)
        acc_ref[...] = jnp.zeros_like(acc_ref)
        c_ref[...] = jnp.zeros_like(c_ref)

        def rows_of(j0, nblk):
            return pl.ds(pl.multiple_of(j0 * BLOCK, BLOCK), nblk * BLOCK)

        def tile(j0, nblk, mask):
            rows = rows_of(j0, nblk)
            zs = [_dot_nt(qh[h], k_ref[rows, _lanes(pr)]) for h, (pr, a) in enumerate(heads)]
            ss = [_softplus(z) for z in zs]
            if mask is not None:
                ss = [jnp.where(mask, s, 0.0) for s in ss]
            ws, cs = _sb_weights(zs, ss, [c_ref[h] for h in range(len(heads))], tt_ref[...], nblk, mask)
            for h, (pr, a) in enumerate(heads):
                acc_ref[h] += _dot(ws[h].astype(BF16), v_ref[rows, _lanes(pr)])
                c_ref[h] = cs[h]
            return _sb_alive(cs, zmax)

        _sb_sweep(i, tile)
        o_ref[...] = _cat([jnp.where(half0, acc_ref[2 * pr], acc_ref[2 * pr + 1]) for pr in range(NP)], 1)

    panel = lambda c: pl.BlockSpec((L, SB_W), lambda p, i: (0, c * nq + p), pipeline_mode=pl.Buffered(1))
    return pl.pallas_call(
        body,
        name="sb_attention_fwd",
        grid=(nq, nb),
        in_specs=[
            pl.BlockSpec((BLOCK, SB_W), lambda p, i: (i, p)),
            panel(1), panel(2),
            pl.BlockSpec((2 * BLOCK, 2 * BLOCK), lambda p, i: (0, 0)),
        ],
        out_specs=pl.BlockSpec((BLOCK, SB_W), lambda p, i: (i, p)),
        out_shape=jax.ShapeDtypeStruct((L, SB_WIDTH), F32),
        scratch_shapes=[pltpu.SMEM((2 * NP,), F32), pltpu.VMEM((2 * NP, BLOCK, BLOCK), F32),
                        pltpu.VMEM((2 * NP, BLOCK, BLOCK), F32)],
        compiler_params=_params(("arbitrary", "arbitrary")),
    )(qkv, qkv, qkv, tt)


def sb_attention_bwd(qkv, o_sb, do_sb, tt):
    L = qkv.shape[0]
    nb = L // BLOCK
    NP = SB_PAIRS_BWD
    SB_W = NP * BLOCK
    nq = SB_WIDTH // SB_W

    def body(q_ref, k_ref, v_ref, o_ref, do_ref, tt_ref, dq_ref, dk_ref, dv_ref, dk_acc, dv_acc, kmax_ref,
             acc_ref, c_ref, ce_ref):
        i = pl.program_id(1)
        half0 = lax.broadcasted_iota(jnp.int32, (1, BLOCK), 1) < HEAD_DIM
        heads = _sb_heads(NP)

        @pl.when(i == 0)
        def _():
            dk_acc[...] = jnp.zeros_like(dk_acc)
            dv_acc[...] = jnp.zeros_like(dv_acc)

        qh = _masked_heads(q_ref[...].astype(F32) * SCALE, half0, NP)
        zmax = _sb_logit_bounds(i, qh, k_ref, kmax_ref, half0, heads)
        do = do_ref[...]
        doh = _masked_heads(do, half0, NP)
        od = o_ref[...] * do.astype(F32)
        acc_ref[...] = jnp.zeros_like(acc_ref)
        c_ref[...] = jnp.zeros_like(c_ref)
        for h, (pr, a) in enumerate(heads):
            x = od[:, _lanes(pr)]
            x = jnp.where(half0, x, 0.0) if a == 0 else jnp.where(half0, 0.0, x)
            ce_ref[h] = jnp.broadcast_to(jnp.sum(x, axis=1, keepdims=True), (BLOCK, BLOCK))

        def rows_of(j0, nblk):
            return pl.ds(pl.multiple_of(j0 * BLOCK, BLOCK), nblk * BLOCK)

        def tile(j0, nblk, mask):
            cs = [c_ref[h] for h in range(len(heads))]
            ces = [ce_ref[h] for h in range(len(heads))]
            rows = rows_of(j0, nblk)
            zs = [_dot_nt(qh[h], k_ref[rows, _lanes(pr)]) for h, (pr, a) in enumerate(heads)]
            dws = [_dot_nt(doh[h], v_ref[rows, _lanes(pr)]) for h, (pr, a) in enumerate(heads)]
            ss = [_softplus(z) for z in zs]
            sigs = [jnp.exp(z - s) for z, s in zip(zs, ss)]
            if mask is not None:
                ss = [jnp.where(mask, s, 0.0) for s in ss]
            ws, cs = _sb_weights(zs, ss, cs, tt_ref[...], nblk, mask)
            wbs = [w.astype(BF16) for w in ws]
            es = [wb.astype(F32) * dw for wb, dw in zip(wbs, dws)]
            esuf = _dot(_cat([_split_bf16(e[:, b * BLOCK:(b + 1) * BLOCK]) for e in es for b in range(nblk)], 0),
                        tt_ref[...])
            out_ces, dzbs = [], []
            for h, (pr, a) in enumerate(heads):
                ce = ces[h]
                dzs = [None] * nblk
                for b in reversed(range(nblk)):
                    eab = esuf[(h * nblk + b) * BLOCK:(h * nblk + b + 1) * BLOCK]
                    sl = slice(b * BLOCK, (b + 1) * BLOCK)
                    e = es[h][:, sl]
                    dzs[b] = e - sigs[h][:, sl] * (e + (ce - eab[:, :BLOCK]))
                    ce = ce - eab[:, BLOCK:]
                dz = _cat(dzs, 1)
                if mask is not None:
                    dz = jnp.where(mask, dz, 0.0)
                dzbs.append(dz.astype(BF16))
                out_ces.append(ce)
            for h, (pr, a) in enumerate(heads):
                acc_ref[h] += _dot(dzbs[h], k_ref[rows, _lanes(pr)])
                c_ref[h] = cs[h]
                ce_ref[h] = out_ces[h]
            dkv = [_dot_tn(_cat([dzbs[2 * pr], wbs[2 * pr], dzbs[2 * pr + 1], wbs[2 * pr + 1]], 0), qdo[pr])
                   for pr in range(NP)]
            dk_acc[rows, :] += _cat([x[:, :BLOCK] for x in dkv], 1)
            dv_acc[rows, :] += _cat([x[:, BLOCK:] for x in dkv], 1)
            return _sb_alive(cs, zmax)

        zb = jnp.zeros((BLOCK, BLOCK), BF16)
        qdo = [_cat([_cat([qh[h], zb], 1) if kind == 0 else _cat([zb, doh[h]], 1)
                     for h in (2 * pr, 2 * pr + 1) for kind in (0, 1)], 0) for pr in range(NP)]
        _sb_sweep(i, tile)
        dq_ref[...] = (_cat([jnp.where(half0, acc_ref[2 * pr], acc_ref[2 * pr + 1]) for pr in range(NP)], 1)
                       * SCALE).astype(BF16)

        @pl.when(i == nb - 1)
        def _():
            dk_ref[...] = dk_acc[...].astype(BF16)
            dv_ref[...] = dv_acc[...].astype(BF16)

    blk = pl.BlockSpec((BLOCK, SB_W), lambda p, i: (i, p))
    panel = pl.BlockSpec((L, SB_W), lambda p, i: (0, p))
    return pl.pallas_call(
        body,
        name="sb_attention_bwd",
        grid=(nq, nb),
        in_specs=[
            blk,
            pl.BlockSpec((L, SB_W), lambda p, i: (0, nq + p), pipeline_mode=pl.Buffered(1)),
            pl.BlockSpec((L, SB_W), lambda p, i: (0, 2 * nq + p), pipeline_mode=pl.Buffered(1)),
            blk, blk,
            pl.BlockSpec((2 * BLOCK, 2 * BLOCK), lambda p, i: (0, 0)),
        ],
        out_specs=[blk, panel, panel],
        out_shape=[jax.ShapeDtypeStruct((L, SB_WIDTH), BF16)] * 3,
        scratch_shapes=[pltpu.VMEM((L, SB_W), F32), pltpu.VMEM((L, SB_W), F32), pltpu.SMEM((2 * NP,), F32)]
        + [pltpu.VMEM((2 * NP, BLOCK, BLOCK), F32)] * 3,
        compiler_params=_params(("arbitrary", "arbitrary")),
    )(qkv, qkv, qkv, o_sb, do_sb, tt)


SWA_PAIRS = SWA_WIDTH // BLOCK
PAIRS_PER_KV = SWA_PAIRS // SWA_KV_HEADS
CB_SWK = SWA_PAIRS
CB_SWV = SWA_PAIRS + 1


def rope_tables(L):
    half = HEAD_DIM // 2
    inv = ROPE_THETA ** (-jnp.arange(half, dtype=F32) / half)
    pos = (jnp.arange(L) - PAD).astype(F32)
    ang = pos[:, None] * inv[None, :]
    reps = BLOCK // half
    return jnp.tile(jnp.cos(ang), (1, reps)), jnp.tile(jnp.sin(ang), (1, reps))


def _rot_half(x):
    lane = lax.broadcasted_iota(jnp.int32, (1, BLOCK), 1)
    first = (lane % HEAD_DIM) < (HEAD_DIM // 2)
    return jnp.where(first, -pltpu.roll(x, BLOCK - HEAD_DIM // 2, axis=1), pltpu.roll(x, HEAD_DIM // 2, axis=1))


def _rope(x, cos, sin):
    return x * cos + _rot_half(x) * sin


def _unrope(x, cos, sin):
    return x * cos - _rot_half(x) * sin


def _swa_specs():
    prev = lambda n: jnp.maximum(n - 1, 0)
    cur = lambda n: n
    blk = lambda f, c: pl.BlockSpec((BLOCK, BLOCK), lambda n: (f(n), c))
    return [
        pl.BlockSpec((BLOCK, SWA_WIDTH), lambda n: (n, 0)),
        blk(prev, CB_SWK), blk(cur, CB_SWK), blk(prev, CB_SWV), blk(cur, CB_SWV),
        blk(prev, 0), blk(cur, 0), blk(prev, 0), blk(cur, 0),
        pl.BlockSpec(memory_space=pltpu.SMEM),
    ]


def _swa_probs(n, q_ref, kp_ref, kc_ref, vp_ref, vc_ref, cp_ref, cc_ref, sp_ref, sc_ref, sink_ref):
    lane = lax.broadcasted_iota(jnp.int32, (1, BLOCK), 1)
    halves = (lane < HEAD_DIM, lane >= HEAD_DIM)
    cosc, sinc = cc_ref[...], sc_ref[...]
    qs = [_rope(q_ref[:, p * BLOCK:(p + 1) * BLOCK], cosc, sinc) * SCALE for p in range(SWA_PAIRS)]
    kb = jnp.concatenate([_rope(kp_ref[...], cp_ref[...], sp_ref[...]), _rope(kc_ref[...], cosc, sinc)], axis=0)
    vb = jnp.concatenate([vp_ref[...], vc_ref[...]], axis=0)
    kv = {True: (kb, vb), False: (pltpu.roll(kb, HEAD_DIM, axis=1), pltpu.roll(vb, HEAD_DIM, axis=1))}
    rows = PAIRS_PER_KV * BLOCK
    r = lax.broadcasted_iota(jnp.int32, (rows, 2 * BLOCK), 0) % BLOCK
    c = lax.broadcasted_iota(jnp.int32, (rows, 2 * BLOCK), 1)
    valid = (c > r) & (c <= r + BLOCK) & ((n - 1) * BLOCK + c >= PAD)
    combos = [(g, a) for g in range(SWA_KV_HEADS) for a in range(2)]
    qst, ksel, vsel, scores = {}, {}, {}, {}
    for g, a in combos:
        qst[g, a] = jnp.concatenate(
            [jnp.where(halves[a], qs[g * PAIRS_PER_KV + j], 0.0) for j in range(PAIRS_PER_KV)], axis=0).astype(BF16)
        ksel[g, a], vsel[g, a] = kv[g == a]
    for g, a in combos:
        scores[g, a] = _dot_nt(qst[g, a], ksel[g, a].astype(BF16))
    out = {}
    for g, a in combos:
        s = jnp.where(valid, scores[g, a], -1e30)
        sink = jnp.concatenate([jnp.full((BLOCK, BLOCK), sink_ref[0, 2 * (g * PAIRS_PER_KV + j) + a], F32)
                                for j in range(PAIRS_PER_KV)], axis=0)
        mx = jnp.maximum(jnp.max(s, axis=1, keepdims=True), sink)
        pe = jnp.exp(s - _twice(mx))
        es = jnp.exp(sink - mx)
        inv = 1.0 / (_row_sums(pe) + es)
        out[g, a] = (qst[g, a], ksel[g, a], vsel[g, a], pe * _twice(inv), es * inv, halves[a])
    return combos, out


def _twice(x):
    return jnp.concatenate([x, x], axis=1)


def _row_sums(x):
    return _dot(_split_bf16(x), jnp.ones((4 * BLOCK, BLOCK), BF16))


def swa_attention_fwd(proj, cos, sin, sinks):
    L = proj.shape[0]

    def body(q_ref, kp_ref, kc_ref, vp_ref, vc_ref, cp_ref, cc_ref, sp_ref, sc_ref, sink_ref, o_ref):
        n = pl.program_id(0)
        combos, parts = _swa_probs(n, q_ref, kp_ref, kc_ref, vp_ref, vc_ref, cp_ref, cc_ref, sp_ref, sc_ref, sink_ref)
        outs = {}
        for g, a in combos:
            qst, ksel, vsel, probs, psink, half = parts[g, a]
            outs[g, a] = _dot(probs.astype(BF16), jnp.where(half, vsel, 0.0).astype(BF16))
        for g in range(SWA_KV_HEADS):
            both = outs[g, 0] + outs[g, 1]
            for j in range(PAIRS_PER_KV):
                p = g * PAIRS_PER_KV + j
                o_ref[:, p * BLOCK:(p + 1) * BLOCK] = both[j * BLOCK:(j + 1) * BLOCK]

    return pl.pallas_call(
        body,
        name="swa_attention_fwd",
        grid=(L // BLOCK,),
        in_specs=_swa_specs(),
        out_specs=pl.BlockSpec((BLOCK, SWA_WIDTH), lambda n: (n, 0)),
        out_shape=jax.ShapeDtypeStruct((L, SWA_WIDTH), F32),
        compiler_params=_params(("arbitrary",)),
    )(proj, proj, proj, proj, proj, cos, cos, sin, sin, sinks)


def swa_attention_bwd(proj, cos, sin, sinks, do_sw):
    L = proj.shape[0]

    def body(q_ref, kp_ref, kc_ref, vp_ref, vc_ref, cp_ref, cc_ref, sp_ref, sc_ref, sink_ref, do_ref,
             dq_ref, dk_ref, dv_ref, ds_ref):
        n = pl.program_id(0)

        @pl.when(n == 0)
        def _():
            dk_ref[...] = jnp.zeros_like(dk_ref)
            dv_ref[...] = jnp.zeros_like(dv_ref)
            ds_ref[...] = jnp.zeros_like(ds_ref)

        combos, parts = _swa_probs(n, q_ref, kp_ref, kc_ref, vp_ref, vc_ref, cp_ref, cc_ref, sp_ref, sc_ref, sink_ref)
        lane8 = lax.broadcasted_iota(jnp.int32, (8, BLOCK), 1)
        dos, dps = {}, {}
        for g, a in combos:
            half = parts[g, a][5]
            dos[g, a] = jnp.concatenate(
                [jnp.where(half, do_ref[:, (g * PAIRS_PER_KV + j) * BLOCK:(g * PAIRS_PER_KV + j + 1) * BLOCK], 0.0)
                 for j in range(PAIRS_PER_KV)], axis=0).astype(BF16)
        for g, a in combos:
            dps[g, a] = _dot_nt(dos[g, a], parts[g, a][2].astype(BF16))
        dqs = {}
        dkb = jnp.zeros((2 * BLOCK, BLOCK), F32)
        dvb = jnp.zeros((2 * BLOCK, BLOCK), F32)
        dsk = jnp.zeros((8, BLOCK), F32)
        for g, a in combos:
            qst, ksel, vsel, probs, psink, half = parts[g, a]
            dp = dps[g, a]
            delta = _row_sums(probs * dp)
            ds = (probs * (dp - _twice(delta))).astype(BF16)
            pd = psink * delta
            for j in range(PAIRS_PER_KV):
                head = 2 * (g * PAIRS_PER_KV + j) + a
                dsk = dsk + jnp.where(lane8 == head, -jnp.sum(pd[j * BLOCK:(j + 1) * BLOCK, :1]), 0.0)
            dqs[g, a] = _dot(ds, jnp.where(half, ksel, 0.0).astype(BF16))
            dk_a = _dot_tn(ds, qst)
            dv_a = _dot_tn(probs.astype(BF16), dos[g, a])
            if g != a:
                dk_a = pltpu.roll(dk_a, HEAD_DIM, axis=1)
                dv_a = pltpu.roll(dv_a, HEAD_DIM, axis=1)
            dkb = dkb + dk_a
            dvb = dvb + dv_a
        cosc, sinc = cc_ref[...], sc_ref[...]
        for g in range(SWA_KV_HEADS):
            both = (dqs[g, 0] + dqs[g, 1]) * SCALE
            for j in range(PAIRS_PER_KV):
                p = g * PAIRS_PER_KV + j
                dq_ref[:, p * BLOCK:(p + 1) * BLOCK] = _unrope(both[j * BLOCK:(j + 1) * BLOCK], cosc, sinc).astype(BF16)
        ds_ref[...] += dsk
        cur = pl.ds(pl.multiple_of(n * BLOCK, BLOCK), BLOCK)
        dk_ref[cur, :] += _unrope(dkb[BLOCK:], cosc, sinc)
        dv_ref[cur, :] += dvb[BLOCK:]

        @pl.when(n > 0)
        def _():
            prv = pl.ds(pl.multiple_of((n - 1) * BLOCK, BLOCK), BLOCK)
            dk_ref[prv, :] += _unrope(dkb[:BLOCK], cp_ref[...], sp_ref[...])
            dv_ref[prv, :] += dvb[:BLOCK]

    whole = lambda n: (0, 0)
    row = pl.BlockSpec((BLOCK, SWA_WIDTH), lambda n: (n, 0))
    return pl.pallas_call(
        body,
        name="swa_attention_bwd",
        grid=(L // BLOCK,),
        in_specs=_swa_specs() + [row],
        out_specs=[row, pl.BlockSpec((L, BLOCK), whole), pl.BlockSpec((L, BLOCK), whole),
                   pl.BlockSpec((8, BLOCK), whole)],
        out_shape=[jax.ShapeDtypeStruct((L, SWA_WIDTH), BF16), jax.ShapeDtypeStruct((L, BLOCK), F32),
                   jax.ShapeDtypeStruct((L, BLOCK), F32), jax.ShapeDtypeStruct((8, BLOCK), F32)],
        compiler_params=_params(("arbitrary",)),
    )(proj, proj, proj, proj, proj, cos, cos, sin, sin, sinks, do_sw)


ROW_TILE = 640
TAIL_ROWS = 208


def _pick(n, cands):
    for c in cands:
        if n % c == 0:
            return c
    raise ValueError(f"no tile for {n}")


def in_proj(h0, gain, w, name, out_dtype):
    L, D = h0.shape
    N = w.shape[1]
    tm = _pick(L, (ROW_TILE, BLOCK))
    tn = _pick(N, (1792, 1536, 1280, 896, 640, 512, 384, 256, 128))

    def body(h_ref, g_ref, w_ref, o_ref, xn_ref):
        @pl.when(pl.program_id(1) == 0)
        def _():
            x = h_ref[...]
            r = lax.rsqrt(jnp.mean(x * x, axis=1, keepdims=True) + RMS_EPS)
            xn_ref[...] = ((x * r) * g_ref[...]).astype(BF16)
        o_ref[...] = _dot(xn_ref[...], w_ref[...]).astype(out_dtype)

    return pl.pallas_call(
        body,
        name=name,
        grid=(L // tm, N // tn),
        in_specs=[pl.BlockSpec((tm, D), lambda i, j: (i, 0)),
                  pl.BlockSpec((1, D), lambda i, j: (0, 0)),
                  pl.BlockSpec((D, tn), lambda i, j: (0, j))],
        out_specs=[pl.BlockSpec((tm, tn), lambda i, j: (i, j)),
                   pl.BlockSpec((tm, D), lambda i, j: (i, 0))],
        out_shape=[jax.ShapeDtypeStruct((L, N), out_dtype), jax.ShapeDtypeStruct((L, D), BF16)],
        compiler_params=_params(("arbitrary", "arbitrary")),
    )(h0, gain, w)


def rows_matmul(a, w, name):
    L, D = a.shape
    N = w.shape[1]
    tm = _pick(L, (ROW_TILE, BLOCK))
    tn = _pick(N, (1792, 1536, 1280, 896, 640, 512, 384, 256, 128))

    def body(a_ref, w_ref, o_ref):
        o_ref[...] = _dot(a_ref[...], w_ref[...])

    return pl.pallas_call(
        body,
        name=name,
        grid=(L // tm, N // tn),
        in_specs=[pl.BlockSpec((tm, D), lambda i, j: (i, 0)), pl.BlockSpec((D, tn), lambda i, j: (0, j))],
        out_specs=pl.BlockSpec((tm, tn), lambda i, j: (i, j)),
        out_shape=jax.ShapeDtypeStruct((L, N), F32),
        compiler_params=_params(("arbitrary", "arbitrary")),
    )(a, w)


def matmul_tn(a, b, name):
    Kd, M = a.shape
    N = b.shape[1]
    tk = _pick(Kd, (ROW_TILE, BLOCK))
    tn = _pick(N, (1280, 1024, 896, 640, 512, 256, 128))
    nk = Kd // tk

    def body(a_ref, b_ref, o_ref):
        k = pl.program_id(1)

        @pl.when(k == 0)
        def _():
            o_ref[...] = jnp.zeros_like(o_ref)
        o_ref[...] += _dot_tn(a_ref[...], b_ref[...])

    return pl.pallas_call(
        body,
        name=name,
        grid=(N // tn, nk),
        in_specs=[pl.BlockSpec((tk, M), lambda j, k: (k, 0)),
                  pl.BlockSpec((tk, tn), lambda j, k: (k, j))],
        out_specs=pl.BlockSpec((M, tn), lambda j, k: (0, j)),
        out_shape=jax.ShapeDtypeStruct((M, N), F32),
        compiler_params=_params(("arbitrary", "arbitrary")),
    )(a, b)


def in_proj_bwd(dproj, w, h0, gain, dh1):
    L, N = dproj.shape
    D = w.shape[0]
    tm = _pick(L, (ROW_TILE, BLOCK))
    tk = _pick(N, (3200, 1280, 640, 512, 256, 128))
    nk = N // tk

    def body(dp_ref, w_ref, h_ref, g_ref, dh1_ref, dh0_ref, dg_ref, acc_ref):
        i, k = pl.program_id(0), pl.program_id(1)

        @pl.when(k == 0)
        def _():
            acc_ref[...] = jnp.zeros_like(acc_ref)

        @pl.when((i == 0) & (k == 0))
        def _():
            dg_ref[...] = jnp.zeros_like(dg_ref)

        acc_ref[...] += _dot_nt(dp_ref[...], w_ref[...])

        @pl.when(k == nk - 1)
        def _():
            x = h_ref[...]
            r = lax.rsqrt(jnp.mean(x * x, axis=1, keepdims=True) + RMS_EPS)
            xhat = x * r
            dxn = acc_ref[...]
            dg_ref[...] += jnp.sum(dxn * xhat, axis=0, keepdims=True)
            dxh = dxn * g_ref[...]
            dh0_ref[...] = r * (dxh - xhat * jnp.mean(dxh * xhat, axis=1, keepdims=True)) + dh1_ref[...]

    row = pl.BlockSpec((tm, D), lambda i, k: (i, 0))
    vec = pl.BlockSpec((1, D), lambda i, k: (0, 0))
    return pl.pallas_call(
        body,
        name="in_proj_bwd",
        grid=(L // tm, nk),
        in_specs=[pl.BlockSpec((tm, tk), lambda i, k: (i, k)),
                  pl.BlockSpec((D, tk), lambda i, k: (0, k)),
                  row, vec, row],
        out_specs=[row, vec],
        out_shape=[jax.ShapeDtypeStruct((L, D), F32), jax.ShapeDtypeStruct((1, D), F32)],
        scratch_shapes=[pltpu.VMEM((tm, D), F32)],
        compiler_params=_params(("arbitrary", "arbitrary")),
    )(dproj, w, h0, gain, dh1)


def tail_fwd_bwd(h0, tgt, o_sb, o_sw, gates, w_bsb, w_bswa, w_out, gain_f):
    L, D = h0.shape
    R = _pick(L, (TAIL_ROWS, BLOCK))
    z0, z1, z2, z3 = 0, SB_WIDTH, SB_WIDTH + SWA_WIDTH, SB_WIDTH + SWA_WIDTH + D_MODEL

    def body(h_ref, t_ref, osb_ref, osw_ref, g_ref, wsb_ref, wsw_ref, wo_ref, gf_ref,
             dosb_ref, dosw_ref, dg_ref, dh1_ref, mb_ref, usb_ref, usw_ref, dh1b_ref, dysb_ref, dysw_ref,
             dgf_ref, loss_ref):
        i = pl.program_id(0)

        @pl.when(i == 0)
        def _():
            dgf_ref[...] = jnp.zeros_like(dgf_ref)
            loss_ref[...] = jnp.zeros_like(loss_ref)

        sbz = g_ref[:, z0:z1]
        swz = g_ref[:, z1:z2]
        s1 = jax.nn.sigmoid(g_ref[:, z2:z3])
        s2 = jax.nn.sigmoid(g_ref[:, z3:])
        sg_sb = jax.nn.sigmoid(sbz)
        sg_sw = jax.nn.sigmoid(swz)
        silu_sb = sbz * sg_sb
        silu_sw = swz * sg_sw
        osb = osb_ref[...]
        osw = osw_ref[...]
        usb = (osb * silu_sb).astype(BF16)
        usw = (osw * silu_sw).astype(BF16)
        y_sb = _dot(usb, wsb_ref[...])
        y_sw = _dot(usw, wsw_ref[...])
        mb = (s1 * y_sb + s2 * y_sw).astype(BF16)
        h1 = h_ref[...] + _dot(mb, wo_ref[...])
        rf = lax.rsqrt(jnp.mean(h1 * h1, axis=1, keepdims=True) + RMS_EPS)
        hhat = h1 * rf
        gf = gf_ref[...]
        row = i * R + lax.broadcasted_iota(jnp.int32, (R, 1), 0)
        err = jnp.where(row >= BLOCK, hhat * gf - t_ref[...], 0.0)
        lane0 = (lax.broadcasted_iota(jnp.int32, (8, BLOCK), 0) == 0) & (lax.broadcasted_iota(jnp.int32, (8, BLOCK), 1) == 0)
        loss_ref[...] += jnp.where(lane0, (0.5 / D) * jnp.sum(err * err), 0.0)
        dy = err * (1.0 / D)
        dgf_ref[...] += jnp.sum(dy * hhat, axis=0, keepdims=True)
        dhh = dy * gf
        dh1 = rf * (dhh - hhat * jnp.mean(dhh * hhat, axis=1, keepdims=True))
        dh1b = dh1.astype(BF16)
        dm = _dot_nt(dh1b, wo_ref[...])
        dysb = (dm * s1).astype(BF16)
        dysw = (dm * s2).astype(BF16)
        dusb = _dot_nt(dysb, wsb_ref[...])
        dusw = _dot_nt(dysw, wsw_ref[...])
        dosb_ref[...] = (dusb * silu_sb).astype(BF16)
        dosw_ref[...] = (dusw * silu_sw).astype(BF16)
        dg_ref[:, z0:z1] = (dusb * osb * (sg_sb * (1.0 + sbz * (1.0 - sg_sb)))).astype(BF16)
        dg_ref[:, z1:z2] = (dusw * osw * (sg_sw * (1.0 + swz * (1.0 - sg_sw)))).astype(BF16)
        dg_ref[:, z2:z3] = (dm * y_sb * (s1 * (1.0 - s1))).astype(BF16)
        dg_ref[:, z3:] = (dm * y_sw * (s2 * (1.0 - s2))).astype(BF16)
        dh1_ref[...] = dh1
        mb_ref[...] = mb
        usb_ref[...] = usb
        usw_ref[...] = usw
        dh1b_ref[...] = dh1b
        dysb_ref[...] = dysb
        dysw_ref[...] = dysw

    def rows(n):
        return pl.BlockSpec((R, n), lambda i: (i, 0))

    def whole(shape):
        return pl.BlockSpec(shape, lambda i: (0, 0))

    GW = gates.shape[1]
    return pl.pallas_call(
        body,
        name="tail_fwd_bwd",
        grid=(L // R,),
        in_specs=[rows(D), rows(D), rows(SB_WIDTH), rows(SWA_WIDTH), rows(GW),
                  whole(w_bsb.shape), whole(w_bswa.shape), whole(w_out.shape), whole((1, D))],
        out_specs=[rows(SB_WIDTH), rows(SWA_WIDTH), rows(GW), rows(D),
                   rows(D), rows(SB_WIDTH), rows(SWA_WIDTH), rows(D), rows(D), rows(D),
                   whole((1, D)), whole((8, BLOCK))],
        out_shape=[jax.ShapeDtypeStruct((L, SB_WIDTH), BF16), jax.ShapeDtypeStruct((L, SWA_WIDTH), BF16),
                   jax.ShapeDtypeStruct((L, GW), BF16), jax.ShapeDtypeStruct((L, D), F32),
                   jax.ShapeDtypeStruct((L, D), BF16), jax.ShapeDtypeStruct((L, SB_WIDTH), BF16),
                   jax.ShapeDtypeStruct((L, SWA_WIDTH), BF16), jax.ShapeDtypeStruct((L, D), BF16),
                   jax.ShapeDtypeStruct((L, D), BF16), jax.ShapeDtypeStruct((L, D), BF16),
                   jax.ShapeDtypeStruct((1, D), F32), jax.ShapeDtypeStruct((8, BLOCK), F32)],
        compiler_params=_params(("arbitrary",)),
    )(h0, tgt, o_sb, o_sw, gates, w_bsb, w_bswa, w_out, gain_f)


def local_step(x, tgt, meta, gain, w_in, w_bsb, w_bswa, w_out, sinks, gain_f):
    S, D = x.shape
    L = S + BLOCK
    h0 = jnp.concatenate([jnp.zeros((PAD, D), F32), meta, x], axis=0)
    tgt_p = jnp.concatenate([jnp.zeros((BLOCK, D), F32), tgt], axis=0)
    tt = _suffix_matrix()
    cos, sin = rope_tables(L)
    qkv, xn = in_proj(h0, gain, w_in[:, :SB_COLS], "in_proj_sb", BF16)
    proj_sw = rows_matmul(xn, w_in[:, SB_COLS:GATE_COL0], "in_proj_swa")
    gates = rows_matmul(xn, w_in[:, GATE_COL0:], "in_proj_gates")
    o_sb = sb_attention_fwd(qkv, tt)
    o_sw = swa_attention_fwd(proj_sw, cos, sin, sinks)
    (do_sb, do_sw, dgates, dh1, mb, usb, usw, dh1b, dysb, dysw, dgf, loss) = tail_fwd_bwd(
        h0, tgt_p, o_sb, o_sw, gates, w_bsb, w_bswa, w_out, gain_f)
    dq_sb, dk_sb, dv_sb = sb_attention_bwd(qkv, o_sb, do_sb, tt)
    dq_sw, dk_sw, dv_sw, dsinks = swa_attention_bwd(proj_sw, cos, sin, sinks, do_sw)
    dproj = jnp.concatenate([dq_sb, dk_sb, dv_sb, dq_sw, dk_sw.astype(BF16), dv_sw.astype(BF16), dgates], axis=1)
    dw_in = matmul_tn(xn, dproj, "dw_in")
    dw_out = matmul_tn(mb, dh1b, "dw_out")
    dw_bsb = matmul_tn(usb, dysb, "dw_bsb")
    dw_bswa = matmul_tn(usw, dysw, "dw_bswa")
    dh0, dgain = in_proj_bwd(dproj, w_in, h0, gain, dh1)
    return (loss[0, 0], dh0[BLOCK:], dh0[PAD:BLOCK], dgain, dw_in, dw_bsb, dw_bswa, dw_out,
            dsinks[:1, :SWA_Q_HEADS], dgf)


MESH_IDS = pl.DeviceIdType.MESH
ANY = pl.BlockSpec(memory_space=pl.ANY)


def _place():
    return lax.axis_index("x"), lax.axis_index("y"), lax.axis_index("c")


def _index(x, y, c):
    return 4 * x + 2 * y + c


def all_gather(block, name):
    def body(x_ref, out_ref, send_sems, recv_sems, local_sem):
        x, y, c = _place()
        me, sibling = (x, y, c), (x, y, 1 - c)
        chips = [(1 - x, y), (x, 1 - y), (1 - x, 1 - y)]

        def copy(k, blk, to, src=None):
            dst = out_ref.at[_index(*blk)]
            return pltpu.make_async_remote_copy(
                src_ref=dst if src is None else src, dst_ref=dst,
                send_sem=send_sems.at[k], recv_sem=recv_sems.at[k], device_id=to, device_id_type=MESH_IDS)

        mine = pltpu.make_async_copy(x_ref, out_ref.at[_index(*me)], local_sem)
        mine.start()
        first = [copy(0, me, sibling, src=x_ref)]
        first += [copy(1 + j, me, (*chip, c), src=x_ref) for j, chip in enumerate(chips)]
        for cp in first:
            cp.start()
        passed = [copy(4 + j, (*chip, c), sibling) for j, chip in enumerate(chips)]
        for j, chip in enumerate(chips):
            copy(1 + j, (*chip, c), me).wait_recv()
            passed[j].start()
        copy(0, sibling, me).wait_recv()
        for j, chip in enumerate(chips):
            copy(4 + j, (*chip, 1 - c), me).wait_recv()
        for cp in first + passed:
            cp.wait_send()
        mine.wait()

    return pl.pallas_call(
        body,
        name=name,
        out_shape=jax.ShapeDtypeStruct((N_DEV,) + block.shape, block.dtype),
        in_specs=[ANY],
        out_specs=ANY,
        scratch_shapes=[pltpu.SemaphoreType.DMA((7,)), pltpu.SemaphoreType.DMA((7,)), pltpu.SemaphoreType.DMA],
    )(block)


def exchange_partials(parts):
    def body(g_ref, out_ref, send_sems, recv_sems, local_sem):
        x, y, c = _place()
        me = _index(x, y, c)
        mine = pltpu.make_async_copy(g_ref.at[me], out_ref.at[me], local_sem)
        mine.start()
        copies = []
        for m in range(1, N_DEV):
            px = 1 - x if m & 4 else x
            py = 1 - y if m & 2 else y
            pc = 1 - c if m & 1 else c
            cp = pltpu.make_async_remote_copy(
                src_ref=g_ref.at[_index(px, py, pc)], dst_ref=out_ref.at[me],
                send_sem=send_sems.at[m - 1], recv_sem=recv_sems.at[m - 1],
                device_id=(px, py, pc), device_id_type=MESH_IDS)
            cp.start()
            copies.append(cp)
        for cp in copies:
            cp.wait()
        mine.wait()

    return pl.pallas_call(
        body,
        name="exchange_partials",
        out_shape=jax.ShapeDtypeStruct(parts.shape, parts.dtype),
        in_specs=[ANY],
        out_specs=ANY,
        scratch_shapes=[pltpu.SemaphoreType.DMA((7,)), pltpu.SemaphoreType.DMA((7,)), pltpu.SemaphoreType.DMA],
    )(parts)


def _adamw(w, g, m, v):
    m = ADAM_B1 * m + (1.0 - ADAM_B1) * g
    v = ADAM_B2 * v + (1.0 - ADAM_B2) * (g * g)
    m_hat = m / (1.0 - ADAM_B1 ** ADAM_STEP)
    v_hat = v / (1.0 - ADAM_B2 ** ADAM_STEP)
    delta = -ADAM_LR * (m_hat / (jnp.sqrt(v_hat) + ADAM_EPS) + ADAM_WD * w)
    return delta, m, v


def sum_and_adamw(parts, w, m, v, name):
    _, R, C = parts.shape
    tr = _pick(R, (528, 512, 256, 128, 24, 8))

    def body(p_ref, w_ref, m_ref, v_ref, g_ref, d_ref, nm_ref, nv_ref):
        g = p_ref[0].astype(F32)
        for s in range(1, N_DEV):
            g = g + p_ref[s].astype(F32)
        d, nm, nv = _adamw(w_ref[...], g, m_ref[...], v_ref[...])
        g_ref[...] = g
        d_ref[...] = d
        nm_ref[...] = nm
        nv_ref[...] = nv

    row = pl.BlockSpec((tr, C), lambda i: (i, 0))
    return pl.pallas_call(
        body,
        name=name,
        grid=(R // tr,),
        in_specs=[pl.BlockSpec((N_DEV, tr, C), lambda i: (0, i, 0)), row, row, row],
        out_specs=[row, row, row, row],
        out_shape=[jax.ShapeDtypeStruct((R, C), F32)] * 4,
        compiler_params=_params(("arbitrary",)),
    )(parts, w, m, v)


W_IN_SHARD = IN_COLS // N_DEV
ROWS_W_IN = D_MODEL * W_IN_SHARD // BLOCK
ROWS_W_BSB = SB_WIDTH
ROWS_W_ROWSHARD = D_MODEL
SMALL_ROWS = 24


def _pack_shards(w_in, w_bsb, w_bswa, w_out, meta):
    return jnp.concatenate([w_in.reshape(ROWS_W_IN, BLOCK), w_bsb.reshape(ROWS_W_BSB, BLOCK),
                            w_bswa.reshape(ROWS_W_ROWSHARD, BLOCK), w_out.reshape(ROWS_W_ROWSHARD, BLOCK),
                            meta.reshape(N_META, BLOCK)], axis=0)


def _unpack_shards(p):
    o = np.cumsum([0, ROWS_W_IN, ROWS_W_BSB, ROWS_W_ROWSHARD, ROWS_W_ROWSHARD, N_META])
    return (p[o[0]:o[1]].reshape(1, D_MODEL, W_IN_SHARD), p[o[1]:o[2]].reshape(1, SB_WIDTH, BLOCK),
            p[o[2]:o[3]].reshape(1, BLOCK, D_MODEL), p[o[3]:o[4]].reshape(1, BLOCK, D_MODEL),
            p[o[4]:o[5]].reshape(N_META, BLOCK))


def _pack_by_owner(dw_in, dw_bsb, dw_bswa, dw_out, dmeta):
    cols = lambda a: a.reshape(a.shape[0], N_DEV, -1).transpose(1, 0, 2)
    return jnp.concatenate([cols(dw_in).reshape(N_DEV, ROWS_W_IN, BLOCK), cols(dw_bsb),
                            dw_bswa.reshape(N_DEV, ROWS_W_ROWSHARD, BLOCK), dw_out.reshape(N_DEV, ROWS_W_ROWSHARD, BLOCK),
                            cols(dmeta)], axis=1)


def _pack_small(gain, gain_f, sinks, loss):
    z = jnp.zeros((SMALL_ROWS - 16, BLOCK), F32)
    z = z.at[0, :SWA_Q_HEADS].set(sinks.reshape(-1)).at[1, 0].set(loss)
    return jnp.concatenate([gain.reshape(8, BLOCK), gain_f.reshape(8, BLOCK), z], axis=0)


def _unpack_small(p):
    return p[0:8].reshape(1, D_MODEL), p[8:16].reshape(D_MODEL), p[16:17, :SWA_Q_HEADS], p[17, 0]


def kernel(x, meta_tokens, norm_gain, w_in, w_branch_sb, w_branch_swa, w_out, attn_sinks, final_norm_gain, loss_target, m_meta_tokens, m_norm_gain, m_w_in, m_w_branch_sb, m_w_branch_swa, m_w_out, m_attn_sinks, m_final_norm_gain, v_meta_tokens, v_norm_gain, v_w_in, v_w_branch_sb, v_w_branch_swa, v_w_out, v_attn_sinks, v_final_norm_gain):
    meta_bits = lax.bitcast_convert_type(meta_tokens, BF16).reshape(2 * N_META, BLOCK)
    mine = jnp.concatenate([_pack_shards(w_in, w_branch_sb, w_branch_swa, w_out, meta_tokens)[:-N_META].astype(BF16),
                            meta_bits], axis=0)
    full = all_gather(mine, "all_gather_weights")
    o = np.cumsum([0, ROWS_W_IN, ROWS_W_BSB, ROWS_W_ROWSHARD, ROWS_W_ROWSHARD, 2 * N_META])
    cols = lambda a: a.transpose(1, 0, 2).reshape(a.shape[1], -1)
    f_w_in = cols(full[:, o[0]:o[1]].reshape(N_DEV, D_MODEL, W_IN_SHARD))
    f_w_bsb = cols(full[:, o[1]:o[2]])
    f_w_bswa = full[:, o[2]:o[3]].reshape(D_MODEL, D_MODEL)
    f_w_out = full[:, o[3]:o[4]].reshape(D_MODEL, D_MODEL)
    f_meta = cols(lax.bitcast_convert_type(full[:, o[4]:o[5]].reshape(N_DEV, N_META, BLOCK, 2), F32))

    (loss, grad_x, dmeta, dgain, dw_in, dw_bsb, dw_bswa, dw_out, dsinks, dgf) = local_step(
        x[0], loss_target[0], f_meta, norm_gain, f_w_in, f_w_bsb, f_w_bswa, f_w_out, attn_sinks,
        final_norm_gain.reshape(1, D_MODEL))

    parts = exchange_partials(_pack_by_owner(*[g.astype(BF16) for g in (dw_in, dw_bsb, dw_bswa, dw_out, dmeta)]))
    packs = [_pack_shards(a[0], b[0], c[0], d[0], e) for a, b, c, d, e in (
        (w_in, w_branch_sb, w_branch_swa, w_out, meta_tokens),
        (m_w_in, m_w_branch_sb, m_w_branch_swa, m_w_out, m_meta_tokens),
        (v_w_in, v_w_branch_sb, v_w_branch_swa, v_w_out, v_meta_tokens))]
    big = [_unpack_shards(p) for p in sum_and_adamw(parts, *packs, "sum_adamw_sharded")]

    small = all_gather(_pack_small(dgain, dgf, dsinks, loss), "all_gather_small")
    zero = jnp.zeros((), F32)
    spacks = [_pack_small(a, b, c, zero) for a, b, c in (
        (norm_gain, final_norm_gain, attn_sinks), (m_norm_gain, m_final_norm_gain, m_attn_sinks),
        (v_norm_gain, v_final_norm_gain, v_attn_sinks))]
    sm = [_unpack_small(p) for p in sum_and_adamw(small, *spacks, "sum_adamw_replicated")]

    def leaves(k):
        b, s = big[k], sm[k]
        return (b[4], s[0], b[0], b[1], b[2], b[3], s[2], s[1])

    return (sm[0][3], grad_x[None], *leaves(0), *leaves(1), *leaves(2), *leaves(3))
```

```python
import numpy as np
import jax
import jax.numpy as jnp
from jax import lax
from jax.experimental import pallas as pl
from jax.experimental.pallas import tpu as pltpu

F32 = jnp.float32
BF16 = jnp.bfloat16

D_MODEL = 1024
N_META = 16
BLOCK = 128
PAD = BLOCK - N_META
HEAD_DIM = 64
SB_HEADS = 8
SB_WIDTH = SB_HEADS * HEAD_DIM
SWA_Q_HEADS = 16
SWA_KV_HEADS = 2
SWA_WIDTH = SWA_Q_HEADS * HEAD_DIM
SWA_KV_WIDTH = SWA_KV_HEADS * HEAD_DIM
ROPE_THETA = 10000.0
RMS_EPS = 1e-6
SCALE = HEAD_DIM ** -0.5
SPLITS = (SB_WIDTH, SB_WIDTH, SB_WIDTH, SWA_WIDTH, SWA_KV_WIDTH, SWA_KV_WIDTH,
          SB_WIDTH, SWA_WIDTH, D_MODEL, D_MODEL)
IN_COLS = sum(SPLITS)
SB_COLS = 3 * SB_WIDTH
SWA_COLS = SWA_WIDTH + 2 * SWA_KV_WIDTH
GATE_COL0 = SB_COLS + SWA_COLS

N_DEV = 8
ADAM_LR = 0.001
ADAM_B1 = 0.9
ADAM_B2 = 0.999
ADAM_EPS = 1e-08
ADAM_WD = 0.01
ADAM_STEP = 10

VMEM_LIMIT = 56 * 1024 * 1024


def _params(sem, **kw):
    return pltpu.CompilerParams(dimension_semantics=sem, vmem_limit_bytes=VMEM_LIMIT, **kw)


def _dot(a, b):
    return jnp.dot(a, b, preferred_element_type=F32)


def _dot_nt(a, b):
    return lax.dot_general(a, b, (((1,), (1,)), ((), ())), preferred_element_type=F32)


def _dot_tn(a, b):
    return lax.dot_general(a, b, (((0,), (0,)), ((), ())), preferred_element_type=F32)


def _cat(xs, axis):
    return xs[0] if len(xs) == 1 else jnp.concatenate(xs, axis=axis)


def _split_bf16(x):
    hi = x.astype(BF16)
    lo = (x - hi.astype(F32)).astype(BF16)
    return jnp.concatenate([hi, lo], axis=1)


def _suffix_matrix():
    j = np.arange(BLOCK)[:, None]
    s = np.arange(BLOCK)[None, :]
    t = np.concatenate([(j >= s).astype(np.float32), np.ones((BLOCK, BLOCK), np.float32)], axis=1)
    return jnp.asarray(np.concatenate([t, t], axis=0), dtype=BF16)


def _softplus(z):
    return jnp.maximum(z, 0.0) + jnp.log(1.0 + jnp.exp(-jnp.abs(z)))


SB_SMALL = 4
SB_PAIRS_FWD = 4
SB_PAIRS_BWD = 2


def _sb_masks(i, nblk):
    if nblk == 0:
        r = lax.broadcasted_iota(jnp.int32, (BLOCK, BLOCK), 0)
        c = lax.broadcasted_iota(jnp.int32, (BLOCK, BLOCK), 1)
        return (c < r) & (i * BLOCK + c >= PAD)
    return lax.broadcasted_iota(jnp.int32, (BLOCK, nblk * BLOCK), 1) >= PAD


def _sb_heads(npairs):
    return [(pr, a) for pr in range(npairs) for a in range(2)]


def _lanes(pr):
    return slice(pr * BLOCK, (pr + 1) * BLOCK)


def _masked_heads(x, half0, npairs):
    out = []
    for pr, a in _sb_heads(npairs):
        xp = x[:, _lanes(pr)]
        out.append((jnp.where(half0, xp, 0.0) if a == 0 else jnp.where(half0, 0.0, xp)).astype(BF16))
    return out


def _sb_sweep(i, tile):
    same = lambda alive: alive
    small = SB_SMALL
    live_tile = tile
    tile = lambda j0, nblk, mask, alive: lax.cond(alive, lambda _: live_tile(j0, nblk, mask), same, alive)
    alive = live_tile(i, 1, _sb_masks(i, 0))
    alive = lax.switch(jnp.minimum(i, 3), [
        same,
        lambda al: tile(0, 1, _sb_masks(i, 1), al),
        lambda al: tile(0, 2, _sb_masks(i, 2), al),
        lambda al: tile(i - 2, 2, None, al)], alive)

    def below(alive):
        rest = i - 2
        n_grp = rest // small
        n_one = rest - n_grp * small
        low_is_one = (n_grp == 0) & (n_one > 0)
        n_plain = n_one - jnp.where(low_is_one, 1, 0)
        alive = lax.fori_loop(0, n_plain, lambda t, al: tile(rest - 1 - t, 1, None, al), alive)
        alive = lax.cond(low_is_one, lambda al: tile(0, 1, _sb_masks(i, 1), al), same, alive)
        alive = lax.fori_loop(0, jnp.maximum(n_grp - 1, 0),
                              lambda t, al: tile((n_grp - 1 - t) * small, small, None, al), alive)
        return lax.cond(n_grp > 0, lambda al: tile(0, small, _sb_masks(i, small), al), same, alive)

    lax.cond(alive & (i > 2), below, same, alive)


def _sb_weights(zs, ss, cs, tt, nblk, mask):
    suf = _dot(_cat([_split_bf16(s[:, b * BLOCK:(b + 1) * BLOCK]) for s in ss for b in range(nblk)], 0), tt)
    ws, out_cs = [], []
    for h in range(len(zs)):
        c = cs[h]
        wb = [None] * nblk
        for b in reversed(range(nblk)):
            sab = suf[(h * nblk + b) * BLOCK:(h * nblk + b + 1) * BLOCK]
            wb[b] = jnp.exp(zs[h][:, b * BLOCK:(b + 1) * BLOCK] + c - sab[:, :BLOCK])
            c = c - sab[:, BLOCK:]
        w = _cat(wb, 1)
        if mask is not None:
            w = jnp.where(mask, w, 0.0)
        ws.append(w)
        out_cs.append(c)
    return ws, out_cs


SB_DEAD = -110.0


def _row_norm_max(x, half0, a):
    xf = x.astype(F32)
    sq = jnp.where(half0, xf * xf, 0.0) if a == 0 else jnp.where(half0, 0.0, xf * xf)
    return jnp.sqrt(jnp.max(jnp.sum(sq, axis=1, keepdims=True)))


def _sb_logit_bounds(i, q_heads, k_ref, kmax_ref, half0, heads):
    @pl.when(i == 0)
    def _():
        for h, (pr, a) in enumerate(heads):
            kmax_ref[h] = _row_norm_max(k_ref[:, _lanes(pr)], half0, a)
    return [_row_norm_max(q_heads[h], half0, a) * kmax_ref[h] * 1.001 + 0.01 for h, (pr, a) in enumerate(heads)]


def _sb_alive(cs, zmax):
    worst = cs[0] + zmax[0]
    for c, zm in zip(cs[1:], zmax[1:]):
        worst = jnp.maximum(worst, c + zm)
    return jnp.max(worst) > SB_DEAD


def sb_attention_fwd(qkv, tt):
    L = qkv.shape[0]
    nb = L // BLOCK
    NP = SB_PAIRS_FWD
    SB_W = NP * BLOCK
    nq = SB_WIDTH // SB_W

    def body(q_ref, k_ref, v_ref, tt_ref, o_ref, kmax_ref, acc_ref, c_ref):
        i = pl.program_id(1)
        half0 = lax.broadcasted_iota(jnp.int32, (1, BLOCK), 1) < HEAD_DIM
        heads = _sb_heads(NP)
        qh = _masked_heads(q_ref[...].astype(F32) * SCALE, half0, NP)
        zmax = _sb_logit_bounds(i, qh, k_ref, kmax_ref, half0, heads)
        acc_ref[...] = jnp.zeros_like(acc_ref)
        c_ref[...] = jnp.zeros_like(c_ref)

        def rows_of(j0, nblk):
            return pl.ds(pl.multiple_of(j0 * BLOCK, BLOCK), nblk * BLOCK)

        def tile(j0, nblk, mask):
            rows = rows_of(j0, nblk)
            zs = [_dot_nt(qh[h], k_ref[rows, _lanes(pr)]) for h, (pr, a) in enumerate(heads)]
            ss = [_softplus(z) for z in zs]
            if mask is not None:
                ss = [jnp.where(mask, s, 0.0) for s in ss]
            ws, cs = _sb_weights(zs, ss, [c_ref[h] for h in range(len(heads))], tt_ref[...], nblk, mask)
            for h, (pr, a) in enumerate(heads):
                acc_ref[h] += _dot(ws[h].astype(BF16), v_ref[rows, _lanes(pr)])
                c_ref[h] = cs[h]
            return _sb_alive(cs, zmax)

        _sb_sweep(i, tile)
        o_ref[...] = _cat([jnp.where(half0, acc_ref[2 * pr], acc_ref[2 * pr + 1]) for pr in range(NP)], 1)

    panel = lambda c: pl.BlockSpec((L, SB_W), lambda p, i: (0, c * nq + p), pipeline_mode=pl.Buffered(1))
    return pl.pallas_call(
        body,
        name="sb_attention_fwd",
        grid=(nq, nb),
        in_specs=[
            pl.BlockSpec((BLOCK, SB_W), lambda p, i: (i, p)),
            panel(1), panel(2),
            pl.BlockSpec((2 * BLOCK, 2 * BLOCK), lambda p, i: (0, 0)),
        ],
        out_specs=pl.BlockSpec((BLOCK, SB_W), lambda p, i: (i, p)),
        out_shape=jax.ShapeDtypeStruct((L, SB_WIDTH), F32),
        scratch_shapes=[pltpu.SMEM((2 * NP,), F32), pltpu.VMEM((2 * NP, BLOCK, BLOCK), F32),
                        pltpu.VMEM((2 * NP, BLOCK, BLOCK), F32)],
        compiler_params=_params(("arbitrary", "arbitrary")),
    )(qkv, qkv, qkv, tt)


def sb_attention_bwd(qkv, o_sb, do_sb, tt):
    L = qkv.shape[0]
    nb = L // BLOCK
    NP = SB_PAIRS_BWD
    SB_W = NP * BLOCK
    nq = SB_WIDTH // SB_W

    def body(q_ref, k_ref, v_ref, o_ref, do_ref, tt_ref, dq_ref, dk_ref, dv_ref, dk_acc, dv_acc, kmax_ref,
             acc_ref, c_ref, ce_ref):
        i = pl.program_id(1)
        half0 = lax.broadcasted_iota(jnp.int32, (1, BLOCK), 1) < HEAD_DIM
        heads = _sb_heads(NP)

        @pl.when(i == 0)
        def _():
            dk_acc[...] = jnp.zeros_like(dk_acc)
            dv_acc[...] = jnp.zeros_like(dv_acc)

        qh = _masked_heads(q_ref[...].astype(F32) * SCALE, half0, NP)
        zmax = _sb_logit_bounds(i, qh, k_ref, kmax_ref, half0, heads)
        do = do_ref[...]
        doh = _masked_heads(do, half0, NP)
        od = o_ref[...] * do.astype(F32)
        acc_ref[...] = jnp.zeros_like(acc_ref)
        c_ref[...] = jnp.zeros_like(c_ref)
        for h, (pr, a) in enumerate(heads):
            x = od[:, _lanes(pr)]
            x = jnp.where(half0, x, 0.0) if a == 0 else jnp.where(half0, 0.0, x)
            ce_ref[h] = jnp.broadcast_to(jnp.sum(x, axis=1, keepdims=True), (BLOCK, BLOCK))

        def rows_of(j0, nblk):
            return pl.ds(pl.multiple_of(j0 * BLOCK, BLOCK), nblk * BLOCK)

        def tile(j0, nblk, mask):
            cs = [c_ref[h] for h in range(len(heads))]
            ces = [ce_ref[h] for h in range(len(heads))]
            rows = rows_of(j0, nblk)
            zs = [_dot_nt(qh[h], k_ref[rows, _lanes(pr)]) for h, (pr, a) in enumerate(heads)]
            dws = [_dot_nt(doh[h], v_ref[rows, _lanes(pr)]) for h, (pr, a) in enumerate(heads)]
            ss = [_softplus(z) for z in zs]
            sigs = [jnp.exp(z - s) for z, s in zip(zs, ss)]
            if mask is not None:
                ss = [jnp.where(mask, s, 0.0) for s in ss]
            ws, cs = _sb_weights(zs, ss, cs, tt_ref[...], nblk, mask)
            wbs = [w.astype(BF16) for w in ws]
            es = [wb.astype(F32) * dw for wb, dw in zip(wbs, dws)]
            esuf = _dot(_cat([_split_bf16(e[:, b * BLOCK:(b + 1) * BLOCK]) for e in es for b in range(nblk)], 0),
                        tt_ref[...])
            out_ces, dzbs = [], []
            for h, (pr, a) in enumerate(heads):
                ce = ces[h]
                dzs = [None] * nblk
                for b in reversed(range(nblk)):
                    eab = esuf[(h * nblk + b) * BLOCK:(h * nblk + b + 1) * BLOCK]
                    sl = slice(b * BLOCK, (b + 1) * BLOCK)
                    e = es[h][:, sl]
                    dzs[b] = e - sigs[h][:, sl] * (e + (ce - eab[:, :BLOCK]))
                    ce = ce - eab[:, BLOCK:]
                dz = _cat(dzs, 1)
                if mask is not None:
                    dz = jnp.where(mask, dz, 0.0)
                dzbs.append(dz.astype(BF16))
                out_ces.append(ce)
            for h, (pr, a) in enumerate(heads):
                acc_ref[h] += _dot(dzbs[h], k_ref[rows, _lanes(pr)])
                c_ref[h] = cs[h]
                ce_ref[h] = out_ces[h]
            dkv = [_dot_tn(_cat([dzbs[2 * pr], wbs[2 * pr], dzbs[2 * pr + 1], wbs[2 * pr + 1]], 0), qdo[pr])
                   for pr in range(NP)]
            dk_acc[rows, :] += _cat([x[:, :BLOCK] for x in dkv], 1)
            dv_acc[rows, :] += _cat([x[:, BLOCK:] for x in dkv], 1)
            return _sb_alive(cs, zmax)

        zb = jnp.zeros((BLOCK, BLOCK), BF16)
        qdo = [_cat([_cat([qh[h], zb], 1) if kind == 0 else _cat([zb, doh[h]], 1)
                     for h in (2 * pr, 2 * pr + 1) for kind in (0, 1)], 0) for pr in range(NP)]
        _sb_sweep(i, tile)
        dq_ref[...] = (_cat([jnp.where(half0, acc_ref[2 * pr], acc_ref[2 * pr + 1]) for pr in range(NP)], 1)
                       * SCALE).astype(BF16)

        @pl.when(i == nb - 1)
        def _():
            dk_ref[...] = dk_acc[...].astype(BF16)
            dv_ref[...] = dv_acc[...].astype(BF16)

    blk = pl.BlockSpec((BLOCK, SB_W), lambda p, i: (i, p))
    panel = pl.BlockSpec((L, SB_W), lambda p, i: (0, p))
    return pl.pallas_call(
        body,
        name="sb_attention_bwd",
        grid=(nq, nb),
        in_specs=[
            blk,
            pl.BlockSpec((L, SB_W), lambda p, i: (0, nq + p), pipeline_mode=pl.Buffered(1)),
            pl.BlockSpec((L, SB_W), lambda p, i: (0, 2 * nq + p), pipeline_mode=pl.Buffered(1)),
            blk, blk,
            pl.BlockSpec((2 * BLOCK, 2 * BLOCK), lambda p, i: (0, 0)),
        ],
        out_specs=[blk, panel, panel],
        out_shape=[jax.ShapeDtypeStruct((L, SB_WIDTH), BF16)] * 3,
        scratch_shapes=[pltpu.VMEM((L, SB_W), F32), pltpu.VMEM((L, SB_W), F32), pltpu.SMEM((2 * NP,), F32)]
        + [pltpu.VMEM((2 * NP, BLOCK, BLOCK), F32)] * 3,
        compiler_params=_params(("arbitrary", "arbitrary")),
    )(qkv, qkv, qkv, o_sb, do_sb, tt)


SWA_PAIRS = SWA_WIDTH // BLOCK
PAIRS_PER_KV = SWA_PAIRS // SWA_KV_HEADS
CB_SWK = SWA_PAIRS
CB_SWV = SWA_PAIRS + 1


def rope_tables(L):
    half = HEAD_DIM // 2
    inv = ROPE_THETA ** (-jnp.arange(half, dtype=F32) / half)
    pos = (jnp.arange(L) - PAD).astype(F32)
    ang = pos[:, None] * inv[None, :]
    reps = BLOCK // half
    return jnp.tile(jnp.cos(ang), (1, reps)), jnp.tile(jnp.sin(ang), (1, reps))


def _rot_half(x):
    lane = lax.broadcasted_iota(jnp.int32, (1, BLOCK), 1)
    first = (lane % HEAD_DIM) < (HEAD_DIM // 2)
    return jnp.where(first, -pltpu.roll(x, BLOCK - HEAD_DIM // 2, axis=1), pltpu.roll(x, HEAD_DIM // 2, axis=1))


def _rope(x, cos, sin):
    return x * cos + _rot_half(x) * sin


def _unrope(x, cos, sin):
    return x * cos - _rot_half(x) * sin


def _swa_specs():
    prev = lambda n: jnp.maximum(n - 1, 0)
    cur = lambda n: n
    blk = lambda f, c: pl.BlockSpec((BLOCK, BLOCK), lambda n: (f(n), c))
    return [
        pl.BlockSpec((BLOCK, SWA_WIDTH), lambda n: (n, 0)),
        blk(prev, CB_SWK), blk(cur, CB_SWK), blk(prev, CB_SWV), blk(cur, CB_SWV),
        blk(prev, 0), blk(cur, 0), blk(prev, 0), blk(cur, 0),
        pl.BlockSpec(memory_space=pltpu.SMEM),
    ]


def _swa_probs(n, q_ref, kp_ref, kc_ref, vp_ref, vc_ref, cp_ref, cc_ref, sp_ref, sc_ref, sink_ref):
    lane = lax.broadcasted_iota(jnp.int32, (1, BLOCK), 1)
    halves = (lane < HEAD_DIM, lane >= HEAD_DIM)
    cosc, sinc = cc_ref[...], sc_ref[...]
    qs = [_rope(q_ref[:, p * BLOCK:(p + 1) * BLOCK], cosc, sinc) * SCALE for p in range(SWA_PAIRS)]
    kb = jnp.concatenate([_rope(kp_ref[...], cp_ref[...], sp_ref[...]), _rope(kc_ref[...], cosc, sinc)], axis=0)
    vb = jnp.concatenate([vp_ref[...], vc_ref[...]], axis=0)
    kv = {True: (kb, vb), False: (pltpu.roll(kb, HEAD_DIM, axis=1), pltpu.roll(vb, HEAD_DIM, axis=1))}
    rows = PAIRS_PER_KV * BLOCK
    r = lax.broadcasted_iota(jnp.int32, (rows, 2 * BLOCK), 0) % BLOCK
    c = lax.broadcasted_iota(jnp.int32, (rows, 2 * BLOCK), 1)
    valid = (c > r) & (c <= r + BLOCK) & ((n - 1) * BLOCK + c >= PAD)
    combos = [(g, a) for g in range(SWA_KV_HEADS) for a in range(2)]
    qst, ksel, vsel, scores = {}, {}, {}, {}
    for g, a in combos:
        qst[g, a] = jnp.concatenate(
            [jnp.where(halves[a], qs[g * PAIRS_PER_KV + j], 0.0) for j in range(PAIRS_PER_KV)], axis=0).astype(BF16)
        ksel[g, a], vsel[g, a] = kv[g == a]
    for g, a in combos:
        scores[g, a] = _dot_nt(qst[g, a], ksel[g, a].astype(BF16))
    out = {}
    for g, a in combos:
        s = jnp.where(valid, scores[g, a], -1e30)
        sink = jnp.concatenate([jnp.full((BLOCK, BLOCK), sink_ref[0, 2 * (g * PAIRS_PER_KV + j) + a], F32)
                                for j in range(PAIRS_PER_KV)], axis=0)
        mx = jnp.maximum(jnp.max(s, axis=1, keepdims=True), sink)
        pe = jnp.exp(s - _twice(mx))
        es = jnp.exp(sink - mx)
        inv = 1.0 / (_row_sums(pe) + es)
        out[g, a] = (qst[g, a], ksel[g, a], vsel[g, a], pe * _twice(inv), es * inv, halves[a])
    return combos, out


def _twice(x):
    return jnp.concatenate([x, x], axis=1)


def _row_sums(x):
    return _dot(_split_bf16(x), jnp.ones((4 * BLOCK, BLOCK), BF16))


def swa_attention_fwd(proj, cos, sin, sinks):
    L = proj.shape[0]

    def body(q_ref, kp_ref, kc_ref, vp_ref, vc_ref, cp_ref, cc_ref, sp_ref, sc_ref, sink_ref, o_ref):
        n = pl.program_id(0)
        combos, parts = _swa_probs(n, q_ref, kp_ref, kc_ref, vp_ref, vc_ref, cp_ref, cc_ref, sp_ref, sc_ref, sink_ref)
        outs = {}
        for g, a in combos:
            qst, ksel, vsel, probs, psink, half = parts[g, a]
            outs[g, a] = _dot(probs.astype(BF16), jnp.where(half, vsel, 0.0).astype(BF16))
        for g in range(SWA_KV_HEADS):
            both = outs[g, 0] + outs[g, 1]
            for j in range(PAIRS_PER_KV):
                p = g * PAIRS_PER_KV + j
                o_ref[:, p * BLOCK:(p + 1) * BLOCK] = both[j * BLOCK:(j + 1) * BLOCK]

    return pl.pallas_call(
        body,
        name="swa_attention_fwd",
        grid=(L // BLOCK,),
        in_specs=_swa_specs(),
        out_specs=pl.BlockSpec((BLOCK, SWA_WIDTH), lambda n: (n, 0)),
        out_shape=jax.ShapeDtypeStruct((L, SWA_WIDTH), F32),
        compiler_params=_params(("arbitrary",)),
    )(proj, proj, proj, proj, proj, cos, cos, sin, sin, sinks)


def swa_attention_bwd(proj, cos, sin, sinks, do_sw):
    L = proj.shape[0]

    def body(q_ref, kp_ref, kc_ref, vp_ref, vc_ref, cp_ref, cc_ref, sp_ref, sc_ref, sink_ref, do_ref,
             dq_ref, dk_ref, dv_ref, ds_ref):
        n = pl.program_id(0)

        @pl.when(n == 0)
        def _():
            dk_ref[...] = jnp.zeros_like(dk_ref)
            dv_ref[...] = jnp.zeros_like(dv_ref)
            ds_ref[...] = jnp.zeros_like(ds_ref)

        combos, parts = _swa_probs(n, q_ref, kp_ref, kc_ref, vp_ref, vc_ref, cp_ref, cc_ref, sp_ref, sc_ref, sink_ref)
        lane8 = lax.broadcasted_iota(jnp.int32, (8, BLOCK), 1)
        dos, dps = {}, {}
        for g, a in combos:
            half = parts[g, a][5]
            dos[g, a] = jnp.concatenate(
                [jnp.where(half, do_ref[:, (g * PAIRS_PER_KV + j) * BLOCK:(g * PAIRS_PER_KV + j + 1) * BLOCK], 0.0)
                 for j in range(PAIRS_PER_KV)], axis=0).astype(BF16)
        for g, a in combos:
            dps[g, a] = _dot_nt(dos[g, a], parts[g, a][2].astype(BF16))
        dqs = {}
        dkb = jnp.zeros((2 * BLOCK, BLOCK), F32)
        dvb = jnp.zeros((2 * BLOCK, BLOCK), F32)
        dsk = jnp.zeros((8, BLOCK), F32)
        for g, a in combos:
            qst, ksel, vsel, probs, psink, half = parts[g, a]
            dp = dps[g, a]
            delta = _row_sums(probs * dp)
            ds = (probs * (dp - _twice(delta))).astype(BF16)
            pd = psink * delta
            for j in range(PAIRS_PER_KV):
                head = 2 * (g * PAIRS_PER_KV + j) + a
                dsk = dsk + jnp.where(lane8 == head, -jnp.sum(pd[j * BLOCK:(j + 1) * BLOCK, :1]), 0.0)
            dqs[g, a] = _dot(ds, jnp.where(half, ksel, 0.0).astype(BF16))
            dk_a = _dot_tn(ds, qst)
            dv_a = _dot_tn(probs.astype(BF16), dos[g, a])
            if g != a:
                dk_a = pltpu.roll(dk_a, HEAD_DIM, axis=1)
                dv_a = pltpu.roll(dv_a, HEAD_DIM, axis=1)
            dkb = dkb + dk_a
            dvb = dvb + dv_a
        cosc, sinc = cc_ref[...], sc_ref[...]
        for g in range(SWA_KV_HEADS):
            both = (dqs[g, 0] + dqs[g, 1]) * SCALE
            for j in range(PAIRS_PER_KV):
                p = g * PAIRS_PER_KV + j
                dq_ref[:, p * BLOCK:(p + 1) * BLOCK] = _unrope(both[j * BLOCK:(j + 1) * BLOCK], cosc, sinc).astype(BF16)
        ds_ref[...] += dsk
        cur = pl.ds(pl.multiple_of(n * BLOCK, BLOCK), BLOCK)
        dk_ref[cur, :] += _unrope(dkb[BLOCK:], cosc, sinc)
        dv_ref[cur, :] += dvb[BLOCK:]

        @pl.when(n > 0)
        def _():
            prv = pl.ds(pl.multiple_of((n - 1) * BLOCK, BLOCK), BLOCK)
            dk_ref[prv, :] += _unrope(dkb[:BLOCK], cp_ref[...], sp_ref[...])
            dv_ref[prv, :] += dvb[:BLOCK]

    whole = lambda n: (0, 0)
    row = pl.BlockSpec((BLOCK, SWA_WIDTH), lambda n: (n, 0))
    return pl.pallas_call(
        body,
        name="swa_attention_bwd",
        grid=(L // BLOCK,),
        in_specs=_swa_specs() + [row],
        out_specs=[row, pl.BlockSpec((L, BLOCK), whole), pl.BlockSpec((L, BLOCK), whole),
                   pl.BlockSpec((8, BLOCK), whole)],
        out_shape=[jax.ShapeDtypeStruct((L, SWA_WIDTH), BF16), jax.ShapeDtypeStruct((L, BLOCK), F32),
                   jax.ShapeDtypeStruct((L, BLOCK), F32), jax.ShapeDtypeStruct((8, BLOCK), F32)],
        compiler_params=_params(("arbitrary",)),
    )(proj, proj, proj, proj, proj, cos, cos, sin, sin, sinks, do_sw)


ROW_TILE = 640
TAIL_ROWS = 320


def _pick(n, cands):
    for c in cands:
        if n % c == 0:
            return c
    raise ValueError(f"no tile for {n}")


def in_proj(h0, gain, w, name, out_dtype):
    L, D = h0.shape
    N = w.shape[1]
    tm = _pick(L, (ROW_TILE, BLOCK))
    tn = _pick(N, (1792, 1536, 1280, 896, 640, 512, 384, 256, 128))

    def body(h_ref, g_ref, w_ref, o_ref, xn_ref):
        @pl.when(pl.program_id(1) == 0)
        def _():
            x = h_ref[...]
            r = lax.rsqrt(jnp.mean(x * x, axis=1, keepdims=True) + RMS_EPS)
            xn_ref[...] = ((x * r) * g_ref[...]).astype(BF16)
        o_ref[...] = _dot(xn_ref[...], w_ref[...]).astype(out_dtype)

    return pl.pallas_call(
        body,
        name=name,
        grid=(L // tm, N // tn),
        in_specs=[pl.BlockSpec((tm, D), lambda i, j: (i, 0)),
                  pl.BlockSpec((1, D), lambda i, j: (0, 0)),
                  pl.BlockSpec((D, tn), lambda i, j: (0, j))],
        out_specs=[pl.BlockSpec((tm, tn), lambda i, j: (i, j)),
                   pl.BlockSpec((tm, D), lambda i, j: (i, 0))],
        out_shape=[jax.ShapeDtypeStruct((L, N), out_dtype), jax.ShapeDtypeStruct((L, D), BF16)],
        compiler_params=_params(("arbitrary", "arbitrary")),
    )(h0, gain, w)


def rows_matmul(a, w, name):
    L, D = a.shape
    N = w.shape[1]
    tm = _pick(L, (ROW_TILE, BLOCK))
    tn = _pick(N, (1792, 1536, 1280, 896, 640, 512, 384, 256, 128))

    def body(a_ref, w_ref, o_ref):
        o_ref[...] = _dot(a_ref[...], w_ref[...])

    return pl.pallas_call(
        body,
        name=name,
        grid=(L // tm, N // tn),
        in_specs=[pl.BlockSpec((tm, D), lambda i, j: (i, 0)), pl.BlockSpec((D, tn), lambda i, j: (0, j))],
        out_specs=pl.BlockSpec((tm, tn), lambda i, j: (i, j)),
        out_shape=jax.ShapeDtypeStruct((L, N), F32),
        compiler_params=_params(("arbitrary", "arbitrary")),
    )(a, w)


def matmul_tn(a, b, name):
    Kd, M = a.shape
    N = b.shape[1]
    tk = _pick(Kd, (ROW_TILE, BLOCK))
    tn = _pick(N, (1280, 1024, 896, 640, 512, 256, 128))
    nk = Kd // tk

    def body(a_ref, b_ref, o_ref):
        k = pl.program_id(1)

        @pl.when(k == 0)
        def _():
            o_ref[...] = jnp.zeros_like(o_ref)
        o_ref[...] += _dot_tn(a_ref[...], b_ref[...])

    return pl.pallas_call(
        body,
        name=name,
        grid=(N // tn, nk),
        in_specs=[pl.BlockSpec((tk, M), lambda j, k: (k, 0)),
                  pl.BlockSpec((tk, tn), lambda j, k: (k, j))],
        out_specs=pl.BlockSpec((M, tn), lambda j, k: (0, j)),
        out_shape=jax.ShapeDtypeStruct((M, N), F32),
        compiler_params=_params(("arbitrary", "arbitrary")),
    )(a, b)


def in_proj_bwd(dproj, w, h0, gain, dh1):
    L, N = dproj.shape
    D = w.shape[0]
    tm = _pick(L, (ROW_TILE, BLOCK))
    tk = _pick(N, (3200, 1280, 640, 512, 256, 128))
    nk = N // tk

    def body(dp_ref, w_ref, h_ref, g_ref, dh1_ref, dh0_ref, dg_ref, acc_ref):
        i, k = pl.program_id(0), pl.program_id(1)

        @pl.when(k == 0)
        def _():
            acc_ref[...] = jnp.zeros_like(acc_ref)

        @pl.when((i == 0) & (k == 0))
        def _():
            dg_ref[...] = jnp.zeros_like(dg_ref)

        acc_ref[...] += _dot_nt(dp_ref[...], w_ref[...])

        @pl.when(k == nk - 1)
        def _():
            x = h_ref[...]
            r = lax.rsqrt(jnp.mean(x * x, axis=1, keepdims=True) + RMS_EPS)
            xhat = x * r
            dxn = acc_ref[...]
            dg_ref[...] += jnp.sum(dxn * xhat, axis=0, keepdims=True)
            dxh = dxn * g_ref[...]
            dh0_ref[...] = r * (dxh - xhat * jnp.mean(dxh * xhat, axis=1, keepdims=True)) + dh1_ref[...]

    row = pl.BlockSpec((tm, D), lambda i, k: (i, 0))
    vec = pl.BlockSpec((1, D), lambda i, k: (0, 0))
    return pl.pallas_call(
        body,
        name="in_proj_bwd",
        grid=(L // tm, nk),
        in_specs=[pl.BlockSpec((tm, tk), lambda i, k: (i, k)),
                  pl.BlockSpec((D, tk), lambda i, k: (0, k)),
                  row, vec, row],
        out_specs=[row, vec],
        out_shape=[jax.ShapeDtypeStruct((L, D), F32), jax.ShapeDtypeStruct((1, D), F32)],
        scratch_shapes=[pltpu.VMEM((tm, D), F32)],
        compiler_params=_params(("arbitrary", "arbitrary")),
    )(dproj, w, h0, gain, dh1)


def tail_fwd_bwd(h0, tgt, o_sb, o_sw, gates, w_bsb, w_bswa, w_out, gain_f):
    L, D = h0.shape
    R = _pick(L, (TAIL_ROWS, BLOCK))
    z0, z1, z2, z3 = 0, SB_WIDTH, SB_WIDTH + SWA_WIDTH, SB_WIDTH + SWA_WIDTH + D_MODEL

    def body(h_ref, t_ref, osb_ref, osw_ref, g_ref, wsb_ref, wsw_ref, wo_ref, gf_ref,
             dosb_ref, dosw_ref, dg_ref, dh1_ref, mb_ref, usb_ref, usw_ref, dh1b_ref, dysb_ref, dysw_ref,
             dgf_ref, loss_ref):
        i = pl.program_id(0)

        @pl.when(i == 0)
        def _():
            dgf_ref[...] = jnp.zeros_like(dgf_ref)
            loss_ref[...] = jnp.zeros_like(loss_ref)

        sbz = g_ref[:, z0:z1]
        swz = g_ref[:, z1:z2]
        s1 = jax.nn.sigmoid(g_ref[:, z2:z3])
        s2 = jax.nn.sigmoid(g_ref[:, z3:])
        sg_sb = jax.nn.sigmoid(sbz)
        sg_sw = jax.nn.sigmoid(swz)
        silu_sb = sbz * sg_sb
        silu_sw = swz * sg_sw
        osb = osb_ref[...]
        osw = osw_ref[...]
        usb = (osb * silu_sb).astype(BF16)
        usw = (osw * silu_sw).astype(BF16)
        y_sb = _dot(usb, wsb_ref[...])
        y_sw = _dot(usw, wsw_ref[...])
        mb = (s1 * y_sb + s2 * y_sw).astype(BF16)
        h1 = h_ref[...] + _dot(mb, wo_ref[...])
        rf = lax.rsqrt(jnp.mean(h1 * h1, axis=1, keepdims=True) + RMS_EPS)
        hhat = h1 * rf
        gf = gf_ref[...]
        row = i * R + lax.broadcasted_iota(jnp.int32, (R, 1), 0)
        err = jnp.where(row >= BLOCK, hhat * gf - t_ref[...], 0.0)
        lane0 = (lax.broadcasted_iota(jnp.int32, (8, BLOCK), 0) == 0) & (lax.broadcasted_iota(jnp.int32, (8, BLOCK), 1) == 0)
        loss_ref[...] += jnp.where(lane0, (0.5 / D) * jnp.sum(err * err), 0.0)
        dy = err * (1.0 / D)
        dgf_ref[...] += jnp.sum(dy * hhat, axis=0, keepdims=True)
        dhh = dy * gf
        dh1 = rf * (dhh - hhat * jnp.mean(dhh * hhat, axis=1, keepdims=True))
        dh1b = dh1.astype(BF16)
        dm = _dot_nt(dh1b, wo_ref[...])
        dysb = (dm * s1).astype(BF16)
        dysw = (dm * s2).astype(BF16)
        dusb = _dot_nt(dysb, wsb_ref[...])
        dusw = _dot_nt(dysw, wsw_ref[...])
        dosb_ref[...] = (dusb * silu_sb).astype(BF16)
        dosw_ref[...] = (dusw * silu_sw).astype(BF16)
        dg_ref[:, z0:z1] = (dusb * osb * (sg_sb * (1.0 + sbz * (1.0 - sg_sb)))).astype(BF16)
        dg_ref[:, z1:z2] = (dusw * osw * (sg_sw * (1.0 + swz * (1.0 - sg_sw)))).astype(BF16)
        dg_ref[:, z2:z3] = (dm * y_sb * (s1 * (1.0 - s1))).astype(BF16)
        dg_ref[:, z3:] = (dm * y_sw * (s2 * (1.0 - s2))).astype(BF16)
        dh1_ref[...] = dh1
        mb_ref[...] = mb
        usb_ref[...] = usb
        usw_ref[...] = usw
        dh1b_ref[...] = dh1b
        dysb_ref[...] = dysb
        dysw_ref[...] = dysw

    def rows(n):
        return pl.BlockSpec((R, n), lambda i: (i, 0))

    def whole(shape):
        return pl.BlockSpec(shape, lambda i: (0, 0))

    GW = gates.shape[1]
    return pl.pallas_call(
        body,
        name="tail_fwd_bwd",
        grid=(L // R,),
        in_specs=[rows(D), rows(D), rows(SB_WIDTH), rows(SWA_WIDTH), rows(GW),
                  whole(w_bsb.shape), whole(w_bswa.shape), whole(w_out.shape), whole((1, D))],
        out_specs=[rows(SB_WIDTH), rows(SWA_WIDTH), rows(GW), rows(D),
                   rows(D), rows(SB_WIDTH), rows(SWA_WIDTH), rows(D), rows(D), rows(D),
                   whole((1, D)), whole((8, BLOCK))],
        out_shape=[jax.ShapeDtypeStruct((L, SB_WIDTH), BF16), jax.ShapeDtypeStruct((L, SWA_WIDTH), BF16),
                   jax.ShapeDtypeStruct((L, GW), BF16), jax.ShapeDtypeStruct((L, D), F32),
                   jax.ShapeDtypeStruct((L, D), BF16), jax.ShapeDtypeStruct((L, SB_WIDTH), BF16),
                   jax.ShapeDtypeStruct((L, SWA_WIDTH), BF16), jax.ShapeDtypeStruct((L, D), BF16),
                   jax.ShapeDtypeStruct((L, D), BF16), jax.ShapeDtypeStruct((L, D), BF16),
                   jax.ShapeDtypeStruct((1, D), F32), jax.ShapeDtypeStruct((8, BLOCK), F32)],
        compiler_params=_params(("arbitrary",)),
    )(h0, tgt, o_sb, o_sw, gates, w_bsb, w_bswa, w_out, gain_f)


def local_step(x, tgt, meta, gain, w_in, w_bsb, w_bswa, w_out, sinks, gain_f):
    S, D = x.shape
    L = S + BLOCK
    h0 = jnp.concatenate([jnp.zeros((PAD, D), F32), meta, x], axis=0)
    tgt_p = jnp.concatenate([jnp.zeros((BLOCK, D), F32), tgt], axis=0)
    tt = _suffix_matrix()
    cos, sin = rope_tables(L)
    qkv, xn = in_proj(h0, gain, w_in[:, :SB_COLS], "in_proj_sb", BF16)
    proj_sw = rows_matmul(xn, w_in[:, SB_COLS:GATE_COL0], "in_proj_swa")
    gates = rows_matmul(xn, w_in[:, GATE_COL0:], "in_proj_gates")
    o_sb = sb_attention_fwd(qkv, tt)
    o_sw = swa_attention_fwd(proj_sw, cos, sin, sinks)
    (do_sb, do_sw, dgates, dh1, mb, usb, usw, dh1b, dysb, dysw, dgf, loss) = tail_fwd_bwd(
        h0, tgt_p, o_sb, o_sw, gates, w_bsb, w_bswa, w_out, gain_f)
    dq_sb, dk_sb, dv_sb = sb_attention_bwd(qkv, o_sb, do_sb, tt)
    dq_sw, dk_sw, dv_sw, dsinks = swa_attention_bwd(proj_sw, cos, sin, sinks, do_sw)
    dproj = jnp.concatenate([dq_sb, dk_sb, dv_sb, dq_sw, dk_sw.astype(BF16), dv_sw.astype(BF16), dgates], axis=1)
    dw_in = matmul_tn(xn, dproj, "dw_in")
    dw_out = matmul_tn(mb, dh1b, "dw_out")
    dw_bsb = matmul_tn(usb, dysb, "dw_bsb")
    dw_bswa = matmul_tn(usw, dysw, "dw_bswa")
    dh0, dgain = in_proj_bwd(dproj, w_in, h0, gain, dh1)
    return (loss[0, 0], dh0[BLOCK:], dh0[PAD:BLOCK], dgain, dw_in, dw_bsb, dw_bswa, dw_out,
            dsinks[:1, :SWA_Q_HEADS], dgf)


MESH_IDS = pl.DeviceIdType.MESH
ANY = pl.BlockSpec(memory_space=pl.ANY)


def _place():
    return lax.axis_index("x"), lax.axis_index("y"), lax.axis_index("c")


def _index(x, y, c):
    return 4 * x + 2 * y + c


def all_gather(block, name):
    def body(x_ref, out_ref, send_sems, recv_sems, local_sem):
        x, y, c = _place()
        me, sibling = (x, y, c), (x, y, 1 - c)
        chips = [(1 - x, y), (x, 1 - y), (1 - x, 1 - y)]

        def copy(k, blk, to, src=None):
            dst = out_ref.at[_index(*blk)]
            return pltpu.make_async_remote_copy(
                src_ref=dst if src is None else src, dst_ref=dst,
                send_sem=send_sems.at[k], recv_sem=recv_sems.at[k], device_id=to, device_id_type=MESH_IDS)

        mine = pltpu.make_async_copy(x_ref, out_ref.at[_index(*me)], local_sem)
        mine.start()
        first = [copy(0, me, sibling, src=x_ref)]
        first += [copy(1 + j, me, (*chip, c), src=x_ref) for j, chip in enumerate(chips)]
        for cp in first:
            cp.start()
        passed = [copy(4 + j, (*chip, c), sibling) for j, chip in enumerate(chips)]
        for j, chip in enumerate(chips):
            copy(1 + j, (*chip, c), me).wait_recv()
            passed[j].start()
        copy(0, sibling, me).wait_recv()
        for j, chip in enumerate(chips):
            copy(4 + j, (*chip, 1 - c), me).wait_recv()
        for cp in first + passed:
            cp.wait_send()
        mine.wait()

    return pl.pallas_call(
        body,
        name=name,
        out_shape=jax.ShapeDtypeStruct((N_DEV,) + block.shape, block.dtype),
        in_specs=[ANY],
        out_specs=ANY,
        scratch_shapes=[pltpu.SemaphoreType.DMA((7,)), pltpu.SemaphoreType.DMA((7,)), pltpu.SemaphoreType.DMA],
    )(block)


def exchange_partials(parts):
    def body(g_ref, out_ref, send_sems, recv_sems, local_sem):
        x, y, c = _place()
        me = _index(x, y, c)
        mine = pltpu.make_async_copy(g_ref.at[me], out_ref.at[me], local_sem)
        mine.start()
        copies = []
        for m in range(1, N_DEV):
            px = 1 - x if m & 4 else x
            py = 1 - y if m & 2 else y
            pc = 1 - c if m & 1 else c
            cp = pltpu.make_async_remote_copy(
                src_ref=g_ref.at[_index(px, py, pc)], dst_ref=out_ref.at[me],
                send_sem=send_sems.at[m - 1], recv_sem=recv_sems.at[m - 1],
                device_id=(px, py, pc), device_id_type=MESH_IDS)
            cp.start()
            copies.append(cp)
        for cp in copies:
            cp.wait()
        mine.wait()

    return pl.pallas_call(
        body,
        name="exchange_partials",
        out_shape=jax.ShapeDtypeStruct(parts.shape, parts.dtype),
        in_specs=[ANY],
        out_specs=ANY,
        scratch_shapes=[pltpu.SemaphoreType.DMA((7,)), pltpu.SemaphoreType.DMA((7,)), pltpu.SemaphoreType.DMA],
    )(parts)


def _adamw(w, g, m, v):
    m = ADAM_B1 * m + (1.0 - ADAM_B1) * g
    v = ADAM_B2 * v + (1.0 - ADAM_B2) * (g * g)
    m_hat = m / (1.0 - ADAM_B1 ** ADAM_STEP)
    v_hat = v / (1.0 - ADAM_B2 ** ADAM_STEP)
    delta = -ADAM_LR * (m_hat / (jnp.sqrt(v_hat) + ADAM_EPS) + ADAM_WD * w)
    return delta, m, v


def sum_and_adamw(parts, w, m, v, name):
    _, R, C = parts.shape
    tr = _pick(R, (528, 512, 256, 128, 24, 8))

    def body(p_ref, w_ref, m_ref, v_ref, g_ref, d_ref, nm_ref, nv_ref):
        g = p_ref[0].astype(F32)
        for s in range(1, N_DEV):
            g = g + p_ref[s].astype(F32)
        d, nm, nv = _adamw(w_ref[...], g, m_ref[...], v_ref[...])
        g_ref[...] = g
        d_ref[...] = d
        nm_ref[...] = nm
        nv_ref[...] = nv

    row = pl.BlockSpec((tr, C), lambda i: (i, 0))
    return pl.pallas_call(
        body,
        name=name,
        grid=(R // tr,),
        in_specs=[pl.BlockSpec((N_DEV, tr, C), lambda i: (0, i, 0)), row, row, row],
        out_specs=[row, row, row, row],
        out_shape=[jax.ShapeDtypeStruct((R, C), F32)] * 4,
        compiler_params=_params(("arbitrary",)),
    )(parts, w, m, v)


W_IN_SHARD = IN_COLS // N_DEV
ROWS_W_IN = D_MODEL * W_IN_SHARD // BLOCK
ROWS_W_BSB = SB_WIDTH
ROWS_W_ROWSHARD = D_MODEL
SMALL_ROWS = 24


def _pack_shards(w_in, w_bsb, w_bswa, w_out, meta):
    return jnp.concatenate([w_in.reshape(ROWS_W_IN, BLOCK), w_bsb.reshape(ROWS_W_BSB, BLOCK),
                            w_bswa.reshape(ROWS_W_ROWSHARD, BLOCK), w_out.reshape(ROWS_W_ROWSHARD, BLOCK),
                            meta.reshape(N_META, BLOCK)], axis=0)


def _unpack_shards(p):
    o = np.cumsum([0, ROWS_W_IN, ROWS_W_BSB, ROWS_W_ROWSHARD, ROWS_W_ROWSHARD, N_META])
    return (p[o[0]:o[1]].reshape(1, D_MODEL, W_IN_SHARD), p[o[1]:o[2]].reshape(1, SB_WIDTH, BLOCK),
            p[o[2]:o[3]].reshape(1, BLOCK, D_MODEL), p[o[3]:o[4]].reshape(1, BLOCK, D_MODEL),
            p[o[4]:o[5]].reshape(N_META, BLOCK))


def _pack_by_owner(dw_in, dw_bsb, dw_bswa, dw_out, dmeta):
    cols = lambda a: a.reshape(a.shape[0], N_DEV, -1).transpose(1, 0, 2)
    return jnp.concatenate([cols(dw_in).reshape(N_DEV, ROWS_W_IN, BLOCK), cols(dw_bsb),
                            dw_bswa.reshape(N_DEV, ROWS_W_ROWSHARD, BLOCK), dw_out.reshape(N_DEV, ROWS_W_ROWSHARD, BLOCK),
                            cols(dmeta)], axis=1)


def _pack_small(gain, gain_f, sinks, loss):
    z = jnp.zeros((SMALL_ROWS - 16, BLOCK), F32)
    z = z.at[0, :SWA_Q_HEADS].set(sinks.reshape(-1)).at[1, 0].set(loss)
    return jnp.concatenate([gain.reshape(8, BLOCK), gain_f.reshape(8, BLOCK), z], axis=0)


def _unpack_small(p):
    return p[0:8].reshape(1, D_MODEL), p[8:16].reshape(D_MODEL), p[16:17, :SWA_Q_HEADS], p[17, 0]


def kernel(x, meta_tokens, norm_gain, w_in, w_branch_sb, w_branch_swa, w_out, attn_sinks, final_norm_gain, loss_target, m_meta_tokens, m_norm_gain, m_w_in, m_w_branch_sb, m_w_branch_swa, m_w_out, m_attn_sinks, m_final_norm_gain, v_meta_tokens, v_norm_gain, v_w_in, v_w_branch_sb, v_w_branch_swa, v_w_out, v_attn_sinks, v_final_norm_gain):
    meta_bits = lax.bitcast_convert_type(meta_tokens, BF16).reshape(2 * N_META, BLOCK)
    mine = jnp.concatenate([_pack_shards(w_in, w_branch_sb, w_branch_swa, w_out, meta_tokens)[:-N_META].astype(BF16),
                            meta_bits], axis=0)
    full = all_gather(mine, "all_gather_weights")
    o = np.cumsum([0, ROWS_W_IN, ROWS_W_BSB, ROWS_W_ROWSHARD, ROWS_W_ROWSHARD, 2 * N_META])
    cols = lambda a: a.transpose(1, 0, 2).reshape(a.shape[1], -1)
    f_w_in = cols(full[:, o[0]:o[1]].reshape(N_DEV, D_MODEL, W_IN_SHARD))
    f_w_bsb = cols(full[:, o[1]:o[2]])
    f_w_bswa = full[:, o[2]:o[3]].reshape(D_MODEL, D_MODEL)
    f_w_out = full[:, o[3]:o[4]].reshape(D_MODEL, D_MODEL)
    f_meta = cols(lax.bitcast_convert_type(full[:, o[4]:o[5]].reshape(N_DEV, N_META, BLOCK, 2), F32))

    (loss, grad_x, dmeta, dgain, dw_in, dw_bsb, dw_bswa, dw_out, dsinks, dgf) = local_step(
        x[0], loss_target[0], f_meta, norm_gain, f_w_in, f_w_bsb, f_w_bswa, f_w_out, attn_sinks,
        final_norm_gain.reshape(1, D_MODEL))

    parts = exchange_partials(_pack_by_owner(*[g.astype(BF16) for g in (dw_in, dw_bsb, dw_bswa, dw_out, dmeta)]))
    packs = [_pack_shards(a[0], b[0], c[0], d[0], e) for a, b, c, d, e in (
        (w_in, w_branch_sb, w_branch_swa, w_out, meta_tokens),
        (m_w_in, m_w_branch_sb, m_w_branch_swa, m_w_out, m_meta_tokens),
        (v_w_in, v_w_branch_sb, v_w_branch_swa, v_w_out, v_meta_tokens))]
    big = [_unpack_shards(p) for p in sum_and_adamw(parts, *packs, "sum_adamw_sharded")]

    small = all_gather(_pack_small(dgain, dgf, dsinks, loss), "all_gather_small")
    zero = jnp.zeros((), F32)
    spacks = [_pack_small(a, b, c, zero) for a, b, c in (
        (norm_gain, final_norm_gain, attn_sinks), (m_norm_gain, m_final_norm_gain, m_attn_sinks),
        (v_norm_gain, v_final_norm_gain, v_attn_sinks))]
    sm = [_unpack_small(p) for p in sum_and_adamw(small, *spacks, "sum_adamw_replicated")]

    def leaves(k):
        b, s = big[k], sm[k]
        return (b[4], s[0], b[0], b[1], b[2], b[3], s[2], s[1])

    return (sm[0][3], grad_x[None], *leaves(0), *leaves(1), *leaves(2), *leaves(3))
```

```python
import numpy as np
import jax
import jax.numpy as jnp
from jax import lax
from jax.experimental import pallas as pl
from jax.experimental.pallas import tpu as pltpu

F32 = jnp.float32
BF16 = jnp.bfloat16

D_MODEL = 1024
N_META = 16
BLOCK = 128
PAD = BLOCK - N_META
HEAD_DIM = 64
SB_HEADS = 8
SB_WIDTH = SB_HEADS * HEAD_DIM
SWA_Q_HEADS = 16
SWA_KV_HEADS = 2
SWA_WIDTH = SWA_Q_HEADS * HEAD_DIM
SWA_KV_WIDTH = SWA_KV_HEADS * HEAD_DIM
ROPE_THETA = 10000.0
RMS_EPS = 1e-6
SCALE = HEAD_DIM ** -0.5
SPLITS = (SB_WIDTH, SB_WIDTH, SB_WIDTH, SWA_WIDTH, SWA_KV_WIDTH, SWA_KV_WIDTH,
          SB_WIDTH, SWA_WIDTH, D_MODEL, D_MODEL)
IN_COLS = sum(SPLITS)
SB_COLS = 3 * SB_WIDTH
SWA_COLS = SWA_WIDTH + 2 * SWA_KV_WIDTH
GATE_COL0 = SB_COLS + SWA_COLS

N_DEV = 8
ADAM_LR = 0.001
ADAM_B1 = 0.9
ADAM_B2 = 0.999
ADAM_EPS = 1e-08
ADAM_WD = 0.01
ADAM_STEP = 10

VMEM_LIMIT = 56 * 1024 * 1024


def _params(sem, **kw):
    return pltpu.CompilerParams(dimension_semantics=sem, vmem_limit_bytes=VMEM_LIMIT, **kw)


def _dot(a, b):
    return jnp.dot(a, b, preferred_element_type=F32)


def _dot_nt(a, b):
    return lax.dot_general(a, b, (((1,), (1,)), ((), ())), preferred_element_type=F32)


def _dot_tn(a, b):
    return lax.dot_general(a, b, (((0,), (0,)), ((), ())), preferred_element_type=F32)


def _cat(xs, axis):
    return xs[0] if len(xs) == 1 else jnp.concatenate(xs, axis=axis)


def _split_bf16(x):
    hi = x.astype(BF16)
    lo = (x - hi.astype(F32)).astype(BF16)
    return jnp.concatenate([hi, lo], axis=1)


def _suffix_matrix():
    j = np.arange(BLOCK)[:, None]
    s = np.arange(BLOCK)[None, :]
    t = np.concatenate([(j >= s).astype(np.float32), np.ones((BLOCK, BLOCK), np.float32)], axis=1)
    return jnp.asarray(np.concatenate([t, t], axis=0), dtype=BF16)


def _softplus(z):
    return jnp.maximum(z, 0.0) + jnp.log(1.0 + jnp.exp(-jnp.abs(z)))


SB_SMALL = 4
SB_PAIRS_FWD = 4
SB_PAIRS_BWD = 2


def _sb_masks(i, nblk):
    if nblk == 0:
        r = lax.broadcasted_iota(jnp.int32, (BLOCK, BLOCK), 0)
        c = lax.broadcasted_iota(jnp.int32, (BLOCK, BLOCK), 1)
        return (c < r) & (i * BLOCK + c >= PAD)
    return lax.broadcasted_iota(jnp.int32, (BLOCK, nblk * BLOCK), 1) >= PAD


def _sb_heads(npairs):
    return [(pr, a) for pr in range(npairs) for a in range(2)]


def _lanes(pr):
    return slice(pr * BLOCK, (pr + 1) * BLOCK)


def _masked_heads(x, half0, npairs):
    out = []
    for pr, a in _sb_heads(npairs):
        xp = x[:, _lanes(pr)]
        out.append((jnp.where(half0, xp, 0.0) if a == 0 else jnp.where(half0, 0.0, xp)).astype(BF16))
    return out


def _sb_sweep(i, tile):
    same = lambda alive: alive
    small = SB_SMALL
    live_tile = tile
    tile = lambda j0, nblk, mask, alive: lax.cond(alive, lambda _: live_tile(j0, nblk, mask), same, alive)
    alive = live_tile(i, 1, _sb_masks(i, 0))
    alive = lax.switch(jnp.minimum(i, 3), [
        same,
        lambda al: tile(0, 1, _sb_masks(i, 1), al),
        lambda al: tile(0, 2, _sb_masks(i, 2), al),
        lambda al: tile(i - 2, 2, None, al)], alive)

    def below(alive):
        rest = i - 2
        n_grp = rest // small
        n_one = rest - n_grp * small
        low_is_one = (n_grp == 0) & (n_one > 0)
        n_plain = n_one - jnp.where(low_is_one, 1, 0)
        alive = lax.fori_loop(0, n_plain, lambda t, al: tile(rest - 1 - t, 1, None, al), alive)
        alive = lax.cond(low_is_one, lambda al: tile(0, 1, _sb_masks(i, 1), al), same, alive)
        alive = lax.fori_loop(0, jnp.maximum(n_grp - 1, 0),
                              lambda t, al: tile((n_grp - 1 - t) * small, small, None, al), alive)
        return lax.cond(n_grp > 0, lambda al: tile(0, small, _sb_masks(i, small), al), same, alive)

    lax.cond(alive & (i > 2), below, same, alive)


def _sb_weights(zs, ss, cs, tt, nblk, mask):
    suf = _dot(_cat([_split_bf16(s[:, b * BLOCK:(b + 1) * BLOCK]) for s in ss for b in range(nblk)], 0), tt)
    ws, out_cs = [], []
    for h in range(len(zs)):
        c = cs[h]
        wb = [None] * nblk
        for b in reversed(range(nblk)):
            sab = suf[(h * nblk + b) * BLOCK:(h * nblk + b + 1) * BLOCK]
            wb[b] = jnp.exp(zs[h][:, b * BLOCK:(b + 1) * BLOCK] + c - sab[:, :BLOCK])
            c = c - sab[:, BLOCK:]
        w = _cat(wb, 1)
        if mask is not None:
            w = jnp.where(mask, w, 0.0)
        ws.append(w)
        out_cs.append(c)
    return ws, out_cs


SB_DEAD = -110.0


def _row_norms(x, half0, a):
    xf = x.astype(F32)
    sq = jnp.where(half0, xf * xf, 0.0) if a == 0 else jnp.where(half0, 0.0, xf * xf)
    return jnp.sqrt(jnp.sum(sq, axis=1, keepdims=True))


def _sb_logit_bounds(i, q_heads, k_ref, kmax_ref, half0, heads):
    @pl.when(i == 0)
    def _():
        for h, (pr, a) in enumerate(heads):
            kmax_ref[h] = jnp.max(_row_norms(k_ref[:, _lanes(pr)], half0, a))
    return [jnp.broadcast_to(_row_norms(q_heads[h], half0, a), (BLOCK, BLOCK)) * (kmax_ref[h] * 1.001) + 0.01
            for h, (pr, a) in enumerate(heads)]


def _sb_alive(cs, zmax):
    worst = cs[0] + zmax[0]
    for c, zm in zip(cs[1:], zmax[1:]):
        worst = jnp.maximum(worst, c + zm)
    return jnp.max(worst) > SB_DEAD


def sb_attention_fwd(qkv, tt):
    L = qkv.shape[0]
    nb = L // BLOCK
    NP = SB_PAIRS_FWD
    SB_W = NP * BLOCK
    nq = SB_WIDTH // SB_W

    def body(q_ref, k_ref, v_ref, tt_ref, o_ref, kmax_ref, acc_ref, c_ref):
        i = pl.program_id(1)
        half0 = lax.broadcasted_iota(jnp.int32, (1, BLOCK), 1) < HEAD_DIM
        heads = _sb_heads(NP)
        qh = _masked_heads(q_ref[...].astype(F32) * SCALE, half0, NP)
        zmax = _sb_logit_bounds(i, qh, k_ref, kmax_ref, half0, heads)
        acc_ref[...] = jnp.zeros_like(acc_ref)
        c_ref[...] = jnp.zeros_like(c_ref)

        def rows_of(j0, nblk):
            return pl.ds(pl.multiple_of(j0 * BLOCK, BLOCK), nblk * BLOCK)

        def tile(j0, nblk, mask):
            rows = rows_of(j0, nblk)
            zs = [_dot_nt(qh[h], k_ref[rows, _lanes(pr)]) for h, (pr, a) in enumerate(heads)]
            ss = [_softplus(z) for z in zs]
            if mask is not None:
                ss = [jnp.where(mask, s, 0.0) for s in ss]
            ws, cs = _sb_weights(zs, ss, [c_ref[h] for h in range(len(heads))], tt_ref[...], nblk, mask)
            for h, (pr, a) in enumerate(heads):
                acc_ref[h] += _dot(ws[h].astype(BF16), v_ref[rows, _lanes(pr)])
                c_ref[h] = cs[h]
            return _sb_alive(cs, zmax)

        _sb_sweep(i, tile)
        o_ref[...] = _cat([jnp.where(half0, acc_ref[2 * pr], acc_ref[2 * pr + 1]) for pr in range(NP)], 1)

    panel = lambda c: pl.BlockSpec((L, SB_W), lambda p, i: (0, c * nq + p), pipeline_mode=pl.Buffered(1))
    return pl.pallas_call(
        body,
        name="sb_attention_fwd",
        grid=(nq, nb),
        in_specs=[
            pl.BlockSpec((BLOCK, SB_W), lambda p, i: (i, p)),
            panel(1), panel(2),
            pl.BlockSpec((2 * BLOCK, 2 * BLOCK), lambda p, i: (0, 0)),
        ],
        out_specs=pl.BlockSpec((BLOCK, SB_W), lambda p, i: (i, p)),
        out_shape=jax.ShapeDtypeStruct((L, SB_WIDTH), F32),
        scratch_shapes=[pltpu.SMEM((2 * NP,), F32), pltpu.VMEM((2 * NP, BLOCK, BLOCK), F32),
                        pltpu.VMEM((2 * NP, BLOCK, BLOCK), F32)],
        compiler_params=_params(("arbitrary", "arbitrary")),
    )(qkv, qkv, qkv, tt)


def sb_attention_bwd(qkv, o_sb, do_sb, tt):
    L = qkv.shape[0]
    nb = L // BLOCK
    NP = SB_PAIRS_BWD
    SB_W = NP * BLOCK
    nq = SB_WIDTH // SB_W

    def body(q_ref, k_ref, v_ref, o_ref, do_ref, tt_ref, dq_ref, dk_ref, dv_ref, dk_acc, dv_acc, kmax_ref,
             acc_ref, c_ref, ce_ref):
        i = pl.program_id(1)
        half0 = lax.broadcasted_iota(jnp.int32, (1, BLOCK), 1) < HEAD_DIM
        heads = _sb_heads(NP)

        @pl.when(i == 0)
        def _():
            dk_acc[...] = jnp.zeros_like(dk_acc)
            dv_acc[...] = jnp.zeros_like(dv_acc)

        qh = _masked_heads(q_ref[...].astype(F32) * SCALE, half0, NP)
        zmax = _sb_logit_bounds(i, qh, k_ref, kmax_ref, half0, heads)
        do = do_ref[...]
        doh = _masked_heads(do, half0, NP)
        od = o_ref[...] * do.astype(F32)
        acc_ref[...] = jnp.zeros_like(acc_ref)
        c_ref[...] = jnp.zeros_like(c_ref)
        for h, (pr, a) in enumerate(heads):
            x = od[:, _lanes(pr)]
            x = jnp.where(half0, x, 0.0) if a == 0 else jnp.where(half0, 0.0, x)
            ce_ref[h] = jnp.broadcast_to(jnp.sum(x, axis=1, keepdims=True), (BLOCK, BLOCK))

        def rows_of(j0, nblk):
            return pl.ds(pl.multiple_of(j0 * BLOCK, BLOCK), nblk * BLOCK)

        def tile(j0, nblk, mask):
            cs = [c_ref[h] for h in range(len(heads))]
            ces = [ce_ref[h] for h in range(len(heads))]
            rows = rows_of(j0, nblk)
            zs = [_dot_nt(qh[h], k_ref[rows, _lanes(pr)]) for h, (pr, a) in enumerate(heads)]
            dws = [_dot_nt(doh[h], v_ref[rows, _lanes(pr)]) for h, (pr, a) in enumerate(heads)]
            ss = [_softplus(z) for z in zs]
            sigs = [jnp.exp(z - s) for z, s in zip(zs, ss)]
            if mask is not None:
                ss = [jnp.where(mask, s, 0.0) for s in ss]
            ws, cs = _sb_weights(zs, ss, cs, tt_ref[...], nblk, mask)
            wbs = [w.astype(BF16) for w in ws]
            es = [wb.astype(F32) * dw for wb, dw in zip(wbs, dws)]
            esuf = _dot(_cat([_split_bf16(e[:, b * BLOCK:(b + 1) * BLOCK]) for e in es for b in range(nblk)], 0),
                        tt_ref[...])
            out_ces, dzbs = [], []
            for h, (pr, a) in enumerate(heads):
                ce = ces[h]
                dzs = [None] * nblk
                for b in reversed(range(nblk)):
                    eab = esuf[(h * nblk + b) * BLOCK:(h * nblk + b + 1) * BLOCK]
                    sl = slice(b * BLOCK, (b + 1) * BLOCK)
                    e = es[h][:, sl]
                    dzs[b] = e - sigs[h][:, sl] * (e + (ce - eab[:, :BLOCK]))
                    ce = ce - eab[:, BLOCK:]
                dz = _cat(dzs, 1)
                if mask is not None:
                    dz = jnp.where(mask, dz, 0.0)
                dzbs.append(dz.astype(BF16))
                out_ces.append(ce)
            for h, (pr, a) in enumerate(heads):
                acc_ref[h] += _dot(dzbs[h], k_ref[rows, _lanes(pr)])
                c_ref[h] = cs[h]
                ce_ref[h] = out_ces[h]
            dkv = [_dot_tn(_cat([dzbs[2 * pr], wbs[2 * pr], dzbs[2 * pr + 1], wbs[2 * pr + 1]], 0), qdo[pr])
                   for pr in range(NP)]
            dk_acc[rows, :] += _cat([x[:, :BLOCK] for x in dkv], 1)
            dv_acc[rows, :] += _cat([x[:, BLOCK:] for x in dkv], 1)
            return _sb_alive(cs, zmax)

        zb = jnp.zeros((BLOCK, BLOCK), BF16)
        qdo = [_cat([_cat([qh[h], zb], 1) if kind == 0 else _cat([zb, doh[h]], 1)
                     for h in (2 * pr, 2 * pr + 1) for kind in (0, 1)], 0) for pr in range(NP)]
        _sb_sweep(i, tile)
        dq_ref[...] = (_cat([jnp.where(half0, acc_ref[2 * pr], acc_ref[2 * pr + 1]) for pr in range(NP)], 1)
                       * SCALE).astype(BF16)

        @pl.when(i == nb - 1)
        def _():
            dk_ref[...] = dk_acc[...].astype(BF16)
            dv_ref[...] = dv_acc[...].astype(BF16)

    blk = pl.BlockSpec((BLOCK, SB_W), lambda p, i: (i, p))
    panel = pl.BlockSpec((L, SB_W), lambda p, i: (0, p))
    return pl.pallas_call(
        body,
        name="sb_attention_bwd",
        grid=(nq, nb),
        in_specs=[
            blk,
            pl.BlockSpec((L, SB_W), lambda p, i: (0, nq + p), pipeline_mode=pl.Buffered(1)),
            pl.BlockSpec((L, SB_W), lambda p, i: (0, 2 * nq + p), pipeline_mode=pl.Buffered(1)),
            blk, blk,
            pl.BlockSpec((2 * BLOCK, 2 * BLOCK), lambda p, i: (0, 0)),
        ],
        out_specs=[blk, panel, panel],
        out_shape=[jax.ShapeDtypeStruct((L, SB_WIDTH), BF16)] * 3,
        scratch_shapes=[pltpu.VMEM((L, SB_W), F32), pltpu.VMEM((L, SB_W), F32), pltpu.SMEM((2 * NP,), F32)]
        + [pltpu.VMEM((2 * NP, BLOCK, BLOCK), F32)] * 3,
        compiler_params=_params(("arbitrary", "arbitrary")),
    )(qkv, qkv, qkv, o_sb, do_sb, tt)


SWA_PAIRS = SWA_WIDTH // BLOCK
PAIRS_PER_KV = SWA_PAIRS // SWA_KV_HEADS
CB_SWK = SWA_PAIRS
CB_SWV = SWA_PAIRS + 1


def rope_tables(L):
    half = HEAD_DIM // 2
    inv = ROPE_THETA ** (-jnp.arange(half, dtype=F32) / half)
    pos = (jnp.arange(L) - PAD).astype(F32)
    ang = pos[:, None] * inv[None, :]
    reps = BLOCK // half
    return jnp.tile(jnp.cos(ang), (1, reps)), jnp.tile(jnp.sin(ang), (1, reps))


def _rot_half(x):
    lane = lax.broadcasted_iota(jnp.int32, (1, BLOCK), 1)
    first = (lane % HEAD_DIM) < (HEAD_DIM // 2)
    return jnp.where(first, -pltpu.roll(x, BLOCK - HEAD_DIM // 2, axis=1), pltpu.roll(x, HEAD_DIM // 2, axis=1))


def _rope(x, cos, sin):
    return x * cos + _rot_half(x) * sin


def _unrope(x, cos, sin):
    return x * cos - _rot_half(x) * sin


def _swa_specs():
    prev = lambda n: jnp.maximum(n - 1, 0)
    cur = lambda n: n
    blk = lambda f, c: pl.BlockSpec((BLOCK, BLOCK), lambda n: (f(n), c))
    return [
        pl.BlockSpec((BLOCK, SWA_WIDTH), lambda n: (n, 0)),
        blk(prev, CB_SWK), blk(cur, CB_SWK), blk(prev, CB_SWV), blk(cur, CB_SWV),
        blk(prev, 0), blk(cur, 0), blk(prev, 0), blk(cur, 0),
        pl.BlockSpec(memory_space=pltpu.SMEM),
    ]


def _swa_probs(n, q_ref, kp_ref, kc_ref, vp_ref, vc_ref, cp_ref, cc_ref, sp_ref, sc_ref, sink_ref):
    lane = lax.broadcasted_iota(jnp.int32, (1, BLOCK), 1)
    halves = (lane < HEAD_DIM, lane >= HEAD_DIM)
    cosc, sinc = cc_ref[...], sc_ref[...]
    qs = [_rope(q_ref[:, p * BLOCK:(p + 1) * BLOCK], cosc, sinc) * SCALE for p in range(SWA_PAIRS)]
    kb = jnp.concatenate([_rope(kp_ref[...], cp_ref[...], sp_ref[...]), _rope(kc_ref[...], cosc, sinc)], axis=0)
    vb = jnp.concatenate([vp_ref[...], vc_ref[...]], axis=0)
    kv = {True: (kb, vb), False: (pltpu.roll(kb, HEAD_DIM, axis=1), pltpu.roll(vb, HEAD_DIM, axis=1))}
    rows = PAIRS_PER_KV * BLOCK
    r = lax.broadcasted_iota(jnp.int32, (rows, 2 * BLOCK), 0) % BLOCK
    c = lax.broadcasted_iota(jnp.int32, (rows, 2 * BLOCK), 1)
    valid = (c > r) & (c <= r + BLOCK) & ((n - 1) * BLOCK + c >= PAD)
    combos = [(g, a) for g in range(SWA_KV_HEADS) for a in range(2)]
    qst, ksel, vsel, scores = {}, {}, {}, {}
    for g, a in combos:
        qst[g, a] = jnp.concatenate(
            [jnp.where(halves[a], qs[g * PAIRS_PER_KV + j], 0.0) for j in range(PAIRS_PER_KV)], axis=0).astype(BF16)
        ksel[g, a], vsel[g, a] = kv[g == a]
    for g, a in combos:
        scores[g, a] = _dot_nt(qst[g, a], ksel[g, a].astype(BF16))
    out = {}
    for g, a in combos:
        s = jnp.where(valid, scores[g, a], -1e30)
        sink = jnp.concatenate([jnp.full((BLOCK, BLOCK), sink_ref[0, 2 * (g * PAIRS_PER_KV + j) + a], F32)
                                for j in range(PAIRS_PER_KV)], axis=0)
        mx = jnp.maximum(jnp.max(s, axis=1, keepdims=True), sink)
        pe = jnp.exp(s - _twice(mx))
        es = jnp.exp(sink - mx)
        inv = 1.0 / (_row_sums(pe) + es)
        out[g, a] = (qst[g, a], ksel[g, a], vsel[g, a], pe * _twice(inv), es * inv, halves[a])
    return combos, out


def _twice(x):
    return jnp.concatenate([x, x], axis=1)


def _row_sums(x):
    return _dot(_split_bf16(x), jnp.ones((4 * BLOCK, BLOCK), BF16))


def swa_attention_fwd(proj, cos, sin, sinks):
    L = proj.shape[0]

    def body(q_ref, kp_ref, kc_ref, vp_ref, vc_ref, cp_ref, cc_ref, sp_ref, sc_ref, sink_ref, o_ref):
        n = pl.program_id(0)
        combos, parts = _swa_probs(n, q_ref, kp_ref, kc_ref, vp_ref, vc_ref, cp_ref, cc_ref, sp_ref, sc_ref, sink_ref)
        outs = {}
        for g, a in combos:
            qst, ksel, vsel, probs, psink, half = parts[g, a]
            outs[g, a] = _dot(probs.astype(BF16), jnp.where(half, vsel, 0.0).astype(BF16))
        for g in range(SWA_KV_HEADS):
            both = outs[g, 0] + outs[g, 1]
            for j in range(PAIRS_PER_KV):
                p = g * PAIRS_PER_KV + j
                o_ref[:, p * BLOCK:(p + 1) * BLOCK] = both[j * BLOCK:(j + 1) * BLOCK]

    return pl.pallas_call(
        body,
        name="swa_attention_fwd",
        grid=(L // BLOCK,),
        in_specs=_swa_specs(),
        out_specs=pl.BlockSpec((BLOCK, SWA_WIDTH), lambda n: (n, 0)),
        out_shape=jax.ShapeDtypeStruct((L, SWA_WIDTH), F32),
        compiler_params=_params(("arbitrary",)),
    )(proj, proj, proj, proj, proj, cos, cos, sin, sin, sinks)


def swa_attention_bwd(proj, cos, sin, sinks, do_sw):
    L = proj.shape[0]

    def body(q_ref, kp_ref, kc_ref, vp_ref, vc_ref, cp_ref, cc_ref, sp_ref, sc_ref, sink_ref, do_ref,
             dq_ref, dk_ref, dv_ref, ds_ref):
        n = pl.program_id(0)

        @pl.when(n == 0)
        def _():
            dk_ref[...] = jnp.zeros_like(dk_ref)
            dv_ref[...] = jnp.zeros_like(dv_ref)
            ds_ref[...] = jnp.zeros_like(ds_ref)

        combos, parts = _swa_probs(n, q_ref, kp_ref, kc_ref, vp_ref, vc_ref, cp_ref, cc_ref, sp_ref, sc_ref, sink_ref)
        lane8 = lax.broadcasted_iota(jnp.int32, (8, BLOCK), 1)
        dos, dps = {}, {}
        for g, a in combos:
            half = parts[g, a][5]
            dos[g, a] = jnp.concatenate(
                [jnp.where(half, do_ref[:, (g * PAIRS_PER_KV + j) * BLOCK:(g * PAIRS_PER_KV + j + 1) * BLOCK], 0.0)
                 for j in range(PAIRS_PER_KV)], axis=0).astype(BF16)
        for g, a in combos:
            dps[g, a] = _dot_nt(dos[g, a], parts[g, a][2].astype(BF16))
        dqs = {}
        dkb = jnp.zeros((2 * BLOCK, BLOCK), F32)
        dvb = jnp.zeros((2 * BLOCK, BLOCK), F32)
        dsk = jnp.zeros((8, BLOCK), F32)
        for g, a in combos:
            qst, ksel, vsel, probs, psink, half = parts[g, a]
            dp = dps[g, a]
            delta = _row_sums(probs * dp)
            ds = (probs * (dp - _twice(delta))).astype(BF16)
            pd = psink * delta
            for j in range(PAIRS_PER_KV):
                head = 2 * (g * PAIRS_PER_KV + j) + a
                dsk = dsk + jnp.where(lane8 == head, -jnp.sum(pd[j * BLOCK:(j + 1) * BLOCK, :1]), 0.0)
            dqs[g, a] = _dot(ds, jnp.where(half, ksel, 0.0).astype(BF16))
            dk_a = _dot_tn(ds, qst)
            dv_a = _dot_tn(probs.astype(BF16), dos[g, a])
            if g != a:
                dk_a = pltpu.roll(dk_a, HEAD_DIM, axis=1)
                dv_a = pltpu.roll(dv_a, HEAD_DIM, axis=1)
            dkb = dkb + dk_a
            dvb = dvb + dv_a
        cosc, sinc = cc_ref[...], sc_ref[...]
        for g in range(SWA_KV_HEADS):
            both = (dqs[g, 0] + dqs[g, 1]) * SCALE
            for j in range(PAIRS_PER_KV):
                p = g * PAIRS_PER_KV + j
                dq_ref[:, p * BLOCK:(p + 1) * BLOCK] = _unrope(both[j * BLOCK:(j + 1) * BLOCK], cosc, sinc).astype(BF16)
        ds_ref[...] += dsk
        cur = pl.ds(pl.multiple_of(n * BLOCK, BLOCK), BLOCK)
        dk_ref[cur, :] += _unrope(dkb[BLOCK:], cosc, sinc)
        dv_ref[cur, :] += dvb[BLOCK:]

        @pl.when(n > 0)
        def _():
            prv = pl.ds(pl.multiple_of((n - 1) * BLOCK, BLOCK), BLOCK)
            dk_ref[prv, :] += _unrope(dkb[:BLOCK], cp_ref[...], sp_ref[...])
            dv_ref[prv, :] += dvb[:BLOCK]

    whole = lambda n: (0, 0)
    row = pl.BlockSpec((BLOCK, SWA_WIDTH), lambda n: (n, 0))
    return pl.pallas_call(
        body,
        name="swa_attention_bwd",
        grid=(L // BLOCK,),
        in_specs=_swa_specs() + [row],
        out_specs=[row, pl.BlockSpec((L, BLOCK), whole), pl.BlockSpec((L, BLOCK), whole),
                   pl.BlockSpec((8, BLOCK), whole)],
        out_shape=[jax.ShapeDtypeStruct((L, SWA_WIDTH), BF16), jax.ShapeDtypeStruct((L, BLOCK), F32),
                   jax.ShapeDtypeStruct((L, BLOCK), F32), jax.ShapeDtypeStruct((8, BLOCK), F32)],
        compiler_params=_params(("arbitrary",)),
    )(proj, proj, proj, proj, proj, cos, cos, sin, sin, sinks, do_sw)


ROW_TILE = 640
TAIL_ROWS = 320


def _pick(n, cands):
    for c in cands:
        if n % c == 0:
            return c
    raise ValueError(f"no tile for {n}")


def in_proj(h0, gain, w, name, out_dtype):
    L, D = h0.shape
    N = w.shape[1]
    tm = _pick(L, (ROW_TILE, BLOCK))
    tn = _pick(N, (1792, 1536, 1280, 896, 640, 512, 384, 256, 128))

    def body(h_ref, g_ref, w_ref, o_ref, xn_ref):
        @pl.when(pl.program_id(1) == 0)
        def _():
            x = h_ref[...]
            r = lax.rsqrt(jnp.mean(x * x, axis=1, keepdims=True) + RMS_EPS)
            xn_ref[...] = ((x * r) * g_ref[...]).astype(BF16)
        o_ref[...] = _dot(xn_ref[...], w_ref[...]).astype(out_dtype)

    return pl.pallas_call(
        body,
        name=name,
        grid=(L // tm, N // tn),
        in_specs=[pl.BlockSpec((tm, D), lambda i, j: (i, 0)),
                  pl.BlockSpec((1, D), lambda i, j: (0, 0)),
                  pl.BlockSpec((D, tn), lambda i, j: (0, j))],
        out_specs=[pl.BlockSpec((tm, tn), lambda i, j: (i, j)),
                   pl.BlockSpec((tm, D), lambda i, j: (i, 0))],
        out_shape=[jax.ShapeDtypeStruct((L, N), out_dtype), jax.ShapeDtypeStruct((L, D), BF16)],
        compiler_params=_params(("arbitrary", "arbitrary")),
    )(h0, gain, w)


def rows_matmul(a, w, name):
    L, D = a.shape
    N = w.shape[1]
    tm = _pick(L, (ROW_TILE, BLOCK))
    tn = _pick(N, (1792, 1536, 1280, 896, 640, 512, 384, 256, 128))

    def body(a_ref, w_ref, o_ref):
        o_ref[...] = _dot(a_ref[...], w_ref[...])

    return pl.pallas_call(
        body,
        name=name,
        grid=(L // tm, N // tn),
        in_specs=[pl.BlockSpec((tm, D), lambda i, j: (i, 0)), pl.BlockSpec((D, tn), lambda i, j: (0, j))],
        out_specs=pl.BlockSpec((tm, tn), lambda i, j: (i, j)),
        out_shape=jax.ShapeDtypeStruct((L, N), F32),
        compiler_params=_params(("arbitrary", "arbitrary")),
    )(a, w)


def matmul_tn(a, b, name):
    Kd, M = a.shape
    N = b.shape[1]
    tk = _pick(Kd, (ROW_TILE, BLOCK))
    tn = _pick(N, (1280, 1024, 896, 640, 512, 256, 128))
    nk = Kd // tk

    def body(a_ref, b_ref, o_ref):
        k = pl.program_id(1)

        @pl.when(k == 0)
        def _():
            o_ref[...] = jnp.zeros_like(o_ref)
        o_ref[...] += _dot_tn(a_ref[...], b_ref[...])

    return pl.pallas_call(
        body,
        name=name,
        grid=(N // tn, nk),
        in_specs=[pl.BlockSpec((tk, M), lambda j, k: (k, 0)),
                  pl.BlockSpec((tk, tn), lambda j, k: (k, j))],
        out_specs=pl.BlockSpec((M, tn), lambda j, k: (0, j)),
        out_shape=jax.ShapeDtypeStruct((M, N), F32),
        compiler_params=_params(("arbitrary", "arbitrary")),
    )(a, b)


def in_proj_bwd(dproj, w, h0, gain, dh1):
    L, N = dproj.shape
    D = w.shape[0]
    tm = _pick(L, (ROW_TILE, BLOCK))
    tk = _pick(N, (3200, 1280, 640, 512, 256, 128))
    nk = N // tk

    def body(dp_ref, w_ref, h_ref, g_ref, dh1_ref, dh0_ref, dg_ref, acc_ref):
        i, k = pl.program_id(0), pl.program_id(1)

        @pl.when(k == 0)
        def _():
            acc_ref[...] = jnp.zeros_like(acc_ref)

        @pl.when((i == 0) & (k == 0))
        def _():
            dg_ref[...] = jnp.zeros_like(dg_ref)

        acc_ref[...] += _dot_nt(dp_ref[...], w_ref[...])

        @pl.when(k == nk - 1)
        def _():
            x = h_ref[...]
            r = lax.rsqrt(jnp.mean(x * x, axis=1, keepdims=True) + RMS_EPS)
            xhat = x * r
            dxn = acc_ref[...]
            dg_ref[...] += jnp.sum(dxn * xhat, axis=0, keepdims=True)
            dxh = dxn * g_ref[...]
            dh0_ref[...] = r * (dxh - xhat * jnp.mean(dxh * xhat, axis=1, keepdims=True)) + dh1_ref[...]

    row = pl.BlockSpec((tm, D), lambda i, k: (i, 0))
    vec = pl.BlockSpec((1, D), lambda i, k: (0, 0))
    return pl.pallas_call(
        body,
        name="in_proj_bwd",
        grid=(L // tm, nk),
        in_specs=[pl.BlockSpec((tm, tk), lambda i, k: (i, k)),
                  pl.BlockSpec((D, tk), lambda i, k: (0, k)),
                  row, vec, row],
        out_specs=[row, vec],
        out_shape=[jax.ShapeDtypeStruct((L, D), F32), jax.ShapeDtypeStruct((1, D), F32)],
        scratch_shapes=[pltpu.VMEM((tm, D), F32)],
        compiler_params=_params(("arbitrary", "arbitrary")),
    )(dproj, w, h0, gain, dh1)


def tail_fwd_bwd(h0, tgt, o_sb, o_sw, gates, w_bsb, w_bswa, w_out, gain_f):
    L, D = h0.shape
    R = _pick(L, (TAIL_ROWS, BLOCK))
    z0, z1, z2, z3 = 0, SB_WIDTH, SB_WIDTH + SWA_WIDTH, SB_WIDTH + SWA_WIDTH + D_MODEL

    def body(h_ref, t_ref, osb_ref, osw_ref, g_ref, wsb_ref, wsw_ref, wo_ref, gf_ref,
             dosb_ref, dosw_ref, dg_ref, dh1_ref, mb_ref, usb_ref, usw_ref, dh1b_ref, dysb_ref, dysw_ref,
             dgf_ref, loss_ref):
        i = pl.program_id(0)

        @pl.when(i == 0)
        def _():
            dgf_ref[...] = jnp.zeros_like(dgf_ref)
            loss_ref[...] = jnp.zeros_like(loss_ref)

        sbz = g_ref[:, z0:z1]
        swz = g_ref[:, z1:z2]
        s1 = jax.nn.sigmoid(g_ref[:, z2:z3])
        s2 = jax.nn.sigmoid(g_ref[:, z3:])
        sg_sb = jax.nn.sigmoid(sbz)
        sg_sw = jax.nn.sigmoid(swz)
        silu_sb = sbz * sg_sb
        silu_sw = swz * sg_sw
        osb = osb_ref[...]
        osw = osw_ref[...]
        usb = (osb * silu_sb).astype(BF16)
        usw = (osw * silu_sw).astype(BF16)
        y_sb = _dot(usb, wsb_ref[...])
        y_sw = _dot(usw, wsw_ref[...])
        mb = (s1 * y_sb + s2 * y_sw).astype(BF16)
        h1 = h_ref[...] + _dot(mb, wo_ref[...])
        rf = lax.rsqrt(jnp.mean(h1 * h1, axis=1, keepdims=True) + RMS_EPS)
        hhat = h1 * rf
        gf = gf_ref[...]
        row = i * R + lax.broadcasted_iota(jnp.int32, (R, 1), 0)
        err = jnp.where(row >= BLOCK, hhat * gf - t_ref[...], 0.0)
        lane0 = (lax.broadcasted_iota(jnp.int32, (8, BLOCK), 0) == 0) & (lax.broadcasted_iota(jnp.int32, (8, BLOCK), 1) == 0)
        loss_ref[...] += jnp.where(lane0, (0.5 / D) * jnp.sum(err * err), 0.0)
        dy = err * (1.0 / D)
        dgf_ref[...] += jnp.sum(dy * hhat, axis=0, keepdims=True)
        dhh = dy * gf
        dh1 = rf * (dhh - hhat * jnp.mean(dhh * hhat, axis=1, keepdims=True))
        dh1b = dh1.astype(BF16)
        dm = _dot_nt(dh1b, wo_ref[...])
        dysb = (dm * s1).astype(BF16)
        dysw = (dm * s2).astype(BF16)
        dusb = _dot_nt(dysb, wsb_ref[...])
        dusw = _dot_nt(dysw, wsw_ref[...])
        dosb_ref[...] = (dusb * silu_sb).astype(BF16)
        dosw_ref[...] = (dusw * silu_sw).astype(BF16)
        dg_ref[:, z0:z1] = (dusb * osb * (sg_sb * (1.0 + sbz * (1.0 - sg_sb)))).astype(BF16)
        dg_ref[:, z1:z2] = (dusw * osw * (sg_sw * (1.0 + swz * (1.0 - sg_sw)))).astype(BF16)
        dg_ref[:, z2:z3] = (dm * y_sb * (s1 * (1.0 - s1))).astype(BF16)
        dg_ref[:, z3:] = (dm * y_sw * (s2 * (1.0 - s2))).astype(BF16)
        dh1_ref[...] = dh1
        mb_ref[...] = mb
        usb_ref[...] = usb
        usw_ref[...] = usw
        dh1b_ref[...] = dh1b
        dysb_ref[...] = dysb
        dysw_ref[...] = dysw

    def rows(n):
        return pl.BlockSpec((R, n), lambda i: (i, 0))

    def whole(shape):
        return pl.BlockSpec(shape, lambda i: (0, 0))

    GW = gates.shape[1]
    return pl.pallas_call(
        body,
        name="tail_fwd_bwd",
        grid=(L // R,),
        in_specs=[rows(D), rows(D), rows(SB_WIDTH), rows(SWA_WIDTH), rows(GW),
                  whole(w_bsb.shape), whole(w_bswa.shape), whole(w_out.shape), whole((1, D))],
        out_specs=[rows(SB_WIDTH), rows(SWA_WIDTH), rows(GW), rows(D),
                   rows(D), rows(SB_WIDTH), rows(SWA_WIDTH), rows(D), rows(D), rows(D),
                   whole((1, D)), whole((8, BLOCK))],
        out_shape=[jax.ShapeDtypeStruct((L, SB_WIDTH), BF16), jax.ShapeDtypeStruct((L, SWA_WIDTH), BF16),
                   jax.ShapeDtypeStruct((L, GW), BF16), jax.ShapeDtypeStruct((L, D), F32),
                   jax.ShapeDtypeStruct((L, D), BF16), jax.ShapeDtypeStruct((L, SB_WIDTH), BF16),
                   jax.ShapeDtypeStruct((L, SWA_WIDTH), BF16), jax.ShapeDtypeStruct((L, D), BF16),
                   jax.ShapeDtypeStruct((L, D), BF16), jax.ShapeDtypeStruct((L, D), BF16),
                   jax.ShapeDtypeStruct((1, D), F32), jax.ShapeDtypeStruct((8, BLOCK), F32)],
        compiler_params=_params(("arbitrary",)),
    )(h0, tgt, o_sb, o_sw, gates, w_bsb, w_bswa, w_out, gain_f)


def local_step(x, tgt, meta, gain, w_in, w_bsb, w_bswa, w_out, sinks, gain_f):
    S, D = x.shape
    L = S + BLOCK
    h0 = jnp.concatenate([jnp.zeros((PAD, D), F32), meta, x], axis=0)
    tgt_p = jnp.concatenate([jnp.zeros((BLOCK, D), F32), tgt], axis=0)
    tt = _suffix_matrix()
    cos, sin = rope_tables(L)
    qkv, xn = in_proj(h0, gain, w_in[:, :SB_COLS], "in_proj_sb", BF16)
    proj_sw = rows_matmul(xn, w_in[:, SB_COLS:GATE_COL0], "in_proj_swa")
    gates = rows_matmul(xn, w_in[:, GATE_COL0:], "in_proj_gates")
    o_sb = sb_attention_fwd(qkv, tt)
    o_sw = swa_attention_fwd(proj_sw, cos, sin, sinks)
    (do_sb, do_sw, dgates, dh1, mb, usb, usw, dh1b, dysb, dysw, dgf, loss) = tail_fwd_bwd(
        h0, tgt_p, o_sb, o_sw, gates, w_bsb, w_bswa, w_out, gain_f)
    dq_sb, dk_sb, dv_sb = sb_attention_bwd(qkv, o_sb, do_sb, tt)
    dq_sw, dk_sw, dv_sw, dsinks = swa_attention_bwd(proj_sw, cos, sin, sinks, do_sw)
    dproj = jnp.concatenate([dq_sb, dk_sb, dv_sb, dq_sw, dk_sw.astype(BF16), dv_sw.astype(BF16), dgates], axis=1)
    dw_in = matmul_tn(xn, dproj, "dw_in")
    dw_out = matmul_tn(mb, dh1b, "dw_out")
    dw_bsb = matmul_tn(usb, dysb, "dw_bsb")
    dw_bswa = matmul_tn(usw, dysw, "dw_bswa")
    dh0, dgain = in_proj_bwd(dproj, w_in, h0, gain, dh1)
    return (loss[0, 0], dh0[BLOCK:], dh0[PAD:BLOCK], dgain, dw_in, dw_bsb, dw_bswa, dw_out,
            dsinks[:1, :SWA_Q_HEADS], dgf)


MESH_IDS = pl.DeviceIdType.MESH
ANY = pl.BlockSpec(memory_space=pl.ANY)


def _place():
    return lax.axis_index("x"), lax.axis_index("y"), lax.axis_index("c")


def _index(x, y, c):
    return 4 * x + 2 * y + c


def all_gather(block, name):
    def body(x_ref, out_ref, send_sems, recv_sems, local_sem):
        x, y, c = _place()
        me, sibling = (x, y, c), (x, y, 1 - c)
        chips = [(1 - x, y), (x, 1 - y), (1 - x, 1 - y)]

        def copy(k, blk, to, src=None):
            dst = out_ref.at[_index(*blk)]
            return pltpu.make_async_remote_copy(
                src_ref=dst if src is None else src, dst_ref=dst,
                send_sem=send_sems.at[k], recv_sem=recv_sems.at[k], device_id=to, device_id_type=MESH_IDS)

        mine = pltpu.make_async_copy(x_ref, out_ref.at[_index(*me)], local_sem)
        mine.start()
        first = [copy(0, me, sibling, src=x_ref)]
        first += [copy(1 + j, me, (*chip, c), src=x_ref) for j, chip in enumerate(chips)]
        for cp in first:
            cp.start()
        passed = [copy(4 + j, (*chip, c), sibling) for j, chip in enumerate(chips)]
        for j, chip in enumerate(chips):
            copy(1 + j, (*chip, c), me).wait_recv()
            passed[j].start()
        copy(0, sibling, me).wait_recv()
        for j, chip in enumerate(chips):
            copy(4 + j, (*chip, 1 - c), me).wait_recv()
        for cp in first + passed:
            cp.wait_send()
        mine.wait()

    return pl.pallas_call(
        body,
        name=name,
        out_shape=jax.ShapeDtypeStruct((N_DEV,) + block.shape, block.dtype),
        in_specs=[ANY],
        out_specs=ANY,
        scratch_shapes=[pltpu.SemaphoreType.DMA((7,)), pltpu.SemaphoreType.DMA((7,)), pltpu.SemaphoreType.DMA],
    )(block)


def exchange_partials(parts):
    def body(g_ref, out_ref, send_sems, recv_sems, local_sem):
        x, y, c = _place()
        me = _index(x, y, c)
        mine = pltpu.make_async_copy(g_ref.at[me], out_ref.at[me], local_sem)
        mine.start()
        copies = []
        for m in range(1, N_DEV):
            px = 1 - x if m & 4 else x
            py = 1 - y if m & 2 else y
            pc = 1 - c if m & 1 else c
            cp = pltpu.make_async_remote_copy(
                src_ref=g_ref.at[_index(px, py, pc)], dst_ref=out_ref.at[me],
                send_sem=send_sems.at[m - 1], recv_sem=recv_sems.at[m - 1],
                device_id=(px, py, pc), device_id_type=MESH_IDS)
            cp.start()
            copies.append(cp)
        for cp in copies:
            cp.wait()
        mine.wait()

    return pl.pallas_call(
        body,
        name="exchange_partials",
        out_shape=jax.ShapeDtypeStruct(parts.shape, parts.dtype),
        in_specs=[ANY],
        out_specs=ANY,
        scratch_shapes=[pltpu.SemaphoreType.DMA((7,)), pltpu.SemaphoreType.DMA((7,)), pltpu.SemaphoreType.DMA],
    )(parts)


def _adamw(w, g, m, v):
    m = ADAM_B1 * m + (1.0 - ADAM_B1) * g
    v = ADAM_B2 * v + (1.0 - ADAM_B2) * (g * g)
    m_hat = m / (1.0 - ADAM_B1 ** ADAM_STEP)
    v_hat = v / (1.0 - ADAM_B2 ** ADAM_STEP)
    delta = -ADAM_LR * (m_hat / (jnp.sqrt(v_hat) + ADAM_EPS) + ADAM_WD * w)
    return delta, m, v


def sum_and_adamw(parts, w, m, v, name):
    _, R, C = parts.shape
    tr = _pick(R, (528, 512, 256, 128, 24, 8))

    def body(p_ref, w_ref, m_ref, v_ref, g_ref, d_ref, nm_ref, nv_ref):
        g = p_ref[0].astype(F32)
        for s in range(1, N_DEV):
            g = g + p_ref[s].astype(F32)
        d, nm, nv = _adamw(w_ref[...], g, m_ref[...], v_ref[...])
        g_ref[...] = g
        d_ref[...] = d
        nm_ref[...] = nm
        nv_ref[...] = nv

    row = pl.BlockSpec((tr, C), lambda i: (i, 0))
    return pl.pallas_call(
        body,
        name=name,
        grid=(R // tr,),
        in_specs=[pl.BlockSpec((N_DEV, tr, C), lambda i: (0, i, 0)), row, row, row],
        out_specs=[row, row, row, row],
        out_shape=[jax.ShapeDtypeStruct((R, C), F32)] * 4,
        compiler_params=_params(("arbitrary",)),
    )(parts, w, m, v)


W_IN_SHARD = IN_COLS // N_DEV
ROWS_W_IN = D_MODEL * W_IN_SHARD // BLOCK
ROWS_W_BSB = SB_WIDTH
ROWS_W_ROWSHARD = D_MODEL
SMALL_ROWS = 24


def _pack_shards(w_in, w_bsb, w_bswa, w_out, meta):
    return jnp.concatenate([w_in.reshape(ROWS_W_IN, BLOCK), w_bsb.reshape(ROWS_W_BSB, BLOCK),
                            w_bswa.reshape(ROWS_W_ROWSHARD, BLOCK), w_out.reshape(ROWS_W_ROWSHARD, BLOCK),
                            meta.reshape(N_META, BLOCK)], axis=0)


def _unpack_shards(p):
    o = np.cumsum([0, ROWS_W_IN, ROWS_W_BSB, ROWS_W_ROWSHARD, ROWS_W_ROWSHARD, N_META])
    return (p[o[0]:o[1]].reshape(1, D_MODEL, W_IN_SHARD), p[o[1]:o[2]].reshape(1, SB_WIDTH, BLOCK),
            p[o[2]:o[3]].reshape(1, BLOCK, D_MODEL), p[o[3]:o[4]].reshape(1, BLOCK, D_MODEL),
            p[o[4]:o[5]].reshape(N_META, BLOCK))


def _pack_by_owner(dw_in, dw_bsb, dw_bswa, dw_out, dmeta):
    cols = lambda a: a.reshape(a.shape[0], N_DEV, -1).transpose(1, 0, 2)
    return jnp.concatenate([cols(dw_in).reshape(N_DEV, ROWS_W_IN, BLOCK), cols(dw_bsb),
                            dw_bswa.reshape(N_DEV, ROWS_W_ROWSHARD, BLOCK), dw_out.reshape(N_DEV, ROWS_W_ROWSHARD, BLOCK),
                            cols(dmeta)], axis=1)


def _pack_small(gain, gain_f, sinks, loss):
    z = jnp.zeros((SMALL_ROWS - 16, BLOCK), F32)
    z = z.at[0, :SWA_Q_HEADS].set(sinks.reshape(-1)).at[1, 0].set(loss)
    return jnp.concatenate([gain.reshape(8, BLOCK), gain_f.reshape(8, BLOCK), z], axis=0)


def _unpack_small(p):
    return p[0:8].reshape(1, D_MODEL), p[8:16].reshape(D_MODEL), p[16:17, :SWA_Q_HEADS], p[17, 0]


def kernel(x, meta_tokens, norm_gain, w_in, w_branch_sb, w_branch_swa, w_out, attn_sinks, final_norm_gain, loss_target, m_meta_tokens, m_norm_gain, m_w_in, m_w_branch_sb, m_w_branch_swa, m_w_out, m_attn_sinks, m_final_norm_gain, v_meta_tokens, v_norm_gain, v_w_in, v_w_branch_sb, v_w_branch_swa, v_w_out, v_attn_sinks, v_final_norm_gain):
    meta_bits = lax.bitcast_convert_type(meta_tokens, BF16).reshape(2 * N_META, BLOCK)
    mine = jnp.concatenate([_pack_shards(w_in, w_branch_sb, w_branch_swa, w_out, meta_tokens)[:-N_META].astype(BF16),
                            meta_bits], axis=0)
    full = all_gather(mine, "all_gather_weights")
    o = np.cumsum([0, ROWS_W_IN, ROWS_W_BSB, ROWS_W_ROWSHARD, ROWS_W_ROWSHARD, 2 * N_META])
    cols = lambda a: a.transpose(1, 0, 2).reshape(a.shape[1], -1)
    f_w_in = cols(full[:, o[0]:o[1]].reshape(N_DEV, D_MODEL, W_IN_SHARD))
    f_w_bsb = cols(full[:, o[1]:o[2]])
    f_w_bswa = full[:, o[2]:o[3]].reshape(D_MODEL, D_MODEL)
    f_w_out = full[:, o[3]:o[4]].reshape(D_MODEL, D_MODEL)
    f_meta = cols(lax.bitcast_convert_type(full[:, o[4]:o[5]].reshape(N_DEV, N_META, BLOCK, 2), F32))

    (loss, grad_x, dmeta, dgain, dw_in, dw_bsb, dw_bswa, dw_out, dsinks, dgf) = local_step(
        x[0], loss_target[0], f_meta, norm_gain, f_w_in, f_w_bsb, f_w_bswa, f_w_out, attn_sinks,
        final_norm_gain.reshape(1, D_MODEL))

    parts = exchange_partials(_pack_by_owner(*[g.astype(BF16) for g in (dw_in, dw_bsb, dw_bswa, dw_out, dmeta)]))
    packs = [_pack_shards(a[0], b[0], c[0], d[0], e) for a, b, c, d, e in (
        (w_in, w_branch_sb, w_branch_swa, w_out, meta_tokens),
        (m_w_in, m_w_branch_sb, m_w_branch_swa, m_w_out, m_meta_tokens),
        (v_w_in, v_w_branch_sb, v_w_branch_swa, v_w_out, v_meta_tokens))]
    big = [_unpack_shards(p) for p in sum_and_adamw(parts, *packs, "sum_adamw_sharded")]

    small = all_gather(_pack_small(dgain, dgf, dsinks, loss), "all_gather_small")
    zero = jnp.zeros((), F32)
    spacks = [_pack_small(a, b, c, zero) for a, b, c in (
        (norm_gain, final_norm_gain, attn_sinks), (m_norm_gain, m_final_norm_gain, m_attn_sinks),
        (v_norm_gain, v_final_norm_gain, v_attn_sinks))]
    sm = [_unpack_small(p) for p in sum_and_adamw(small, *spacks, "sum_adamw_replicated")]

    def leaves(k):
        b, s = big[k], sm[k]
        return (b[4], s[0], b[0], b[1], b[2], b[3], s[2], s[1])

    return (sm[0][3], grad_x[None], *leaves(0), *leaves(1), *leaves(2), *leaves(3))
```

```python
import numpy as np
import jax
import jax.numpy as jnp
from jax import lax
from jax.experimental import pallas as pl
from jax.experimental.pallas import tpu as pltpu

F32 = jnp.float32
BF16 = jnp.bfloat16

D_MODEL = 1024
N_META = 16
BLOCK = 128
PAD = BLOCK - N_META
HEAD_DIM = 64
SB_HEADS = 8
SB_WIDTH = SB_HEADS * HEAD_DIM
SWA_Q_HEADS = 16
SWA_KV_HEADS = 2
SWA_WIDTH = SWA_Q_HEADS * HEAD_DIM
SWA_KV_WIDTH = SWA_KV_HEADS * HEAD_DIM
ROPE_THETA = 10000.0
RMS_EPS = 1e-6
SCALE = HEAD_DIM ** -0.5
SPLITS = (SB_WIDTH, SB_WIDTH, SB_WIDTH, SWA_WIDTH, SWA_KV_WIDTH, SWA_KV_WIDTH,
          SB_WIDTH, SWA_WIDTH, D_MODEL, D_MODEL)
IN_COLS = sum(SPLITS)
SB_COLS = 3 * SB_WIDTH
SWA_COLS = SWA_WIDTH + 2 * SWA_KV_WIDTH
GATE_COL0 = SB_COLS + SWA_COLS

N_DEV = 8
ADAM_LR = 0.001
ADAM_B1 = 0.9
ADAM_B2 = 0.999
ADAM_EPS = 1e-08
ADAM_WD = 0.01
ADAM_STEP = 10

VMEM_LIMIT = 56 * 1024 * 1024


def _params(sem, **kw):
    return pltpu.CompilerParams(dimension_semantics=sem, vmem_limit_bytes=VMEM_LIMIT, **kw)


def _dot(a, b):
    return jnp.dot(a, b, preferred_element_type=F32)


def _dot_nt(a, b):
    return lax.dot_general(a, b, (((1,), (1,)), ((), ())), preferred_element_type=F32)


def _dot_tn(a, b):
    return lax.dot_general(a, b, (((0,), (0,)), ((), ())), preferred_element_type=F32)


def _cat(xs, axis):
    return xs[0] if len(xs) == 1 else jnp.concatenate(xs, axis=axis)


def _split_bf16(x):
    hi = x.astype(BF16)
    lo = (x - hi.astype(F32)).astype(BF16)
    return jnp.concatenate([hi, lo], axis=1)


def _suffix_matrix():
    j = np.arange(BLOCK)[:, None]
    s = np.arange(BLOCK)[None, :]
    t = np.concatenate([(j >= s).astype(np.float32), np.ones((BLOCK, BLOCK), np.float32)], axis=1)
    return jnp.asarray(np.concatenate([t, t], axis=0), dtype=BF16)


def _softplus(z):
    return jnp.maximum(z, 0.0) + jnp.log(1.0 + jnp.exp(-jnp.abs(z)))


SB_SMALL = 4
SB_PAIRS_FWD = 4
SB_PAIRS_BWD = 2


def _sb_masks(i, nblk):
    if nblk == 0:
        r = lax.broadcasted_iota(jnp.int32, (BLOCK, BLOCK), 0)
        c = lax.broadcasted_iota(jnp.int32, (BLOCK, BLOCK), 1)
        return (c < r) & (i * BLOCK + c >= PAD)
    return lax.broadcasted_iota(jnp.int32, (BLOCK, nblk * BLOCK), 1) >= PAD


def _sb_heads(npairs):
    return [(pr, a) for pr in range(npairs) for a in range(2)]


def _lanes(pr):
    return slice(pr * BLOCK, (pr + 1) * BLOCK)


def _masked_heads(x, half0, npairs):
    out = []
    for pr, a in _sb_heads(npairs):
        xp = x[:, _lanes(pr)]
        out.append((jnp.where(half0, xp, 0.0) if a == 0 else jnp.where(half0, 0.0, xp)).astype(BF16))
    return out


def _sb_sweep(i, tile):
    same = lambda alive: alive
    small = SB_SMALL
    live_tile = tile
    tile = lambda j0, nblk, mask, alive: lax.cond(alive, lambda _: live_tile(j0, nblk, mask), same, alive)
    alive = live_tile(i, 1, _sb_masks(i, 0))
    alive = lax.switch(jnp.minimum(i, 3), [
        same,
        lambda al: tile(0, 1, _sb_masks(i, 1), al),
        lambda al: tile(0, 2, _sb_masks(i, 2), al),
        lambda al: tile(i - 2, 2, None, al)], alive)

    def below(alive):
        rest = i - 2
        n_grp = rest // small
        n_one = rest - n_grp * small
        low_is_one = (n_grp == 0) & (n_one > 0)
        n_plain = n_one - jnp.where(low_is_one, 1, 0)
        alive = lax.fori_loop(0, n_plain, lambda t, al: tile(rest - 1 - t, 1, None, al), alive)
        alive = lax.cond(low_is_one, lambda al: tile(0, 1, _sb_masks(i, 1), al), same, alive)
        alive = lax.fori_loop(0, jnp.maximum(n_grp - 1, 0),
                              lambda t, al: tile((n_grp - 1 - t) * small, small, None, al), alive)
        return lax.cond(n_grp > 0, lambda al: tile(0, small, _sb_masks(i, small), al), same, alive)

    lax.cond(alive & (i > 2), below, same, alive)


def _sb_weights(zs, ss, cs, tt, nblk, mask):
    suf = _dot(_cat([_split_bf16(s[:, b * BLOCK:(b + 1) * BLOCK]) for s in ss for b in range(nblk)], 0), tt)
    ws, out_cs = [], []
    for h in range(len(zs)):
        c = cs[h]
        wb = [None] * nblk
        for b in reversed(range(nblk)):
            sab = suf[(h * nblk + b) * BLOCK:(h * nblk + b + 1) * BLOCK]
            wb[b] = jnp.exp(zs[h][:, b * BLOCK:(b + 1) * BLOCK] + c - sab[:, :BLOCK])
            c = c - sab[:, BLOCK:]
        w = _cat(wb, 1)
        if mask is not None:
            w = jnp.where(mask, w, 0.0)
        ws.append(w)
        out_cs.append(c)
    return ws, out_cs


SB_DEAD = -110.0


def _row_norms(x, half0, a):
    xf = x.astype(F32)
    sq = jnp.where(half0, xf * xf, 0.0) if a == 0 else jnp.where(half0, 0.0, xf * xf)
    return jnp.sqrt(jnp.sum(sq, axis=1, keepdims=True))


def _sb_logit_bounds(i, q_heads, k_ref, kmax_ref, half0, heads):
    @pl.when(i == 0)
    def _():
        for h, (pr, a) in enumerate(heads):
            kmax_ref[h] = jnp.max(_row_norms(k_ref[:, _lanes(pr)], half0, a))
    return [jnp.broadcast_to(_row_norms(q_heads[h], half0, a), (BLOCK, BLOCK)) * (kmax_ref[h] * 1.001) + 0.01
            for h, (pr, a) in enumerate(heads)]


def _sb_alive(cs, zmax):
    worst = cs[0] + zmax[0]
    for c, zm in zip(cs[1:], zmax[1:]):
        worst = jnp.maximum(worst, c + zm)
    return jnp.max(worst) > SB_DEAD


def sb_attention_fwd(qkv, tt):
    L = qkv.shape[0]
    nb = L // BLOCK
    NP = SB_PAIRS_FWD
    SB_W = NP * BLOCK
    nq = SB_WIDTH // SB_W

    def body(q_ref, k_ref, v_ref, tt_ref, o_ref, kmax_ref, acc_ref, c_ref):
        i = pl.program_id(1)
        half0 = lax.broadcasted_iota(jnp.int32, (1, BLOCK), 1) < HEAD_DIM
        heads = _sb_heads(NP)
        qh = _masked_heads(q_ref[...].astype(F32) * SCALE, half0, NP)
        zmax = _sb_logit_bounds(i, qh, k_ref, kmax_ref, half0, heads)
        acc_ref[...] = jnp.zeros_like(acc_ref)
        c_ref[...] = jnp.zeros_like(c_ref)

        def rows_of(j0, nblk):
            return pl.ds(pl.multiple_of(j0 * BLOCK, BLOCK), nblk * BLOCK)

        def tile(j0, nblk, mask):
            rows = rows_of(j0, nblk)
            zs = [_dot_nt(qh[h], k_ref[rows, _lanes(pr)]) for h, (pr, a) in enumerate(heads)]
            ss = [_softplus(z) for z in zs]
            if mask is not None:
                ss = [jnp.where(mask, s, 0.0) for s in ss]
            ws, cs = _sb_weights(zs, ss, [c_ref[h] for h in range(len(heads))], tt_ref[...], nblk, mask)
            for h, (pr, a) in enumerate(heads):
                acc_ref[h] += _dot(ws[h].astype(BF16), v_ref[rows, _lanes(pr)])
                c_ref[h] = cs[h]
            return _sb_alive(cs, zmax)

        _sb_sweep(i, tile)
        o_ref[...] = _cat([jnp.where(half0, acc_ref[2 * pr], acc_ref[2 * pr + 1]) for pr in range(NP)], 1)

    panel = lambda c: pl.BlockSpec((L, SB_W), lambda p, i: (0, c * nq + p), pipeline_mode=pl.Buffered(1))
    return pl.pallas_call(
        body,
        name="sb_attention_fwd",
        grid=(nq, nb),
        in_specs=[
            pl.BlockSpec((BLOCK, SB_W), lambda p, i: (i, p)),
            panel(1), panel(2),
            pl.BlockSpec((2 * BLOCK, 2 * BLOCK), lambda p, i: (0, 0)),
        ],
        out_specs=pl.BlockSpec((BLOCK, SB_W), lambda p, i: (i, p)),
        out_shape=jax.ShapeDtypeStruct((L, SB_WIDTH), F32),
        scratch_shapes=[pltpu.SMEM((2 * NP,), F32), pltpu.VMEM((2 * NP, BLOCK, BLOCK), F32),
                        pltpu.VMEM((2 * NP, BLOCK, BLOCK), F32)],
        compiler_params=_params(("arbitrary", "arbitrary")),
    )(qkv, qkv, qkv, tt)


def sb_attention_bwd(qkv, o_sb, do_sb, tt):
    L = qkv.shape[0]
    nb = L // BLOCK
    NP = SB_PAIRS_BWD
    SB_W = NP * BLOCK
    nq = SB_WIDTH // SB_W

    def body(q_ref, k_ref, v_ref, o_ref, do_ref, tt_ref, dq_ref, dk_ref, dv_ref, dk_acc, dv_acc, kmax_ref,
             acc_ref, c_ref, ce_ref):
        i = pl.program_id(1)
        half0 = lax.broadcasted_iota(jnp.int32, (1, BLOCK), 1) < HEAD_DIM
        heads = _sb_heads(NP)

        @pl.when(i == 0)
        def _():
            dk_acc[...] = jnp.zeros_like(dk_acc)
            dv_acc[...] = jnp.zeros_like(dv_acc)

        qh = _masked_heads(q_ref[...].astype(F32) * SCALE, half0, NP)
        zmax = _sb_logit_bounds(i, qh, k_ref, kmax_ref, half0, heads)
        do = do_ref[...]
        doh = _masked_heads(do, half0, NP)
        od = o_ref[...] * do.astype(F32)
        acc_ref[...] = jnp.zeros_like(acc_ref)
        c_ref[...] = jnp.zeros_like(c_ref)
        for h, (pr, a) in enumerate(heads):
            x = od[:, _lanes(pr)]
            x = jnp.where(half0, x, 0.0) if a == 0 else jnp.where(half0, 0.0, x)
            ce_ref[h] = jnp.broadcast_to(jnp.sum(x, axis=1, keepdims=True), (BLOCK, BLOCK))

        def rows_of(j0, nblk):
            return pl.ds(pl.multiple_of(j0 * BLOCK, BLOCK), nblk * BLOCK)

        def tile(j0, nblk, mask):
            cs = [c_ref[h] for h in range(len(heads))]
            ces = [ce_ref[h] for h in range(len(heads))]
            rows = rows_of(j0, nblk)
            zs = [_dot_nt(qh[h], k_ref[rows, _lanes(pr)]) for h, (pr, a) in enumerate(heads)]
            dws = [_dot_nt(doh[h], v_ref[rows, _lanes(pr)]) for h, (pr, a) in enumerate(heads)]
            ss = [_softplus(z) for z in zs]
            sigs = [jnp.exp(z - s) for z, s in zip(zs, ss)]
            if mask is not None:
                ss = [jnp.where(mask, s, 0.0) for s in ss]
            ws, cs = _sb_weights(zs, ss, cs, tt_ref[...], nblk, mask)
            wbs = [w.astype(BF16) for w in ws]
            es = [wb.astype(F32) * dw for wb, dw in zip(wbs, dws)]
            esuf = _dot(_cat([_split_bf16(e[:, b * BLOCK:(b + 1) * BLOCK]) for e in es for b in range(nblk)], 0),
                        tt_ref[...])
            out_ces, dzbs = [], []
            for h, (pr, a) in enumerate(heads):
                ce = ces[h]
                dzs = [None] * nblk
                for b in reversed(range(nblk)):
                    eab = esuf[(h * nblk + b) * BLOCK:(h * nblk + b + 1) * BLOCK]
                    sl = slice(b * BLOCK, (b + 1) * BLOCK)
                    e = es[h][:, sl]
                    dzs[b] = e - sigs[h][:, sl] * (e + (ce - eab[:, :BLOCK]))
                    ce = ce - eab[:, BLOCK:]
                dz = _cat(dzs, 1)
                if mask is not None:
                    dz = jnp.where(mask, dz, 0.0)
                dzbs.append(dz.astype(BF16))
                out_ces.append(ce)
            for h, (pr, a) in enumerate(heads):
                acc_ref[h] += _dot(dzbs[h], k_ref[rows, _lanes(pr)])
                c_ref[h] = cs[h]
                ce_ref[h] = out_ces[h]
            dkv = [_dot_tn(_cat([dzbs[2 * pr], wbs[2 * pr], dzbs[2 * pr + 1], wbs[2 * pr + 1]], 0), qdo[pr])
                   for pr in range(NP)]
            dk_acc[rows, :] += _cat([x[:, :BLOCK] for x in dkv], 1)
            dv_acc[rows, :] += _cat([x[:, BLOCK:] for x in dkv], 1)
            return _sb_alive(cs, zmax)

        zb = jnp.zeros((BLOCK, BLOCK), BF16)
        qdo = [_cat([_cat([qh[h], zb], 1) if kind == 0 else _cat([zb, doh[h]], 1)
                     for h in (2 * pr, 2 * pr + 1) for kind in (0, 1)], 0) for pr in range(NP)]
        _sb_sweep(i, tile)
        dq_ref[...] = (_cat([jnp.where(half0, acc_ref[2 * pr], acc_ref[2 * pr + 1]) for pr in range(NP)], 1)
                       * SCALE).astype(BF16)

        @pl.when(i == nb - 1)
        def _():
            dk_ref[...] = dk_acc[...].astype(BF16)
            dv_ref[...] = dv_acc[...].astype(BF16)

    blk = pl.BlockSpec((BLOCK, SB_W), lambda p, i: (i, p))
    panel = pl.BlockSpec((L, SB_W), lambda p, i: (0, p))
    return pl.pallas_call(
        body,
        name="sb_attention_bwd",
        grid=(nq, nb),
        in_specs=[
            blk,
            pl.BlockSpec((L, SB_W), lambda p, i: (0, nq + p), pipeline_mode=pl.Buffered(1)),
            pl.BlockSpec((L, SB_W), lambda p, i: (0, 2 * nq + p), pipeline_mode=pl.Buffered(1)),
            blk, blk,
            pl.BlockSpec((2 * BLOCK, 2 * BLOCK), lambda p, i: (0, 0)),
        ],
        out_specs=[blk, panel, panel],
        out_shape=[jax.ShapeDtypeStruct((L, SB_WIDTH), BF16)] * 3,
        scratch_shapes=[pltpu.VMEM((L, SB_W), F32), pltpu.VMEM((L, SB_W), F32), pltpu.SMEM((2 * NP,), F32)]
        + [pltpu.VMEM((2 * NP, BLOCK, BLOCK), F32)] * 3,
        compiler_params=_params(("arbitrary", "arbitrary")),
    )(qkv, qkv, qkv, o_sb, do_sb, tt)


SWA_PAIRS = SWA_WIDTH // BLOCK
PAIRS_PER_KV = SWA_PAIRS // SWA_KV_HEADS
CB_SWK = SWA_PAIRS
CB_SWV = SWA_PAIRS + 1


def rope_tables(L):
    half = HEAD_DIM // 2
    inv = ROPE_THETA ** (-jnp.arange(half, dtype=F32) / half)
    pos = (jnp.arange(L) - PAD).astype(F32)
    ang = pos[:, None] * inv[None, :]
    reps = BLOCK // half
    return jnp.tile(jnp.cos(ang), (1, reps)), jnp.tile(jnp.sin(ang), (1, reps))


def _rot_half(x):
    lane = lax.broadcasted_iota(jnp.int32, (1, BLOCK), 1)
    first = (lane % HEAD_DIM) < (HEAD_DIM // 2)
    return jnp.where(first, -pltpu.roll(x, BLOCK - HEAD_DIM // 2, axis=1), pltpu.roll(x, HEAD_DIM // 2, axis=1))


def _rope(x, cos, sin):
    return x * cos + _rot_half(x) * sin


def _unrope(x, cos, sin):
    return x * cos - _rot_half(x) * sin


def _swa_specs():
    prev = lambda n: jnp.maximum(n - 1, 0)
    cur = lambda n: n
    blk = lambda f, c: pl.BlockSpec((BLOCK, BLOCK), lambda n: (f(n), c))
    return [
        pl.BlockSpec((BLOCK, SWA_WIDTH), lambda n: (n, 0)),
        blk(prev, CB_SWK), blk(cur, CB_SWK), blk(prev, CB_SWV), blk(cur, CB_SWV),
        blk(prev, 0), blk(cur, 0), blk(prev, 0), blk(cur, 0),
        pl.BlockSpec(memory_space=pltpu.SMEM),
    ]


def _swa_probs(n, q_ref, kp_ref, kc_ref, vp_ref, vc_ref, cp_ref, cc_ref, sp_ref, sc_ref, sink_ref):
    lane = lax.broadcasted_iota(jnp.int32, (1, BLOCK), 1)
    halves = (lane < HEAD_DIM, lane >= HEAD_DIM)
    cosc, sinc = cc_ref[...], sc_ref[...]
    qs = [_rope(q_ref[:, p * BLOCK:(p + 1) * BLOCK], cosc, sinc) * SCALE for p in range(SWA_PAIRS)]
    kb = jnp.concatenate([_rope(kp_ref[...], cp_ref[...], sp_ref[...]), _rope(kc_ref[...], cosc, sinc)], axis=0)
    vb = jnp.concatenate([vp_ref[...], vc_ref[...]], axis=0)
    kv = {True: (kb, vb), False: (pltpu.roll(kb, HEAD_DIM, axis=1), pltpu.roll(vb, HEAD_DIM, axis=1))}
    rows = PAIRS_PER_KV * BLOCK
    r = lax.broadcasted_iota(jnp.int32, (rows, 2 * BLOCK), 0) % BLOCK
    c = lax.broadcasted_iota(jnp.int32, (rows, 2 * BLOCK), 1)
    valid = (c > r) & (c <= r + BLOCK) & ((n - 1) * BLOCK + c >= PAD)
    combos = [(g, a) for g in range(SWA_KV_HEADS) for a in range(2)]
    qst, ksel, vsel, scores = {}, {}, {}, {}
    for g, a in combos:
        qst[g, a] = jnp.concatenate(
            [jnp.where(halves[a], qs[g * PAIRS_PER_KV + j], 0.0) for j in range(PAIRS_PER_KV)], axis=0).astype(BF16)
        ksel[g, a], vsel[g, a] = kv[g == a]
    for g, a in combos:
        scores[g, a] = _dot_nt(qst[g, a], ksel[g, a].astype(BF16))
    out = {}
    for g, a in combos:
        s = jnp.where(valid, scores[g, a], -1e30)
        sink = jnp.concatenate([jnp.full((BLOCK, BLOCK), sink_ref[0, 2 * (g * PAIRS_PER_KV + j) + a], F32)
                                for j in range(PAIRS_PER_KV)], axis=0)
        mx = jnp.maximum(jnp.max(s, axis=1, keepdims=True), sink)
        pe = jnp.exp(s - _twice(mx))
        es = jnp.exp(sink - mx)
        inv = 1.0 / (_row_sums(pe) + es)
        out[g, a] = (qst[g, a], ksel[g, a], vsel[g, a], pe * _twice(inv), es * inv, halves[a])
    return combos, out


def _twice(x):
    return jnp.concatenate([x, x], axis=1)


def _row_sums(x):
    return _dot(_split_bf16(x), jnp.ones((4 * BLOCK, BLOCK), BF16))


def swa_attention_fwd(proj, cos, sin, sinks):
    L = proj.shape[0]

    def body(q_ref, kp_ref, kc_ref, vp_ref, vc_ref, cp_ref, cc_ref, sp_ref, sc_ref, sink_ref, o_ref):
        n = pl.program_id(0)
        combos, parts = _swa_probs(n, q_ref, kp_ref, kc_ref, vp_ref, vc_ref, cp_ref, cc_ref, sp_ref, sc_ref, sink_ref)
        outs = {}
        for g, a in combos:
            qst, ksel, vsel, probs, psink, half = parts[g, a]
            outs[g, a] = _dot(probs.astype(BF16), jnp.where(half, vsel, 0.0).astype(BF16))
        for g in range(SWA_KV_HEADS):
            both = outs[g, 0] + outs[g, 1]
            for j in range(PAIRS_PER_KV):
                p = g * PAIRS_PER_KV + j
                o_ref[:, p * BLOCK:(p + 1) * BLOCK] = both[j * BLOCK:(j + 1) * BLOCK]

    return pl.pallas_call(
        body,
        name="swa_attention_fwd",
        grid=(L // BLOCK,),
        in_specs=_swa_specs(),
        out_specs=pl.BlockSpec((BLOCK, SWA_WIDTH), lambda n: (n, 0)),
        out_shape=jax.ShapeDtypeStruct((L, SWA_WIDTH), F32),
        compiler_params=_params(("arbitrary",)),
    )(proj, proj, proj, proj, proj, cos, cos, sin, sin, sinks)


def swa_attention_bwd(proj, cos, sin, sinks, do_sw):
    L = proj.shape[0]

    def body(q_ref, kp_ref, kc_ref, vp_ref, vc_ref, cp_ref, cc_ref, sp_ref, sc_ref, sink_ref, do_ref,
             dq_ref, dk_ref, dv_ref, ds_ref):
        n = pl.program_id(0)

        @pl.when(n == 0)
        def _():
            dk_ref[...] = jnp.zeros_like(dk_ref)
            dv_ref[...] = jnp.zeros_like(dv_ref)
            ds_ref[...] = jnp.zeros_like(ds_ref)

        combos, parts = _swa_probs(n, q_ref, kp_ref, kc_ref, vp_ref, vc_ref, cp_ref, cc_ref, sp_ref, sc_ref, sink_ref)
        lane8 = lax.broadcasted_iota(jnp.int32, (8, BLOCK), 1)
        dos, dps = {}, {}
        for g, a in combos:
            half = parts[g, a][5]
            dos[g, a] = jnp.concatenate(
                [jnp.where(half, do_ref[:, (g * PAIRS_PER_KV + j) * BLOCK:(g * PAIRS_PER_KV + j + 1) * BLOCK], 0.0)
                 for j in range(PAIRS_PER_KV)], axis=0).astype(BF16)
        for g, a in combos:
            dps[g, a] = _dot_nt(dos[g, a], parts[g, a][2].astype(BF16))
        dqs = {}
        dkb = jnp.zeros((2 * BLOCK, BLOCK), F32)
        dvb = jnp.zeros((2 * BLOCK, BLOCK), F32)
        dsk = jnp.zeros((8, BLOCK), F32)
        for g, a in combos:
            qst, ksel, vsel, probs, psink, half = parts[g, a]
            dp = dps[g, a]
            delta = _row_sums(probs * dp)
            ds = (probs * (dp - _twice(delta))).astype(BF16)
            pd = psink * delta
            for j in range(PAIRS_PER_KV):
                head = 2 * (g * PAIRS_PER_KV + j) + a
                dsk = dsk + jnp.where(lane8 == head, -jnp.sum(pd[j * BLOCK:(j + 1) * BLOCK, :1]), 0.0)
            dqs[g, a] = _dot(ds, jnp.where(half, ksel, 0.0).astype(BF16))
            dk_a = _dot_tn(ds, qst)
            dv_a = _dot_tn(probs.astype(BF16), dos[g, a])
            if g != a:
                dk_a = pltpu.roll(dk_a, HEAD_DIM, axis=1)
                dv_a = pltpu.roll(dv_a, HEAD_DIM, axis=1)
            dkb = dkb + dk_a
            dvb = dvb + dv_a
        cosc, sinc = cc_ref[...], sc_ref[...]
        for g in range(SWA_KV_HEADS):
            both = (dqs[g, 0] + dqs[g, 1]) * SCALE
            for j in range(PAIRS_PER_KV):
                p = g * PAIRS_PER_KV + j
                dq_ref[:, p * BLOCK:(p + 1) * BLOCK] = _unrope(both[j * BLOCK:(j + 1) * BLOCK], cosc, sinc).astype(BF16)
        ds_ref[...] += dsk
        cur = pl.ds(pl.multiple_of(n * BLOCK, BLOCK), BLOCK)
        dk_ref[cur, :] += _unrope(dkb[BLOCK:], cosc, sinc)
        dv_ref[cur, :] += dvb[BLOCK:]

        @pl.when(n > 0)
        def _():
            prv = pl.ds(pl.multiple_of((n - 1) * BLOCK, BLOCK), BLOCK)
            dk_ref[prv, :] += _unrope(dkb[:BLOCK], cp_ref[...], sp_ref[...])
            dv_ref[prv, :] += dvb[:BLOCK]

    whole = lambda n: (0, 0)
    row = pl.BlockSpec((BLOCK, SWA_WIDTH), lambda n: (n, 0))
    return pl.pallas_call(
        body,
        name="swa_attention_bwd",
        grid=(L // BLOCK,),
        in_specs=_swa_specs() + [row],
        out_specs=[row, pl.BlockSpec((L, BLOCK), whole), pl.BlockSpec((L, BLOCK), whole),
                   pl.BlockSpec((8, BLOCK), whole)],
        out_shape=[jax.ShapeDtypeStruct((L, SWA_WIDTH), BF16), jax.ShapeDtypeStruct((L, BLOCK), F32),
                   jax.ShapeDtypeStruct((L, BLOCK), F32), jax.ShapeDtypeStruct((8, BLOCK), F32)],
        compiler_params=_params(("arbitrary",)),
    )(proj, proj, proj, proj, proj, cos, cos, sin, sin, sinks, do_sw)


ROW_TILE = 640
TAIL_ROWS = 320


def _pick(n, cands):
    for c in cands:
        if n % c == 0:
            return c
    raise ValueError(f"no tile for {n}")


def in_proj(h0, gain, w, name, out_dtype):
    L, D = h0.shape
    N = w.shape[1]
    tm = _pick(L, (ROW_TILE, BLOCK))
    tn = _pick(N, (1792, 1536, 1280, 896, 640, 512, 384, 256, 128))

    def body(h_ref, g_ref, w_ref, o_ref, xn_ref):
        @pl.when(pl.program_id(1) == 0)
        def _():
            x = h_ref[...]
            r = lax.rsqrt(jnp.mean(x * x, axis=1, keepdims=True) + RMS_EPS)
            xn_ref[...] = ((x * r) * g_ref[...]).astype(BF16)
        o_ref[...] = _dot(xn_ref[...], w_ref[...]).astype(out_dtype)

    return pl.pallas_call(
        body,
        name=name,
        grid=(L // tm, N // tn),
        in_specs=[pl.BlockSpec((tm, D), lambda i, j: (i, 0)),
                  pl.BlockSpec((1, D), lambda i, j: (0, 0)),
                  pl.BlockSpec((D, tn), lambda i, j: (0, j))],
        out_specs=[pl.BlockSpec((tm, tn), lambda i, j: (i, j)),
                   pl.BlockSpec((tm, D), lambda i, j: (i, 0))],
        out_shape=[jax.ShapeDtypeStruct((L, N), out_dtype), jax.ShapeDtypeStruct((L, D), BF16)],
        compiler_params=_params(("arbitrary", "arbitrary")),
    )(h0, gain, w)


def rows_matmul(a, w, name):
    L, D = a.shape
    N = w.shape[1]
    tm = _pick(L, (ROW_TILE, BLOCK))
    tn = _pick(N, (1792, 1536, 1280, 896, 640, 512, 384, 256, 128))

    def body(a_ref, w_ref, o_ref):
        o_ref[...] = _dot(a_ref[...], w_ref[...])

    return pl.pallas_call(
        body,
        name=name,
        grid=(L // tm, N // tn),
        in_specs=[pl.BlockSpec((tm, D), lambda i, j: (i, 0)), pl.BlockSpec((D, tn), lambda i, j: (0, j))],
        out_specs=pl.BlockSpec((tm, tn), lambda i, j: (i, j)),
        out_shape=jax.ShapeDtypeStruct((L, N), F32),
        compiler_params=_params(("arbitrary", "arbitrary")),
    )(a, w)


def matmul_tn(a, b, name):
    Kd, M = a.shape
    N = b.shape[1]
    tk = _pick(Kd, (ROW_TILE, BLOCK))
    tn = _pick(N, (1280, 1024, 896, 640, 512, 256, 128))
    nk = Kd // tk

    def body(a_ref, b_ref, o_ref):
        k = pl.program_id(1)

        @pl.when(k == 0)
        def _():
            o_ref[...] = jnp.zeros_like(o_ref)
        o_ref[...] += _dot_tn(a_ref[...], b_ref[...])

    return pl.pallas_call(
        body,
        name=name,
        grid=(N // tn, nk),
        in_specs=[pl.BlockSpec((tk, M), lambda j, k: (k, 0)),
                  pl.BlockSpec((tk, tn), lambda j, k: (k, j))],
        out_specs=pl.BlockSpec((M, tn), lambda j, k: (0, j)),
        out_shape=jax.ShapeDtypeStruct((M, N), F32),
        compiler_params=_params(("arbitrary", "arbitrary")),
    )(a, b)


def in_proj_bwd(dproj, w, h0, gain, dh1):
    L, N = dproj.shape
    D = w.shape[0]
    tm = _pick(L, (ROW_TILE, BLOCK))
    tk = _pick(N, (3200, 1280, 640, 512, 256, 128))
    nk = N // tk

    def body(dp_ref, w_ref, h_ref, g_ref, dh1_ref, dh0_ref, dg_ref, acc_ref):
        i, k = pl.program_id(0), pl.program_id(1)

        @pl.when(k == 0)
        def _():
            acc_ref[...] = jnp.zeros_like(acc_ref)

        @pl.when((i == 0) & (k == 0))
        def _():
            dg_ref[...] = jnp.zeros_like(dg_ref)

        acc_ref[...] += _dot_nt(dp_ref[...], w_ref[...])

        @pl.when(k == nk - 1)
        def _():
            x = h_ref[...]
            r = lax.rsqrt(jnp.mean(x * x, axis=1, keepdims=True) + RMS_EPS)
            xhat = x * r
            dxn = acc_ref[...]
            dg_ref[...] += jnp.sum(dxn * xhat, axis=0, keepdims=True)
            dxh = dxn * g_ref[...]
            dh0_ref[...] = r * (dxh - xhat * jnp.mean(dxh * xhat, axis=1, keepdims=True)) + dh1_ref[...]

    row = pl.BlockSpec((tm, D), lambda i, k: (i, 0))
    vec = pl.BlockSpec((1, D), lambda i, k: (0, 0))
    return pl.pallas_call(
        body,
        name="in_proj_bwd",
        grid=(L // tm, nk),
        in_specs=[pl.BlockSpec((tm, tk), lambda i, k: (i, k)),
                  pl.BlockSpec((D, tk), lambda i, k: (0, k)),
                  row, vec, row],
        out_specs=[row, vec],
        out_shape=[jax.ShapeDtypeStruct((L, D), F32), jax.ShapeDtypeStruct((1, D), F32)],
        scratch_shapes=[pltpu.VMEM((tm, D), F32)],
        compiler_params=_params(("arbitrary", "arbitrary")),
    )(dproj, w, h0, gain, dh1)


def tail_fwd_bwd(h0, tgt, o_sb, o_sw, gates, w_bsb, w_bswa, w_out, gain_f):
    L, D = h0.shape
    R = _pick(L, (TAIL_ROWS, BLOCK))
    z0, z1, z2, z3 = 0, SB_WIDTH, SB_WIDTH + SWA_WIDTH, SB_WIDTH + SWA_WIDTH + D_MODEL

    def body(h_ref, t_ref, osb_ref, osw_ref, g_ref, wsb_ref, wsw_ref, wo_ref, gf_ref,
             dosb_ref, dosw_ref, dg_ref, dh1_ref, mb_ref, usb_ref, usw_ref, dh1b_ref, dysb_ref, dysw_ref,
             dgf_ref, loss_ref):
        i = pl.program_id(0)

        @pl.when(i == 0)
        def _():
            dgf_ref[...] = jnp.zeros_like(dgf_ref)
            loss_ref[...] = jnp.zeros_like(loss_ref)

        sbz = g_ref[:, z0:z1]
        swz = g_ref[:, z1:z2]
        s1 = jax.nn.sigmoid(g_ref[:, z2:z3])
        s2 = jax.nn.sigmoid(g_ref[:, z3:])
        sg_sb = jax.nn.sigmoid(sbz)
        sg_sw = jax.nn.sigmoid(swz)
        silu_sb = sbz * sg_sb
        silu_sw = swz * sg_sw
        osb = osb_ref[...]
        osw = osw_ref[...]
        usb = (osb * silu_sb).astype(BF16)
        usw = (osw * silu_sw).astype(BF16)
        y_sb = _dot(usb, wsb_ref[...])
        y_sw = _dot(usw, wsw_ref[...])
        mb = (s1 * y_sb + s2 * y_sw).astype(BF16)
        h1 = h_ref[...] + _dot(mb, wo_ref[...])
        rf = lax.rsqrt(jnp.mean(h1 * h1, axis=1, keepdims=True) + RMS_EPS)
        hhat = h1 * rf
        gf = gf_ref[...]
        row = i * R + lax.broadcasted_iota(jnp.int32, (R, 1), 0)
        err = jnp.where(row >= BLOCK, hhat * gf - t_ref[...], 0.0)
        lane0 = (lax.broadcasted_iota(jnp.int32, (8, BLOCK), 0) == 0) & (lax.broadcasted_iota(jnp.int32, (8, BLOCK), 1) == 0)
        loss_ref[...] += jnp.where(lane0, (0.5 / D) * jnp.sum(err * err), 0.0)
        dy = err * (1.0 / D)
        dgf_ref[...] += jnp.sum(dy * hhat, axis=0, keepdims=True)
        dhh = dy * gf
        dh1 = rf * (dhh - hhat * jnp.mean(dhh * hhat, axis=1, keepdims=True))
        dh1b = dh1.astype(BF16)
        dm = _dot_nt(dh1b, wo_ref[...])
        dysb = (dm * s1).astype(BF16)
        dysw = (dm * s2).astype(BF16)
        dusb = _dot_nt(dysb, wsb_ref[...])
        dusw = _dot_nt(dysw, wsw_ref[...])
        dosb_ref[...] = (dusb * silu_sb).astype(BF16)
        dosw_ref[...] = (dusw * silu_sw).astype(BF16)
        dg_ref[:, z0:z1] = (dusb * osb * (sg_sb * (1.0 + sbz * (1.0 - sg_sb)))).astype(BF16)
        dg_ref[:, z1:z2] = (dusw * osw * (sg_sw * (1.0 + swz * (1.0 - sg_sw)))).astype(BF16)
        dg_ref[:, z2:z3] = (dm * y_sb * (s1 * (1.0 - s1))).astype(BF16)
        dg_ref[:, z3:] = (dm * y_sw * (s2 * (1.0 - s2))).astype(BF16)
        dh1_ref[...] = dh1
        mb_ref[...] = mb
        usb_ref[...] = usb
        usw_ref[...] = usw
        dh1b_ref[...] = dh1b
        dysb_ref[...] = dysb
        dysw_ref[...] = dysw

    def rows(n):
        return pl.BlockSpec((R, n), lambda i: (i, 0))

    def whole(shape):
        return pl.BlockSpec(shape, lambda i: (0, 0))

    GW = gates.shape[1]
    return pl.pallas_call(
        body,
        name="tail_fwd_bwd",
        grid=(L // R,),
        in_specs=[rows(D), rows(D), rows(SB_WIDTH), rows(SWA_WIDTH), rows(GW),
                  whole(w_bsb.shape), whole(w_bswa.shape), whole(w_out.shape), whole((1, D))],
        out_specs=[rows(SB_WIDTH), rows(SWA_WIDTH), rows(GW), rows(D),
                   rows(D), rows(SB_WIDTH), rows(SWA_WIDTH), rows(D), rows(D), rows(D),
                   whole((1, D)), whole((8, BLOCK))],
        out_shape=[jax.ShapeDtypeStruct((L, SB_WIDTH), BF16), jax.ShapeDtypeStruct((L, SWA_WIDTH), BF16),
                   jax.ShapeDtypeStruct((L, GW), BF16), jax.ShapeDtypeStruct((L, D), F32),
                   jax.ShapeDtypeStruct((L, D), BF16), jax.ShapeDtypeStruct((L, SB_WIDTH), BF16),
                   jax.ShapeDtypeStruct((L, SWA_WIDTH), BF16), jax.ShapeDtypeStruct((L, D), BF16),
                   jax.ShapeDtypeStruct((L, D), BF16), jax.ShapeDtypeStruct((L, D), BF16),
                   jax.ShapeDtypeStruct((1, D), F32), jax.ShapeDtypeStruct((8, BLOCK), F32)],
        compiler_params=_params(("arbitrary",)),
    )(h0, tgt, o_sb, o_sw, gates, w_bsb, w_bswa, w_out, gain_f)


def local_step(x, tgt, meta, gain, w_in, w_bsb, w_bswa, w_out, sinks, gain_f):
    S, D = x.shape
    L = S + BLOCK
    h0 = jnp.concatenate([jnp.zeros((PAD, D), F32), meta, x], axis=0)
    tgt_p = jnp.concatenate([jnp.zeros((BLOCK, D), F32), tgt], axis=0)
    tt = _suffix_matrix()
    cos, sin = rope_tables(L)
    qkv, xn = in_proj(h0, gain, w_in[:, :SB_COLS], "in_proj_sb", BF16)
    proj_sw = rows_matmul(xn, w_in[:, SB_COLS:GATE_COL0], "in_proj_swa")
    gates = rows_matmul(xn, w_in[:, GATE_COL0:], "in_proj_gates")
    o_sb = sb_attention_fwd(qkv, tt)
    o_sw = swa_attention_fwd(proj_sw, cos, sin, sinks)
    (do_sb, do_sw, dgates, dh1, mb, usb, usw, dh1b, dysb, dysw, dgf, loss) = tail_fwd_bwd(
        h0, tgt_p, o_sb, o_sw, gates, w_bsb, w_bswa, w_out, gain_f)
    dq_sb, dk_sb, dv_sb = sb_attention_bwd(qkv, o_sb, do_sb, tt)
    dq_sw, dk_sw, dv_sw, dsinks = swa_attention_bwd(proj_sw, cos, sin, sinks, do_sw)
    dproj = jnp.concatenate([dq_sb, dk_sb, dv_sb, dq_sw, dk_sw.astype(BF16), dv_sw.astype(BF16), dgates], axis=1)
    dw_in = matmul_tn(xn, dproj, "dw_in")
    dw_out = matmul_tn(mb, dh1b, "dw_out")
    dw_bsb = matmul_tn(usb, dysb, "dw_bsb")
    dw_bswa = matmul_tn(usw, dysw, "dw_bswa")
    dh0, dgain = in_proj_bwd(dproj, w_in, h0, gain, dh1)
    return (loss[0, 0], dh0[BLOCK:], dh0[PAD:BLOCK], dgain, dw_in, dw_bsb, dw_bswa, dw_out,
            dsinks[:1, :SWA_Q_HEADS], dgf)


MESH_IDS = pl.DeviceIdType.MESH
ANY = pl.BlockSpec(memory_space=pl.ANY)


def _place():
    return lax.axis_index("x"), lax.axis_index("y"), lax.axis_index("c")


def _index(x, y, c):
    return 4 * x + 2 * y + c


def all_gather(block, name):
    def body(x_ref, out_ref, send_sems, recv_sems, local_sem):
        x, y, c = _place()
        me, sibling = (x, y, c), (x, y, 1 - c)
        chips = [(1 - x, y), (x, 1 - y), (1 - x, 1 - y)]

        def copy(k, blk, to, src=None):
            dst = out_ref.at[_index(*blk)]
            return pltpu.make_async_remote_copy(
                src_ref=dst if src is None else src, dst_ref=dst,
                send_sem=send_sems.at[k], recv_sem=recv_sems.at[k], device_id=to, device_id_type=MESH_IDS)

        mine = pltpu.make_async_copy(x_ref, out_ref.at[_index(*me)], local_sem)
        mine.start()
        first = [copy(0, me, sibling, src=x_ref)]
        first += [copy(1 + j, me, (*chip, c), src=x_ref) for j, chip in enumerate(chips)]
        for cp in first:
            cp.start()
        passed = [copy(4 + j, (*chip, c), sibling) for j, chip in enumerate(chips)]
        for j, chip in enumerate(chips):
            copy(1 + j, (*chip, c), me).wait_recv()
            passed[j].start()
        copy(0, sibling, me).wait_recv()
        for j, chip in enumerate(chips):
            copy(4 + j, (*chip, 1 - c), me).wait_recv()
        for cp in first + passed:
            cp.wait_send()
        mine.wait()

    return pl.pallas_call(
        body,
        name=name,
        out_shape=jax.ShapeDtypeStruct((N_DEV,) + block.shape, block.dtype),
        in_specs=[ANY],
        out_specs=ANY,
        scratch_shapes=[pltpu.SemaphoreType.DMA((7,)), pltpu.SemaphoreType.DMA((7,)), pltpu.SemaphoreType.DMA],
    )(block)


def exchange_partials(parts):
    def body(g_ref, out_ref, send_sems, recv_sems, local_sem):
        x, y, c = _place()
        me = _index(x, y, c)
        mine = pltpu.make_async_copy(g_ref.at[me], out_ref.at[me], local_sem)
        mine.start()
        copies = []
        for m in (6, 7, 4, 5, 2, 3, 1):
            px = 1 - x if m & 4 else x
            py = 1 - y if m & 2 else y
            pc = 1 - c if m & 1 else c
            cp = pltpu.make_async_remote_copy(
                src_ref=g_ref.at[_index(px, py, pc)], dst_ref=out_ref.at[me],
                send_sem=send_sems.at[m - 1], recv_sem=recv_sems.at[m - 1],
                device_id=(px, py, pc), device_id_type=MESH_IDS)
            cp.start()
            copies.append(cp)
        for cp in copies:
            cp.wait()
        mine.wait()

    return pl.pallas_call(
        body,
        name="exchange_partials",
        out_shape=jax.ShapeDtypeStruct(parts.shape, parts.dtype),
        in_specs=[ANY],
        out_specs=ANY,
        scratch_shapes=[pltpu.SemaphoreType.DMA((7,)), pltpu.SemaphoreType.DMA((7,)), pltpu.SemaphoreType.DMA],
    )(parts)


def _adamw(w, g, m, v):
    m = ADAM_B1 * m + (1.0 - ADAM_B1) * g
    v = ADAM_B2 * v + (1.0 - ADAM_B2) * (g * g)
    m_hat = m / (1.0 - ADAM_B1 ** ADAM_STEP)
    v_hat = v / (1.0 - ADAM_B2 ** ADAM_STEP)
    delta = -ADAM_LR * (m_hat / (jnp.sqrt(v_hat) + ADAM_EPS) + ADAM_WD * w)
    return delta, m, v


def sum_and_adamw(parts, w, m, v, name):
    _, R, C = parts.shape
    tr = _pick(R, (528, 512, 256, 128, 24, 8))

    def body(p_ref, w_ref, m_ref, v_ref, g_ref, d_ref, nm_ref, nv_ref):
        g = p_ref[0].astype(F32)
        for s in range(1, N_DEV):
            g = g + p_ref[s].astype(F32)
        d, nm, nv = _adamw(w_ref[...], g, m_ref[...], v_ref[...])
        g_ref[...] = g
        d_ref[...] = d
        nm_ref[...] = nm
        nv_ref[...] = nv

    row = pl.BlockSpec((tr, C), lambda i: (i, 0))
    return pl.pallas_call(
        body,
        name=name,
        grid=(R // tr,),
        in_specs=[pl.BlockSpec((N_DEV, tr, C), lambda i: (0, i, 0)), row, row, row],
        out_specs=[row, row, row, row],
        out_shape=[jax.ShapeDtypeStruct((R, C), F32)] * 4,
        compiler_params=_params(("arbitrary",)),
    )(parts, w, m, v)


W_IN_SHARD = IN_COLS // N_DEV
ROWS_W_IN = D_MODEL * W_IN_SHARD // BLOCK
ROWS_W_BSB = SB_WIDTH
ROWS_W_ROWSHARD = D_MODEL
SMALL_ROWS = 24


def _pack_shards(w_in, w_bsb, w_bswa, w_out, meta):
    return jnp.concatenate([w_in.reshape(ROWS_W_IN, BLOCK), w_bsb.reshape(ROWS_W_BSB, BLOCK),
                            w_bswa.reshape(ROWS_W_ROWSHARD, BLOCK), w_out.reshape(ROWS_W_ROWSHARD, BLOCK),
                            meta.reshape(N_META, BLOCK)], axis=0)


def _unpack_shards(p):
    o = np.cumsum([0, ROWS_W_IN, ROWS_W_BSB, ROWS_W_ROWSHARD, ROWS_W_ROWSHARD, N_META])
    return (p[o[0]:o[1]].reshape(1, D_MODEL, W_IN_SHARD), p[o[1]:o[2]].reshape(1, SB_WIDTH, BLOCK),
            p[o[2]:o[3]].reshape(1, BLOCK, D_MODEL), p[o[3]:o[4]].reshape(1, BLOCK, D_MODEL),
            p[o[4]:o[5]].reshape(N_META, BLOCK))


def _pack_by_owner(dw_in, dw_bsb, dw_bswa, dw_out, dmeta):
    cols = lambda a: a.reshape(a.shape[0], N_DEV, -1).transpose(1, 0, 2)
    return jnp.concatenate([cols(dw_in).reshape(N_DEV, ROWS_W_IN, BLOCK), cols(dw_bsb),
                            dw_bswa.reshape(N_DEV, ROWS_W_ROWSHARD, BLOCK), dw_out.reshape(N_DEV, ROWS_W_ROWSHARD, BLOCK),
                            cols(dmeta)], axis=1)


def _pack_small(gain, gain_f, sinks, loss):
    z = jnp.zeros((SMALL_ROWS - 16, BLOCK), F32)
    z = z.at[0, :SWA_Q_HEADS].set(sinks.reshape(-1)).at[1, 0].set(loss)
    return jnp.concatenate([gain.reshape(8, BLOCK), gain_f.reshape(8, BLOCK), z], axis=0)


def _unpack_small(p):
    return p[0:8].reshape(1, D_MODEL), p[8:16].reshape(D_MODEL), p[16:17, :SWA_Q_HEADS], p[17, 0]


def kernel(x, meta_tokens, norm_gain, w_in, w_branch_sb, w_branch_swa, w_out, attn_sinks, final_norm_gain, loss_target, m_meta_tokens, m_norm_gain, m_w_in, m_w_branch_sb, m_w_branch_swa, m_w_out, m_attn_sinks, m_final_norm_gain, v_meta_tokens, v_norm_gain, v_w_in, v_w_branch_sb, v_w_branch_swa, v_w_out, v_attn_sinks, v_final_norm_gain):
    meta_bits = lax.bitcast_convert_type(meta_tokens, BF16).reshape(2 * N_META, BLOCK)
    mine = jnp.concatenate([_pack_shards(w_in, w_branch_sb, w_branch_swa, w_out, meta_tokens)[:-N_META].astype(BF16),
                            meta_bits], axis=0)
    full = all_gather(mine, "all_gather_weights")
    o = np.cumsum([0, ROWS_W_IN, ROWS_W_BSB, ROWS_W_ROWSHARD, ROWS_W_ROWSHARD, 2 * N_META])
    cols = lambda a: a.transpose(1, 0, 2).reshape(a.shape[1], -1)
    f_w_in = cols(full[:, o[0]:o[1]].reshape(N_DEV, D_MODEL, W_IN_SHARD))
    f_w_bsb = cols(full[:, o[1]:o[2]])
    f_w_bswa = full[:, o[2]:o[3]].reshape(D_MODEL, D_MODEL)
    f_w_out = full[:, o[3]:o[4]].reshape(D_MODEL, D_MODEL)
    f_meta = cols(lax.bitcast_convert_type(full[:, o[4]:o[5]].reshape(N_DEV, N_META, BLOCK, 2), F32))

    (loss, grad_x, dmeta, dgain, dw_in, dw_bsb, dw_bswa, dw_out, dsinks, dgf) = local_step(
        x[0], loss_target[0], f_meta, norm_gain, f_w_in, f_w_bsb, f_w_bswa, f_w_out, attn_sinks,
        final_norm_gain.reshape(1, D_MODEL))

    parts = exchange_partials(_pack_by_owner(*[g.astype(BF16) for g in (dw_in, dw_bsb, dw_bswa, dw_out, dmeta)]))
    packs = [_pack_shards(a[0], b[0], c[0], d[0], e) for a, b, c, d, e in (
        (w_in, w_branch_sb, w_branch_swa, w_out, meta_tokens),
        (m_w_in, m_w_branch_sb, m_w_branch_swa, m_w_out, m_meta_tokens),
        (v_w_in, v_w_branch_sb, v_w_branch_swa, v_w_out, v_meta_tokens))]
    big = [_unpack_shards(p) for p in sum_and_adamw(parts, *packs, "sum_adamw_sharded")]

    small = all_gather(_pack_small(dgain, dgf, dsinks, loss), "all_gather_small")
    zero = jnp.zeros((), F32)
    spacks = [_pack_small(a, b, c, zero) for a, b, c in (
        (norm_gain, final_norm_gain, attn_sinks), (m_norm_gain, m_final_norm_gain, m_attn_sinks),
        (v_norm_gain, v_final_norm_gain, v_attn_sinks))]
    sm = [_unpack_small(p) for p in sum_and_adamw(small, *spacks, "sum_adamw_replicated")]

    def leaves(k):
        b, s = big[k], sm[k]
        return (b[4], s[0], b[0], b[1], b[2], b[3], s[2], s[1])

    return (sm[0][3], grad_x[None], *leaves(0), *leaves(1), *leaves(2), *leaves(3))
```
